```python
import math
import jax, jax.numpy as jnp
from jax import lax
import numpy as np

D_MODEL = 2048
BATCH = 4
SEQ = 8192
DEPTH = 4

HEAD_DIM = 128
N_HEADS = D_MODEL // HEAD_DIM
A_HEADS = N_HEADS // 2
NSA_HEADS = N_HEADS - A_HEADS
NSA_KV_GROUPS = 2
NSA_REP = NSA_HEADS // NSA_KV_GROUPS
A_WIDTH = A_HEADS * HEAD_DIM
NSA_WIDTH = NSA_HEADS * HEAD_DIM
KV_WIDTH = NSA_KV_GROUPS * HEAD_DIM
N_BRANCH = 3
IN_COLS = 3 * A_WIDTH + NSA_WIDTH + 2 * N_BRANCH * KV_WIDTH + N_BRANCH * NSA_HEADS
D_FF = 5632
DILATED_PATTERNS = ((128, 1), (512, 4), (2048, 16))
BAND = 128
CMP_STRIDE = 16
CMP_LEN = 2 * CMP_STRIDE
CMP_HIDDEN = 512
SLC_LEN = 64
SLC_TOPK = 16
WIN_LEN = 512
NSA_QBLK = 64
N_BUCKETS = 32
MAX_DISTANCE = 2048
FORCE_SCORE = 1e6
NEG_INF = -1e30
EPS = 1e-6

kernel_name = 'hybrid_dilated_nsa_macaron'


def rmsnorm(x, g):
    xf = x.astype(jnp.float32)
    y = xf * lax.rsqrt(jnp.mean(xf * xf, axis=-1, keepdims=True) + EPS)
    return (y * g.astype(jnp.float32)).astype(x.dtype)


def swiglu(x, w1, w3, w2):
    return (jax.nn.silu(x @ w1) * (x @ w3)) @ w2


def rel_bucket(dist):
    n = jnp.maximum(dist, 0)
    max_exact = N_BUCKETS // 2
    nf = jnp.maximum(n, 1).astype(jnp.float32)
    log_b = max_exact + (jnp.log(nf / max_exact) / math.log(MAX_DISTANCE / max_exact) * (N_BUCKETS - max_exact)).astype(jnp.int32)
    return jnp.where(n < max_exact, n, jnp.minimum(log_b, N_BUCKETS - 1))


def masked_probs(logits, mask):
    logits = jnp.where(mask, logits, NEG_INF)
    m = jnp.max(logits, axis=-1, keepdims=True)
    p = jnp.where(mask, jnp.exp(logits - m), 0.0)
    s = jnp.maximum(jnp.sum(p, axis=-1, keepdims=True), 1e-30)
    return p / s, (m + jnp.log(s))[..., 0]


def dilated_pattern(q, k, v, window, dilation, bias_table):
    B, S, H, Dh = q.shape
    span = dilation * BAND
    Sp = -(-S // span) * span
    nb = Sp // span

    def to_sub(t):
        t = jnp.pad(t, ((0, 0), (0, Sp - S), (0, 0), (0, 0)))
        return t.reshape(B, nb, BAND, dilation, H, Dh).transpose(0, 3, 4, 1, 2, 5)

    def with_prev(t):
        prev = jnp.pad(t, ((0, 0), (0, 0), (0, 0), (1, 0), (0, 0), (0, 0)))[:, :, :, :-1]
        return jnp.concatenate([prev, t], axis=4)

    qs = to_sub(q)
    kb = with_prev(to_sub(k))
    vb = with_prev(to_sub(v))
    logits = jnp.einsum('brhnqd,brhnkd->brhnqk', qs, kb).astype(jnp.float32)
    i = jnp.arange(BAND)
    j = jnp.arange(2 * BAND) - BAND
    diff = i[:, None] - j[None, :]
    mask = (diff >= 0) & (diff <= window // dilation)
    mask = mask[None] & ((jnp.arange(nb)[:, None, None] > 0) | (j >= 0)[None, None, :])
    bias = bias_table[rel_bucket(diff * dilation)].astype(jnp.float32).transpose(2, 0, 1)
    probs, lse = masked_probs(logits + bias[None, None, :, None], mask[None, None, None])
    out = jnp.einsum('brhnqk,brhnkd->brhnqd', probs.astype(v.dtype), vb)
    out = out.transpose(0, 3, 4, 1, 2, 5).reshape(B, Sp, H, Dh)[:, :S]
    lse = lse.transpose(0, 3, 4, 1, 2).reshape(B, Sp, H)[:, :S]
    return out, lse


def dilated_attention(q, k, v, bias_table):
    outs, lses = [], []
    for window, dilation in DILATED_PATTERNS:
        o, s = dilated_pattern(q, k, v, window, dilation, bias_table)
        outs.append(o)
        lses.append(s)
    w = jax.nn.softmax(jnp.stack(lses), axis=0)
    return jnp.einsum('pbsh,pbshd->bshd', w.astype(q.dtype), jnp.stack(outs))


def compress(t, pos, w1, w2):
    B, S, G, Dh = t.shape
    ch = t.reshape(B, S // CMP_STRIDE, CMP_STRIDE, G, Dh)
    blocks = jnp.concatenate([ch[:, :-1], ch[:, 1:]], axis=2) + pos[None, None, :, None, :]
    flat = blocks.transpose(0, 1, 3, 2, 4).reshape(B, S // CMP_STRIDE - 1, G, CMP_LEN * Dh)
    return jax.nn.gelu(flat @ w1) @ w2


def nsa_attention(q, kc, vc, ks, vs, kw, vw, gates, bias_table):
    B, S, H, Dh = q.shape
    G, R = NSA_KV_GROUPS, NSA_REP
    n_cmp = kc.shape[1]
    n_slc = S // SLC_LEN
    n_top = min(SLC_TOPK, n_slc)
    cmp_end = jnp.arange(n_cmp) * CMP_STRIDE + CMP_LEN - 1
    c_start = jnp.arange(n_cmp)[:, None] * CMP_STRIDE
    s_start = jnp.arange(n_slc)[None, :] * SLC_LEN
    overlap = ((c_start < s_start + SLC_LEN) & (c_start + CMP_LEN > s_start)).astype(jnp.float32)
    tbl = bias_table.astype(jnp.float32).reshape(N_BUCKETS, G, R).transpose(1, 0, 2)
    ks_t = ks.transpose(0, 2, 1, 3)
    vs_t = vs.transpose(0, 2, 1, 3)
    kw_p = jnp.pad(kw, ((0, 0), (WIN_LEN, 0), (0, 0), (0, 0)))
    vw_p = jnp.pad(vw, ((0, 0), (WIN_LEN, 0), (0, 0), (0, 0)))
    gather_tokens = jax.vmap(jax.vmap(lambda src, ix: src[ix]))
    per_group_bias = jax.vmap(lambda t, b: t[b], in_axes=(0, 1), out_axes=1)

    def block(qi):
        start = qi * NSA_QBLK
        tq = start + jnp.arange(NSA_QBLK)
        qg = lax.dynamic_slice_in_dim(q, start, NSA_QBLK, axis=1).reshape(B, NSA_QBLK, G, R, Dh)
        g = lax.dynamic_slice_in_dim(gates, start, NSA_QBLK, axis=1).reshape(B, NSA_QBLK, G, R, N_BRANCH)
        lc = jnp.einsum('bqgrd,bcgd->bgrqc', qg, kc).astype(jnp.float32)
        pc, _ = masked_probs(lc, cmp_end[None, :] <= tq[:, None])
        o_cmp = jnp.einsum('bgrqc,bcgd->bqgrd', pc.astype(vc.dtype), vc)
        imp = jnp.einsum('bgrqc,cn->bgqn', pc, overlap)
        blk = jnp.arange(n_slc)[None, :]
        cur = (tq // SLC_LEN)[:, None]
        valid = blk * SLC_LEN <= tq[:, None]
        forced = (blk == 0) | (blk == cur) | (blk == cur - 1)
        score = jnp.where(valid & forced, FORCE_SCORE, jnp.where(valid, imp, -1.0))
        _, top = lax.top_k(score, n_top)
        tok = (top[..., None] * SLC_LEN + jnp.arange(SLC_LEN)).reshape(B, G, NSA_QBLK, n_top * SLC_LEN)
        k_sel = gather_tokens(ks_t, tok)
        v_sel = gather_tokens(vs_t, tok)
        dist = tq[None, None, :, None] - tok
        b_sel = per_group_bias(tbl, rel_bucket(dist)).transpose(0, 1, 4, 2, 3)
        ls = jnp.einsum('bqgrd,bgqtd->bgrqt', qg, k_sel).astype(jnp.float32) + b_sel
        ps, _ = masked_probs(ls, (dist >= 0)[:, :, None])
        o_slc = jnp.einsum('bgrqt,bgqtd->bqgrd', ps.astype(vs.dtype), v_sel)
        k_win = lax.dynamic_slice_in_dim(kw_p, start, WIN_LEN + NSA_QBLK, axis=1)
        v_win = lax.dynamic_slice_in_dim(vw_p, start, WIN_LEN + NSA_QBLK, axis=1)
        kpos = start - WIN_LEN + jnp.arange(WIN_LEN + NSA_QBLK)
        dw = tq[:, None] - kpos[None, :]
        wmask = (dw >= 0) & (dw < WIN_LEN) & (kpos >= 0)[None, :]
        b_win = bias_table[rel_bucket(dw)].astype(jnp.float32).reshape(NSA_QBLK, WIN_LEN + NSA_QBLK, G, R).transpose(2, 3, 0, 1)
        lw = jnp.einsum('bqgrd,bkgd->bgrqk', qg, k_win).astype(jnp.float32) + b_win[None]
        pw, _ = masked_probs(lw, wmask)
        o_win = jnp.einsum('bgrqk,bkgd->bqgrd', pw.astype(vw.dtype), v_win)
        out = g[..., 0:1] * o_cmp + g[..., 1:2] * o_slc + g[..., 2:3] * o_win
        return out.reshape(B, NSA_QBLK, H * Dh)

    out = lax.map(block, jnp.arange(S // NSA_QBLK))
    return out.transpose(1, 0, 2, 3).reshape(B, S, H * Dh)


def setup_inputs(seed: int = 0) -> dict:
    key = jax.random.key(seed)
    ks = jax.random.split(key, 24)
    f32 = jnp.float32
    L = DEPTH

    def dense(k, shape, fan_in):
        return jax.random.normal(k, shape, f32) * fan_in ** -0.5

    def gain(k, shape):
        return 1.0 + 0.02 * jax.random.normal(k, shape, f32)

    return {
        'x': jax.random.normal(ks[0], (BATCH, SEQ, D_MODEL), f32),
        'ffn1_norm': gain(ks[1], (L, D_MODEL)),
        'ffn1_w1': dense(ks[2], (L, D_MODEL, D_FF), D_MODEL),
        'ffn1_w3': dense(ks[3], (L, D_MODEL, D_FF), D_MODEL),
        'ffn1_w2': dense(ks[4], (L, D_FF, D_MODEL), D_FF),
        'mix_norm': gain(ks[5], (L, D_MODEL)),
        'w_in': dense(ks[6], (L, D_MODEL, IN_COLS), D_MODEL),
        'gate_bias': 0.1 * jax.random.normal(ks[7], (L, N_BRANCH * NSA_HEADS), f32),
        'q_norm_a': gain(ks[8], (L, HEAD_DIM)),
        'k_norm_a': gain(ks[9], (L, HEAD_DIM)),
        'q_norm_nsa': gain(ks[10], (L, HEAD_DIM)),
        'k_norm_nsa': gain(ks[11], (L, HEAD_DIM)),
        'cmp_pos': 0.2 * jax.random.normal(ks[12], (L, CMP_LEN, HEAD_DIM), f32),
        'cmp_k_w1': dense(ks[13], (L, CMP_LEN * HEAD_DIM, CMP_HIDDEN), CMP_LEN * HEAD_DIM),
        'cmp_k_w2': dense(ks[14], (L, CMP_HIDDEN, HEAD_DIM), CMP_HIDDEN),
        'cmp_v_w1': dense(ks[15], (L, CMP_LEN * HEAD_DIM, CMP_HIDDEN), CMP_LEN * HEAD_DIM),
        'cmp_v_w2': dense(ks[16], (L, CMP_HIDDEN, HEAD_DIM), CMP_HIDDEN),
        'out_norm': gain(ks[17], (L, D_MODEL)),
        'w_out': dense(ks[18], (L, D_MODEL, D_MODEL), D_MODEL),
        'ffn2_norm': gain(ks[19], (L, D_MODEL)),
        'ffn2_w1': dense(ks[20], (L, D_MODEL, D_FF), D_MODEL),
        'ffn2_w3': dense(ks[21], (L, D_MODEL, D_FF), D_MODEL),
        'ffn2_w2': dense(ks[22], (L, D_FF, D_MODEL), D_FF),
        'rel_bias': 0.5 * jax.random.normal(ks[23], (N_BUCKETS, N_HEADS), f32),
    }


def reference(x, ffn1_norm, ffn1_w1, ffn1_w3, ffn1_w2, mix_norm, w_in, gate_bias,
              q_norm_a, k_norm_a, q_norm_nsa, k_norm_nsa, cmp_pos, cmp_k_w1, cmp_k_w2,
              cmp_v_w1, cmp_v_w2, out_norm, w_out, ffn2_norm, ffn2_w1, ffn2_w3, ffn2_w2, rel_bias):
    B, S, _ = x.shape
    scale = HEAD_DIM ** -0.5
    sizes = (A_WIDTH,) * 3 + (NSA_WIDTH,) + (KV_WIDTH,) * (2 * N_BRANCH) + (N_BRANCH * NSA_HEADS,)
    splits = np.cumsum(sizes)[:-1].tolist()

    def heads(t, n):
        return t.reshape(B, S, n, HEAD_DIM)

    for l in range(DEPTH):
        x = x + 0.5 * swiglu(rmsnorm(x, ffn1_norm[l]), ffn1_w1[l], ffn1_w3[l], ffn1_w2[l])
        h = rmsnorm(x, mix_norm[l])
        qa, ka, va, qn, kc, vc, k_s, v_s, k_w, v_w, gl = jnp.split(h @ w_in[l], splits, axis=-1)
        qa = rmsnorm(heads(qa, A_HEADS), q_norm_a[l]) * scale
        ka = rmsnorm(heads(ka, A_HEADS), k_norm_a[l])
        o_a = dilated_attention(qa, ka, heads(va, A_HEADS), rel_bias[:, :A_HEADS]).reshape(B, S, A_WIDTH)
        qn = rmsnorm(heads(qn, NSA_HEADS), q_norm_nsa[l]) * scale
        kc = rmsnorm(compress(heads(kc, NSA_KV_GROUPS), cmp_pos[l], cmp_k_w1[l], cmp_k_w2[l]), k_norm_nsa[l])
        vc = compress(heads(vc, NSA_KV_GROUPS), cmp_pos[l], cmp_v_w1[l], cmp_v_w2[l])
        k_s = rmsnorm(heads(k_s, NSA_KV_GROUPS), k_norm_nsa[l])
        k_w = rmsnorm(heads(k_w, NSA_KV_GROUPS), k_norm_nsa[l])
        gates = jax.nn.sigmoid(gl + gate_bias[l]).reshape(B, S, NSA_HEADS, N_BRANCH)
        o_n = nsa_attention(qn, kc, vc, k_s, heads(v_s, NSA_KV_GROUPS), k_w, heads(v_w, NSA_KV_GROUPS),
                            gates, rel_bias[:, A_HEADS:])
        y = jnp.concatenate([rmsnorm(o_a, out_norm[l, :A_WIDTH]), rmsnorm(o_n, out_norm[l, A_WIDTH:])], axis=-1)
        x = x + y @ w_out[l]
        x = x + 0.5 * swiglu(rmsnorm(x, ffn2_norm[l]), ffn2_w1[l], ffn2_w3[l], ffn2_w2[l])
    return x
```

```python
import functools
import math

import numpy as np
import jax
import jax.numpy as jnp
from jax import lax
from jax.experimental import pallas as pl
from jax.experimental.pallas import tpu as pltpu

D_MODEL = 2048
HEAD_DIM = 128
N_HEADS = 16
A_HEADS = 8
NSA_HEADS = 8
NSA_KV_GROUPS = 2
NSA_REP = 4
A_WIDTH = A_HEADS * HEAD_DIM
NSA_WIDTH = NSA_HEADS * HEAD_DIM
KV_WIDTH = NSA_KV_GROUPS * HEAD_DIM
N_BRANCH = 3
D_FF = 5632
DILATED_PATTERNS = ((128, 1), (512, 4), (2048, 16))
BAND = 128
CMP_STRIDE = 16
CMP_LEN = 32
CMP_HIDDEN = 512
SLC_LEN = 64
SLC_TOPK = 16
WIN_LEN = 512
N_BUCKETS = 32
MAX_DISTANCE = 2048
FORCE_SCORE = 1e6
NEG_INF = -1e30
EPS = 1e-6

LANES = 128
DIL_SPAN = 2048
NSA_TQ = 128
NSA_TK = 128
N_BIAS_TILES = 14
VMEM_LIMIT = 56 * 1024 * 1024

F32 = jnp.float32
BF16 = jnp.bfloat16

P1_QA, P1_KA, P1_VA, P1_KC, P1_VC, P1_GL = 0, 8, 16, 24, 26, 28
P1_HEADS = 30
P2_QN, P2_KS, P2_VS, P2_KW, P2_VW = 0, 8, 10, 12, 14
P2_HEADS = 16


def _params(sem):
    return pltpu.CompilerParams(dimension_semantics=sem, vmem_limit_bytes=VMEM_LIMIT)


def _nt_dot(a, b):
    return lax.dot_general(a, b, (((1,), (1,)), ((), ())), preferred_element_type=F32)


def _dot(a, b):
    return jnp.dot(a, b, preferred_element_type=F32)


def _rel_bucket_np(dist):
    n = np.maximum(dist, 0)
    max_exact = N_BUCKETS // 2
    nf = np.maximum(n, 1).astype(np.float32)
    log_b = max_exact + (np.log(nf / np.float32(max_exact)) / np.float32(math.log(MAX_DISTANCE / max_exact))
                         * np.float32(N_BUCKETS - max_exact)).astype(np.int32)
    return np.where(n < max_exact, n, np.minimum(log_b, N_BUCKETS - 1)).astype(np.int32)


def _dilated_buckets():
    i = np.arange(BAND)[:, None]
    j = np.arange(2 * BAND)[None, :] - BAND
    diff = i - j
    out = []
    for window, dilation in DILATED_PATTERNS:
        mask = (diff >= 0) & (diff <= window // dilation)
        bkt = _rel_bucket_np(diff * dilation)
        out.append(np.where(mask, bkt, -1))
        out.append(np.where(mask & (j >= 0), bkt, -1))
    return np.stack(out).astype(np.int32)


def _nsa_buckets():
    i = np.arange(NSA_TQ)[:, None]
    j = np.arange(NSA_TK)[None, :]
    tiles = [_rel_bucket_np(NSA_TK * m + i - j) for m in range(N_BIAS_TILES)]
    assert (tiles[-1] == N_BUCKETS - 1).all()
    return np.stack(tiles).astype(np.int32)


def _overlap_np(n_cmp_pad):
    c_start = np.arange(n_cmp_pad)[:, None] * CMP_STRIDE
    s_start = np.arange(LANES)[None, :] * SLC_LEN
    return ((c_start < s_start + SLC_LEN) & (c_start + CMP_LEN > s_start)).astype(np.float32)


def _bias_kernel(tbl_ref, bkt_ref, o_ref, *, n_heads):
    bkt = bkt_ref[0]
    for h in range(n_heads):
        acc = jnp.full(bkt.shape, NEG_INF, F32)
        for b in range(N_BUCKETS):
            acc = jnp.where(bkt == b, tbl_ref[b, h], acc)
        o_ref[0, h] = acc


def _expand_bias(table, buckets):
    n_t, rows, width = buckets.shape
    nh = table.shape[1]
    return pl.pallas_call(
        functools.partial(_bias_kernel, n_heads=nh),
        grid=(n_t,),
        in_specs=[pl.BlockSpec(memory_space=pltpu.SMEM),
                  pl.BlockSpec((1, rows, width), lambda t: (t, 0, 0))],
        out_specs=pl.BlockSpec((1, nh, rows, width), lambda t: (t, 0, 0, 0)),
        out_shape=jax.ShapeDtypeStruct((n_t, nh, rows, width), F32),
        compiler_params=_params(("arbitrary",)),
        name="bias_expand",
    )(table, buckets)


def _ffn_kernel(x_ref, g_ref, w1_ref, w3_ref, w2_ref, o_ref, n_scr):
    j = pl.program_id(1)

    @pl.when(j == 0)
    def _():
        x = x_ref[...]
        ms = jnp.mean(x * x, axis=-1, keepdims=True)
        n_scr[...] = (x * lax.rsqrt(ms + EPS) * g_ref[...]).astype(BF16)
        o_ref[...] = jnp.zeros_like(o_ref)

    n = n_scr[...]
    h1 = _dot(n, w1_ref[...])
    h3 = _dot(n, w3_ref[...])
    a = (h1 * jax.nn.sigmoid(h1) * h3).astype(BF16)
    o_ref[...] += _dot(a, w2_ref[...])

    @pl.when(j == pl.num_programs(1) - 1)
    def _():
        o_ref[...] = x_ref[...] + 0.5 * o_ref[...]


def _ffn(x2, gain, w1, w3, w2, *, tm, tf):
    n_tok, d = x2.shape
    f = w1.shape[1]
    return pl.pallas_call(
        _ffn_kernel,
        grid=(n_tok // tm, f // tf),
        in_specs=[pl.BlockSpec((tm, d), lambda i, j: (i, 0)),
                  pl.BlockSpec((1, d), lambda i, j: (0, 0)),
                  pl.BlockSpec((d, tf), lambda i, j: (0, j)),
                  pl.BlockSpec((d, tf), lambda i, j: (0, j)),
                  pl.BlockSpec((tf, d), lambda i, j: (j, 0))],
        out_specs=pl.BlockSpec((tm, d), lambda i, j: (i, 0)),
        out_shape=jax.ShapeDtypeStruct((n_tok, d), F32),
        scratch_shapes=[pltpu.VMEM((tm, d), BF16)],
        compiler_params=_params(("parallel", "arbitrary")),
        name="ffn",
    )(x2, gain.reshape(1, d), w1, w3, w2)


def _proj_kernel(x_ref, g_ref, w_ref, cg_ref, cf_ref, o_ref, n_scr):
    j = pl.program_id(1)

    @pl.when(j == 0)
    def _():
        x = x_ref[...]
        ms = jnp.mean(x * x, axis=-1, keepdims=True)
        n_scr[...] = (x * lax.rsqrt(ms + EPS) * g_ref[...]).astype(BF16)

    acc = _dot(n_scr[...], w_ref[...])
    for h in range(acc.shape[1] // LANES):
        cols = slice(h * LANES, (h + 1) * LANES)
        a = acc[:, cols]
        ms = jnp.mean(a * a, axis=-1, keepdims=True)
        inv = jnp.where(cf_ref[:, cols] > 0.0, lax.rsqrt(ms + EPS), 1.0)
        o_ref[:, cols] = (a * inv * cg_ref[:, cols]).astype(o_ref.dtype)


def _proj(x2, gain, w, col_gain, col_flag, out_dtype, *, tm, tn):
    n_tok, d = x2.shape
    n_cols = w.shape[1]
    return pl.pallas_call(
        _proj_kernel,
        grid=(n_tok // tm, n_cols // tn),
        in_specs=[pl.BlockSpec((tm, d), lambda i, j: (i, 0)),
                  pl.BlockSpec((1, d), lambda i, j: (0, 0)),
                  pl.BlockSpec((d, tn), lambda i, j: (0, j)),
                  pl.BlockSpec((1, tn), lambda i, j: (0, j)),
                  pl.BlockSpec((1, tn), lambda i, j: (0, j))],
        out_specs=pl.BlockSpec((tm, tn), lambda i, j: (i, j)),
        out_shape=jax.ShapeDtypeStruct((n_tok, n_cols), out_dtype),
        scratch_shapes=[pltpu.VMEM((tm, d), BF16)],
        compiler_params=_params(("parallel", "arbitrary")),
        name="in_proj",
    )(x2, gain.reshape(1, d), w, col_gain.reshape(1, n_cols), col_flag.reshape(1, n_cols))


def _dilated_kernel(q_ref, kc_ref, kp_ref, vc_ref, vp_ref, bias_ref, o_ref,
                    k_scr, v_scr, out_scr, lse_scr):
    n = pl.program_id(2)
    k_scr[0:DIL_SPAN, :] = kp_ref[0]
    k_scr[DIL_SPAN:2 * DIL_SPAN, :] = kc_ref[0]
    v_scr[0:DIL_SPAN, :] = vp_ref[0]
    v_scr[DIL_SPAN:2 * DIL_SPAN, :] = vc_ref[0]

    for p, (_, dil) in enumerate(DILATED_PATTERNS):
        span = BAND * dil

        def unit(u, carry, p=p, dil=dil, span=span):
            s = u // dil
            r = u - s * dil
            base = s * span + r
            first = jnp.logical_and(n == 0, s == 0).astype(jnp.int32)
            q = q_ref[0, pl.ds(base, BAND, stride=dil), :].astype(BF16)
            kk = k_scr[pl.ds(DIL_SPAN + base - span, 2 * BAND, stride=dil), :].astype(BF16)
            vv = v_scr[pl.ds(DIL_SPAN + base - span, 2 * BAND, stride=dil), :].astype(BF16)
            lg = _nt_dot(q, kk) + bias_ref[p, first, 0]
            m = jnp.max(lg, axis=-1, keepdims=True)
            e = jnp.exp(lg - m)
            ssum = jnp.sum(e, axis=-1, keepdims=True)
            o = _dot(e.astype(BF16), vv) / ssum
            lse = m + jnp.log(ssum)
            out_scr[p, pl.ds(base, BAND, stride=dil), :] = o
            lse_scr[p, pl.ds(base, BAND, stride=dil), :] = jnp.broadcast_to(lse, (BAND, LANES))
            return carry

        lax.fori_loop(0, DIL_SPAN // BAND, unit, 0)

    chunk = 256

    def combine(c, carry):
        rows = pl.ds(pl.multiple_of(c * chunk, chunk), chunk)
        l0, l1, l2 = lse_scr[0, rows, :], lse_scr[1, rows, :], lse_scr[2, rows, :]
        big = jnp.maximum(jnp.maximum(l0, l1), l2)
        e0, e1, e2 = jnp.exp(l0 - big), jnp.exp(l1 - big), jnp.exp(l2 - big)
        num = e0 * out_scr[0, rows, :] + e1 * out_scr[1, rows, :] + e2 * out_scr[2, rows, :]
        o_ref[0, rows, :] = num / (e0 + e1 + e2)
        return carry

    lax.fori_loop(0, DIL_SPAN // chunk, combine, 0)


def _dilated(p1, bias, batch, seq):
    blk = (1, DIL_SPAN, LANES)
    prev = lambda nn: jnp.maximum(nn - 1, 0)
    return pl.pallas_call(
        _dilated_kernel,
        grid=(batch, A_HEADS, seq // DIL_SPAN),
        in_specs=[pl.BlockSpec(blk, lambda b, h, nn: (b, nn, P1_QA + h)),
                  pl.BlockSpec(blk, lambda b, h, nn: (b, nn, P1_KA + h)),
                  pl.BlockSpec(blk, lambda b, h, nn: (b, prev(nn), P1_KA + h)),
                  pl.BlockSpec(blk, lambda b, h, nn: (b, nn, P1_VA + h)),
                  pl.BlockSpec(blk, lambda b, h, nn: (b, prev(nn), P1_VA + h)),
                  pl.BlockSpec((3, 2, 1, BAND, 2 * BAND), lambda b, h, nn: (0, 0, h, 0, 0))],
        out_specs=pl.BlockSpec(blk, lambda b, h, nn: (b, nn, h)),
        out_shape=jax.ShapeDtypeStruct((batch, seq, A_WIDTH), F32),
        scratch_shapes=[pltpu.VMEM((2 * DIL_SPAN, LANES), F32),
                        pltpu.VMEM((2 * DIL_SPAN, LANES), F32),
                        pltpu.VMEM((3, DIL_SPAN, LANES), F32),
                        pltpu.VMEM((3, DIL_SPAN, LANES), F32)],
        compiler_params=_params(("parallel", "parallel", "arbitrary")),
        name="dilated_attn",
    )(p1, p1, p1, p1, p1, bias)


def _compress_kernel(t_ref, pos_ref, w1_ref, w2_ref, gain_ref, flag_ref, o_ref, lo_scr, hi_scr):
    n_blk = o_ref.shape[3]
    lo_scr[...] = jnp.zeros_like(lo_scr)
    hi_scr[...] = jnp.zeros_like(hi_scr)
    for i in range(CMP_STRIDE):
        rows = t_ref[0, pl.ds(i, n_blk, stride=CMP_STRIDE), :]
        a = (rows + pos_ref[i:i + 1, :]).astype(BF16)
        lo_scr[...] += _dot(a, w1_ref[0, i * HEAD_DIM:(i + 1) * HEAD_DIM, :])
        b = (rows + pos_ref[CMP_STRIDE + i:CMP_STRIDE + i + 1, :]).astype(BF16)
        hi_scr[...] += _dot(b, w1_ref[0, (CMP_STRIDE + i) * HEAD_DIM:(CMP_STRIDE + i + 1) * HEAD_DIM, :])
    hid = lo_scr[...] + pltpu.roll(hi_scr[...], n_blk - 1, 0)
    o = _dot(jax.nn.gelu(hid).astype(BF16), w2_ref[0])
    ms = jnp.mean(o * o, axis=-1, keepdims=True)
    inv = jnp.where(flag_ref[0] > 0.0, lax.rsqrt(ms + EPS), 1.0)
    o_ref[0, 0, 0] = o * inv * gain_ref[0]


def _compress(p1, pos, w1, w2, gain, flag, batch, seq):
    n_blk = seq // CMP_STRIDE
    return pl.pallas_call(
        _compress_kernel,
        grid=(batch, 2, NSA_KV_GROUPS),
        in_specs=[pl.BlockSpec((1, seq, LANES), lambda b, kv, g: (b, 0, P1_KC + 2 * kv + g)),
                  pl.BlockSpec((CMP_LEN, HEAD_DIM), lambda b, kv, g: (0, 0)),
                  pl.BlockSpec((1, CMP_LEN * HEAD_DIM, CMP_HIDDEN), lambda b, kv, g: (kv, 0, 0)),
                  pl.BlockSpec((1, CMP_HIDDEN, HEAD_DIM), lambda b, kv, g: (kv, 0, 0)),
                  pl.BlockSpec((1, 1, HEAD_DIM), lambda b, kv, g: (kv, 0, 0)),
                  pl.BlockSpec((1, 1, HEAD_DIM), lambda b, kv, g: (kv, 0, 0))],
        out_specs=pl.BlockSpec((1, 1, 1, n_blk, HEAD_DIM), lambda b, kv, g: (b, kv, g, 0, 0)),
        out_shape=jax.ShapeDtypeStruct((batch, 2, NSA_KV_GROUPS, n_blk, HEAD_DIM), F32),
        scratch_shapes=[pltpu.VMEM((n_blk, CMP_HIDDEN), F32), pltpu.VMEM((n_blk, CMP_HIDDEN), F32)],
        compiler_params=_params(("parallel", "parallel", "parallel")),
        name="nsa_compress",
    )(p1, pos, w1, w2, gain, flag)


def _nsa_kernel(q_ref, ks_ref, vs_ref, kw_ref, vw_ref, kc_ref, vc_ref, gl_ref, gb_ref,
                bias_ref, ovl_ref, o_ref, m_scr, l_scr, acc_scr):
    qi = pl.program_id(2)
    tq, rep = NSA_TQ, NSA_REP
    n_cmp = kc_ref.shape[3]

    q2 = q_ref[0]
    q4 = jnp.concatenate([q2[:, r * HEAD_DIM:(r + 1) * HEAD_DIM] for r in range(rep)], axis=0)
    t_col = qi * tq + lax.broadcasted_iota(jnp.int32, (tq, 1), 0)

    lc = _nt_dot(q4, kc_ref[0, 0, 0].astype(BF16)).reshape(rep, tq, n_cmp)
    cmp_end = lax.broadcasted_iota(jnp.int32, (1, n_cmp), 1) * CMP_STRIDE + (CMP_LEN - 1)
    cmask = (cmp_end <= t_col)[None]
    lc = jnp.where(cmask, lc, NEG_INF)
    mc = jnp.max(lc, axis=-1, keepdims=True)
    pc = jnp.where(cmask, jnp.exp(lc - mc), 0.0)
    pc = pc / jnp.maximum(jnp.sum(pc, axis=-1, keepdims=True), 1e-30)
    o_cmp = _dot(pc.reshape(rep * tq, n_cmp).astype(BF16), vc_ref[0, 0, 0].astype(BF16))

    pc_sum = pc[0] + pc[1] + pc[2] + pc[3]
    hi = pc_sum.astype(BF16)
    rem = pc_sum - hi.astype(F32)
    mid = rem.astype(BF16)
    low = (rem - mid.astype(F32)).astype(BF16)
    ovl = ovl_ref[...]
    imp = _dot(hi, ovl) + _dot(mid, ovl) + _dot(low, ovl)

    blk = lax.broadcasted_iota(jnp.int32, (1, LANES), 1)
    blk_f = blk.astype(F32)
    cur = t_col >> 6
    valid = blk * SLC_LEN <= t_col
    forced = (blk == 0) | (blk == cur) | (blk == cur - 1)
    score = jnp.where(valid, jnp.where(forced, FORCE_SCORE, imp), -1.0)
    sel = jnp.zeros((tq, LANES), F32)
    for _ in range(SLC_TOPK):
        best = jnp.max(score, axis=-1, keepdims=True)
        idx = jnp.min(jnp.where(score == best, blk_f, float(LANES)), axis=-1, keepdims=True)
        hit = blk_f == idx
        sel = jnp.where(hit, 1.0, sel)
        score = jnp.where(hit, -3e38, score)
    sel_b = sel.astype(BF16)

    lane = lax.broadcasted_iota(jnp.int32, (1, NSA_TK), 1)
    blk_rows = lax.broadcasted_iota(jnp.int32, (LANES, 1), 0)

    def reset():
        m_scr[...] = jnp.full(m_scr.shape, NEG_INF, F32)
        l_scr[...] = jnp.zeros_like(l_scr)
        acc_scr[...] = jnp.zeros_like(acc_scr)

    def flash_step(j, k_ref, v_ref, mask):
        rows = pl.ds(pl.multiple_of(j * NSA_TK, NSA_TK), NSA_TK)
        s = _nt_dot(q4, k_ref[0, rows, :]).reshape(rep, tq, NSA_TK)
        s = s + bias_ref[jnp.minimum(qi - j, N_BIAS_TILES - 1)]
        mask = mask[None]
        s = jnp.where(mask, s, NEG_INF)
        m_old = m_scr[...]
        m_new = jnp.maximum(m_old, jnp.max(s, axis=-1, keepdims=True))
        alpha = jnp.exp(m_old - m_new)
        p = jnp.where(mask, jnp.exp(s - m_new), 0.0)
        l_scr[...] = alpha * l_scr[...] + jnp.sum(p, axis=-1, keepdims=True)
        pv = _dot(p.reshape(rep * tq, NSA_TK).astype(BF16), v_ref[0, rows, :])
        acc_scr[...] = alpha * acc_scr[...] + pv.reshape(rep, tq, HEAD_DIM)
        m_scr[...] = m_new

    def finish():
        return acc_scr[...] / jnp.maximum(l_scr[...], 1e-30)

    def sel_step(j, carry):
        key_blk = 2 * j + (lane >> 6)
        expand = jnp.where(blk_rows == key_blk, 1.0, 0.0).astype(BF16)
        chosen = _dot(sel_b, expand)
        kpos = j * NSA_TK + lane
        mask = jnp.where(kpos <= t_col, chosen, 0.0) > 0.5
        flash_step(j, ks_ref, vs_ref, mask)
        return carry

    reset()
    lax.fori_loop(0, qi + 1, sel_step, 0)
    o_slc = finish()

    def win_step(j, carry):
        dist = t_col - (j * NSA_TK + lane)
        mask = jnp.where(dist >= 0, dist, WIN_LEN) < WIN_LEN
        flash_step(j, kw_ref, vw_ref, mask)
        return carry

    reset()
    lax.fori_loop(jnp.maximum(qi - WIN_LEN // NSA_TK, 0), qi + 1, win_step, 0)
    o_win = finish()

    gates = jax.nn.sigmoid(gl_ref[0] + gb_ref[0])
    o_cmp = o_cmp.reshape(rep, tq, HEAD_DIM)
    for r in range(rep):
        c = r * N_BRANCH
        out = (gates[:, c:c + 1] * o_cmp[r] + gates[:, c + 1:c + 2] * o_slc[r]
               + gates[:, c + 2:c + 3] * o_win[r])
        o_ref[0, :, r * HEAD_DIM:(r + 1) * HEAD_DIM] = out


def _nsa(p1, p2, kcv, gate_bias, bias, overlap, batch, seq):
    tq = NSA_TQ
    n_cmp = seq // CMP_STRIDE
    kv_blk = (1, seq, LANES)
    grp_w = NSA_REP * HEAD_DIM
    return pl.pallas_call(
        _nsa_kernel,
        grid=(batch, NSA_KV_GROUPS, seq // tq),
        in_specs=[pl.BlockSpec((1, tq, grp_w), lambda b, g, i: (b, i, g)),
                  pl.BlockSpec(kv_blk, lambda b, g, i: (b, 0, P2_KS + g)),
                  pl.BlockSpec(kv_blk, lambda b, g, i: (b, 0, P2_VS + g)),
                  pl.BlockSpec(kv_blk, lambda b, g, i: (b, 0, P2_KW + g)),
                  pl.BlockSpec(kv_blk, lambda b, g, i: (b, 0, P2_VW + g)),
                  pl.BlockSpec((1, 1, 1, n_cmp, HEAD_DIM), lambda b, g, i: (b, 0, g, 0, 0)),
                  pl.BlockSpec((1, 1, 1, n_cmp, HEAD_DIM), lambda b, g, i: (b, 1, g, 0, 0)),
                  pl.BlockSpec((1, tq, LANES), lambda b, g, i: (b, i, P1_GL + g)),
                  pl.BlockSpec((1, 1, LANES), lambda b, g, i: (g, 0, 0)),
                  pl.BlockSpec((N_BIAS_TILES, NSA_REP, NSA_TQ, NSA_TK), lambda b, g, i: (0, g, 0, 0)),
                  pl.BlockSpec((n_cmp, LANES), lambda b, g, i: (0, 0))],
        out_specs=pl.BlockSpec((1, tq, grp_w), lambda b, g, i: (b, i, g)),
        out_shape=jax.ShapeDtypeStruct((batch, seq, NSA_WIDTH), F32),
        scratch_shapes=[pltpu.VMEM((NSA_REP, tq, 1), F32),
                        pltpu.VMEM((NSA_REP, tq, 1), F32),
                        pltpu.VMEM((NSA_REP, tq, HEAD_DIM), F32)],
        compiler_params=_params(("parallel", "parallel", "arbitrary")),
        name="nsa_attn",
    )(p2, p2, p2, p2, p2, kcv, kcv, p1, gate_bias, bias, overlap)


def _out_kernel(x_ref, oa_ref, on_ref, ga_ref, gn_ref, w_ref, o_ref):
    def norm(t, g):
        ms = jnp.mean(t * t, axis=-1, keepdims=True)
        return (t * lax.rsqrt(ms + EPS) * g).astype(BF16)

    ya = norm(oa_ref[...], ga_ref[...])
    yn = norm(on_ref[...], gn_ref[...])
    y = _dot(ya, w_ref[0:A_WIDTH, :]) + _dot(yn, w_ref[A_WIDTH:A_WIDTH + NSA_WIDTH, :])
    o_ref[...] = x_ref[...] + y


def _out_proj(x2, oa, on, gain, w, *, tm):
    n_tok, d = x2.shape
    return pl.pallas_call(
        _out_kernel,
        grid=(n_tok // tm,),
        in_specs=[pl.BlockSpec((tm, d), lambda i: (i, 0)),
                  pl.BlockSpec((tm, A_WIDTH), lambda i: (i, 0)),
                  pl.BlockSpec((tm, NSA_WIDTH), lambda i: (i, 0)),
                  pl.BlockSpec((1, A_WIDTH), lambda i: (0, 0)),
                  pl.BlockSpec((1, NSA_WIDTH), lambda i: (0, 0)),
                  pl.BlockSpec((d, d), lambda i: (0, 0))],
        out_specs=pl.BlockSpec((tm, d), lambda i: (i, 0)),
        out_shape=jax.ShapeDtypeStruct((n_tok, d), F32),
        compiler_params=_params(("parallel",)),
        name="out_proj",
    )(x2, oa, on, gain[:A_WIDTH].reshape(1, A_WIDTH), gain[A_WIDTH:].reshape(1, NSA_WIDTH), w)


def _pack_w_in(w_in, gate_bias, q_norm_a, k_norm_a, q_norm_nsa, k_norm_nsa):
    scale = HEAD_DIM ** -0.5
    sizes = (A_WIDTH,) * 3 + (NSA_WIDTH,) + (KV_WIDTH,) * 6 + (N_BRANCH * NSA_HEADS,)
    offs = np.concatenate([[0], np.cumsum(sizes)])
    qa, ka, va, qn, kc, vc, k_s, v_s, k_w, v_w, gl = [w_in[:, offs[i]:offs[i + 1]] for i in range(11)]
    per_grp = NSA_REP * N_BRANCH
    pad = jnp.zeros((w_in.shape[0], LANES - per_grp), w_in.dtype)
    gl_cols = [jnp.concatenate([gl[:, g * per_grp:(g + 1) * per_grp], pad], axis=1) for g in range(NSA_KV_GROUPS)]
    w1 = jnp.concatenate([qa, ka, va, kc, vc] + gl_cols, axis=1).astype(BF16)
    w2 = jnp.concatenate([qn, k_s, v_s, k_w, v_w], axis=1).astype(BF16)

    ones = jnp.ones((HEAD_DIM,), F32)
    tile = lambda v, n: jnp.tile(v, n)
    g1 = jnp.concatenate([tile(q_norm_a * scale, A_HEADS), tile(k_norm_a, A_HEADS), tile(ones, A_HEADS + 6)])
    f1 = jnp.concatenate([jnp.ones((2 * A_WIDTH,), F32), jnp.zeros((A_WIDTH + 6 * HEAD_DIM,), F32)])
    g2 = jnp.concatenate([tile(q_norm_nsa * scale, NSA_HEADS), tile(k_norm_nsa, 2), tile(ones, 2),
                          tile(k_norm_nsa, 2), tile(ones, 2)])
    f2 = jnp.concatenate([jnp.ones((NSA_WIDTH + KV_WIDTH,), F32), jnp.zeros((KV_WIDTH,), F32),
                          jnp.ones((KV_WIDTH,), F32), jnp.zeros((KV_WIDTH,), F32)])
    gb = jnp.stack([jnp.concatenate([gate_bias[g * per_grp:(g + 1) * per_grp],
                                     jnp.zeros((LANES - per_grp,), F32)]) for g in range(NSA_KV_GROUPS)])
    return w1, g1, f1, w2, g2, f2, gb.reshape(NSA_KV_GROUPS, 1, LANES)


def _forward(x, ffn1_norm, ffn1_w1, ffn1_w3, ffn1_w2, mix_norm, w_in, gate_bias, q_norm_a, k_norm_a,
             q_norm_nsa, k_norm_nsa, cmp_pos, cmp_k_w1, cmp_k_w2, cmp_v_w1, cmp_v_w2, out_norm, w_out,
             ffn2_norm, ffn2_w1, ffn2_w3, ffn2_w2, rel_bias, *, tm_ffn, tf_ffn, tm_proj, tm_out):
    batch, seq, d = x.shape
    depth = ffn1_w1.shape[0]
    n_tok = batch * seq
    assert seq % DIL_SPAN == 0 and seq // SLC_LEN <= LANES

    bias_a = _expand_bias(rel_bias[:, :A_HEADS], jnp.asarray(_dilated_buckets()))
    bias_a = bias_a.reshape(3, 2, A_HEADS, BAND, 2 * BAND)
    bias_n = _expand_bias(rel_bias[:, A_HEADS:], jnp.asarray(_nsa_buckets()))
    overlap = jnp.asarray(_overlap_np(seq // CMP_STRIDE)).astype(BF16)

    x2 = x.reshape(n_tok, d)
    for l in range(depth):
        x2 = _ffn(x2, ffn1_norm[l], ffn1_w1[l].astype(BF16), ffn1_w3[l].astype(BF16),
                  ffn1_w2[l].astype(BF16), tm=tm_ffn, tf=tf_ffn)

        w1, g1, f1, w2, g2, f2, gb = _pack_w_in(w_in[l], gate_bias[l], q_norm_a[l], k_norm_a[l],
                                                q_norm_nsa[l], k_norm_nsa[l])
        p1 = _proj(x2, mix_norm[l], w1, g1, f1, F32, tm=tm_proj, tn=640).reshape(batch, seq, P1_HEADS * LANES)
        p2 = _proj(x2, mix_norm[l], w2, g2, f2, BF16, tm=tm_proj, tn=512).reshape(batch, seq, P2_HEADS * LANES)

        o_a = _dilated(p1, bias_a, batch, seq)

        cmp_w1 = jnp.stack([cmp_k_w1[l], cmp_v_w1[l]]).astype(BF16)
        cmp_w2 = jnp.stack([cmp_k_w2[l], cmp_v_w2[l]]).astype(BF16)
        cmp_gain = jnp.stack([k_norm_nsa[l], jnp.ones((HEAD_DIM,), F32)]).reshape(2, 1, HEAD_DIM)
        cmp_flag = jnp.stack([jnp.ones((HEAD_DIM,), F32), jnp.zeros((HEAD_DIM,), F32)]).reshape(2, 1, HEAD_DIM)
        kcv = _compress(p1, cmp_pos[l], cmp_w1, cmp_w2, cmp_gain, cmp_flag, batch, seq)

        o_n = _nsa(p1, p2, kcv, gb, bias_n, overlap, batch, seq)

        x2 = _out_proj(x2, o_a.reshape(n_tok, A_WIDTH), o_n.reshape(n_tok, NSA_WIDTH), out_norm[l],
                       w_out[l].astype(BF16), tm=tm_out)

        x2 = _ffn(x2, ffn2_norm[l], ffn2_w1[l].astype(BF16), ffn2_w3[l].astype(BF16),
                  ffn2_w2[l].astype(BF16), tm=tm_ffn, tf=tf_ffn)
    return x2.reshape(batch, seq, d)


def kernel(x, ffn1_norm, ffn1_w1, ffn1_w3, ffn1_w2, mix_norm, w_in, gate_bias, q_norm_a, k_norm_a, q_norm_nsa, k_norm_nsa, cmp_pos, cmp_k_w1, cmp_k_w2, cmp_v_w1, cmp_v_w2, out_norm, w_out, ffn2_norm, ffn2_w1, ffn2_w3, ffn2_w2, rel_bias):
    return _forward(x, ffn1_norm, ffn1_w1, ffn1_w3, ffn1_w2, mix_norm, w_in, gate_bias, q_norm_a, k_norm_a,
                    q_norm_nsa, k_norm_nsa, cmp_pos, cmp_k_w1, cmp_k_w2, cmp_v_w1, cmp_v_w2, out_norm, w_out,
                    ffn2_norm, ffn2_w1, ffn2_w3, ffn2_w2, rel_bias,
                    tm_ffn=512, tf_ffn=512, tm_proj=512, tm_out=256)
```

```python
import functools
import math

import numpy as np
import jax
import jax.numpy as jnp
from jax import lax
from jax.experimental import pallas as pl
from jax.experimental.pallas import tpu as pltpu

D_MODEL = 2048
HEAD_DIM = 128
N_HEADS = 16
A_HEADS = 8
NSA_HEADS = 8
NSA_KV_GROUPS = 2
NSA_REP = 4
A_WIDTH = A_HEADS * HEAD_DIM
NSA_WIDTH = NSA_HEADS * HEAD_DIM
KV_WIDTH = NSA_KV_GROUPS * HEAD_DIM
N_BRANCH = 3
D_FF = 5632
DILATED_PATTERNS = ((128, 1), (512, 4), (2048, 16))
BAND = 128
CMP_STRIDE = 16
CMP_LEN = 32
CMP_HIDDEN = 512
SLC_LEN = 64
SLC_TOPK = 16
WIN_LEN = 512
N_BUCKETS = 32
MAX_DISTANCE = 2048
FORCE_SCORE = 1e6
NEG_INF = -1e30
EPS = 1e-6
LOG2E = 1.4426950408889634

LANES = 128
DIL_SPAN = 2048
NSA_TQ = 128
NSA_TK = 128
SEL_TK = 256
BIAS_FAR = 13
BIAS_FUTURE = 14
N_BIAS_TILES = 15
MASK_BIG = 1e30
VMEM_LIMIT = 56 * 1024 * 1024

F32 = jnp.float32
BF16 = jnp.bfloat16

P1_QA, P1_KA, P1_VA, P1_KC, P1_VC, P1_GL = 0, 8, 16, 24, 26, 28
P1_HEADS = 30
P2_QN, P2_KS, P2_VS, P2_KW, P2_VW = 0, 8, 10, 12, 14
P2_HEADS = 16


def _params(sem):
    return pltpu.CompilerParams(dimension_semantics=sem, vmem_limit_bytes=VMEM_LIMIT)


def _nt_dot(a, b):
    return lax.dot_general(a, b, (((1,), (1,)), ((), ())), preferred_element_type=F32)


def _dot(a, b):
    return jnp.dot(a, b, preferred_element_type=F32)


def _rel_bucket_np(dist):
    n = np.maximum(dist, 0)
    max_exact = N_BUCKETS // 2
    nf = np.maximum(n, 1).astype(np.float32)
    log_b = max_exact + (np.log(nf / np.float32(max_exact)) / np.float32(math.log(MAX_DISTANCE / max_exact))
                         * np.float32(N_BUCKETS - max_exact)).astype(np.int32)
    return np.where(n < max_exact, n, np.minimum(log_b, N_BUCKETS - 1)).astype(np.int32)


def _dilated_buckets():
    i = np.arange(BAND)[:, None]
    j = np.arange(2 * BAND)[None, :] - BAND
    diff = i - j
    out = []
    for window, dilation in DILATED_PATTERNS:
        mask = (diff >= 0) & (diff <= window // dilation)
        bkt = _rel_bucket_np(diff * dilation)
        out.append(np.where(mask, bkt, -1))
        out.append(np.where(mask & (j >= 0), bkt, -1))
    return np.stack(out).astype(np.int32)


def _nsa_buckets():
    i = np.arange(NSA_TQ)[:, None]
    j = np.arange(NSA_TK)[None, :]
    tiles = []
    for m in range(BIAS_FAR + 1):
        dist = NSA_TK * m + i - j
        tiles.append(np.where(dist >= 0, _rel_bucket_np(dist), -1))
    assert (tiles[BIAS_FAR] == N_BUCKETS - 1).all()
    tiles.append(np.full((NSA_TQ, NSA_TK), -1))
    assert len(tiles) == N_BIAS_TILES and BIAS_FUTURE == N_BIAS_TILES - 1
    return np.stack(tiles).astype(np.int32)


def _overlap_t_np(n_cmp_pad):
    c_start = np.arange(n_cmp_pad)[None, :] * CMP_STRIDE
    s_start = np.arange(LANES)[:, None] * SLC_LEN
    return ((c_start < s_start + SLC_LEN) & (c_start + CMP_LEN > s_start)).astype(np.float32)


def _block_membership_np(seq):
    k_blk = np.arange(seq)[:, None] // SLC_LEN
    return np.where(k_blk == np.arange(LANES)[None, :], MASK_BIG, 0.0).astype(np.float32)


def _bias_kernel(tbl_ref, bkt_ref, o_ref, *, n_heads, scale, shift_far):
    bkt = bkt_ref[0]
    for h in range(n_heads):
        acc = jnp.full(bkt.shape, NEG_INF, F32)
        base = tbl_ref[N_BUCKETS - 1, h] if shift_far else 0.0
        for b in range(N_BUCKETS):
            acc = jnp.where(bkt == b, (tbl_ref[b, h] - base) * scale, acc)
        o_ref[0, h] = acc


def _expand_bias(table, buckets, scale=1.0, shift_far=False):
    n_t, rows, width = buckets.shape
    nh = table.shape[1]
    return pl.pallas_call(
        functools.partial(_bias_kernel, n_heads=nh, scale=scale, shift_far=shift_far),
        grid=(n_t,),
        in_specs=[pl.BlockSpec(memory_space=pltpu.SMEM),
                  pl.BlockSpec((1, rows, width), lambda t: (t, 0, 0))],
        out_specs=pl.BlockSpec((1, nh, rows, width), lambda t: (t, 0, 0, 0)),
        out_shape=jax.ShapeDtypeStruct((n_t, nh, rows, width), F32),
        compiler_params=_params(("arbitrary",)),
        name="bias_expand",
    )(table, buckets)


def _ffn_kernel(x_ref, g_ref, w1_ref, w3_ref, w2_ref, o_ref, n_scr):
    j = pl.program_id(1)

    @pl.when(j == 0)
    def _():
        x = x_ref[...]
        ms = jnp.mean(x * x, axis=-1, keepdims=True)
        n_scr[...] = (x * lax.rsqrt(ms + EPS) * g_ref[...]).astype(BF16)
        o_ref[...] = jnp.zeros_like(o_ref)

    n = n_scr[...]
    h1 = _dot(n, w1_ref[...])
    h3 = _dot(n, w3_ref[...])
    a = (h1 * jax.nn.sigmoid(h1) * h3).astype(BF16)
    o_ref[...] += _dot(a, w2_ref[...])

    @pl.when(j == pl.num_programs(1) - 1)
    def _():
        o_ref[...] = x_ref[...] + 0.5 * o_ref[...]


def _ffn(x2, gain, w1, w3, w2, *, tm, tf):
    n_tok, d = x2.shape
    f = w1.shape[1]
    return pl.pallas_call(
        _ffn_kernel,
        grid=(n_tok // tm, f // tf),
        in_specs=[pl.BlockSpec((tm, d), lambda i, j: (i, 0)),
                  pl.BlockSpec((1, d), lambda i, j: (0, 0)),
                  pl.BlockSpec((d, tf), lambda i, j: (0, j)),
                  pl.BlockSpec((d, tf), lambda i, j: (0, j)),
                  pl.BlockSpec((tf, d), lambda i, j: (j, 0))],
        out_specs=pl.BlockSpec((tm, d), lambda i, j: (i, 0)),
        out_shape=jax.ShapeDtypeStruct((n_tok, d), F32),
        scratch_shapes=[pltpu.VMEM((tm, d), BF16)],
        compiler_params=_params(("arbitrary", "arbitrary")),
        name="ffn",
    )(x2, gain.reshape(1, d), w1, w3, w2)


def _proj_kernel(x_ref, g_ref, w_ref, cg_ref, cf_ref, o_ref, n_scr):
    j = pl.program_id(1)

    @pl.when(j == 0)
    def _():
        x = x_ref[...]
        ms = jnp.mean(x * x, axis=-1, keepdims=True)
        n_scr[...] = (x * lax.rsqrt(ms + EPS) * g_ref[...]).astype(BF16)

    acc = _dot(n_scr[...], w_ref[...])
    for h in range(acc.shape[1] // LANES):
        cols = slice(h * LANES, (h + 1) * LANES)
        a = acc[:, cols]
        ms = jnp.mean(a * a, axis=-1, keepdims=True)
        inv = jnp.where(cf_ref[:, cols] > 0.0, lax.rsqrt(ms + EPS), 1.0)
        o_ref[:, cols] = (a * inv * cg_ref[:, cols]).astype(o_ref.dtype)


def _proj(x2, gain, w, col_gain, col_flag, out_dtype, *, tm, tn):
    n_tok, d = x2.shape
    n_cols = w.shape[1]
    return pl.pallas_call(
        _proj_kernel,
        grid=(n_tok // tm, n_cols // tn),
        in_specs=[pl.BlockSpec((tm, d), lambda i, j: (i, 0)),
                  pl.BlockSpec((1, d), lambda i, j: (0, 0)),
                  pl.BlockSpec((d, tn), lambda i, j: (0, j)),
                  pl.BlockSpec((1, tn), lambda i, j: (0, j)),
                  pl.BlockSpec((1, tn), lambda i, j: (0, j))],
        out_specs=pl.BlockSpec((tm, tn), lambda i, j: (i, j)),
        out_shape=jax.ShapeDtypeStruct((n_tok, n_cols), out_dtype),
        scratch_shapes=[pltpu.VMEM((tm, d), BF16)],
        compiler_params=_params(("arbitrary", "arbitrary")),
        name="in_proj",
    )(x2, gain.reshape(1, d), w, col_gain.reshape(1, n_cols), col_flag.reshape(1, n_cols))


def _dilated_kernel(q_ref, kc_ref, kp_ref, vc_ref, vp_ref, bias_ref, o_ref,
                    k_scr, v_scr, out_scr, lse_scr):
    n = pl.program_id(2)
    k_scr[0:DIL_SPAN, :] = kp_ref[0]
    k_scr[DIL_SPAN:2 * DIL_SPAN, :] = kc_ref[0]
    v_scr[0:DIL_SPAN, :] = vp_ref[0]
    v_scr[DIL_SPAN:2 * DIL_SPAN, :] = vc_ref[0]

    for p, (_, dil) in enumerate(DILATED_PATTERNS):
        span = BAND * dil

        def unit(u, carry, p=p, dil=dil, span=span):
            s = u // dil
            r = u - s * dil
            base = s * span + r
            first = jnp.logical_and(n == 0, s == 0).astype(jnp.int32)
            q = q_ref[0, pl.ds(base, BAND, stride=dil), :].astype(BF16)
            kk = k_scr[pl.ds(DIL_SPAN + base - span, 2 * BAND, stride=dil), :].astype(BF16)
            vv = v_scr[pl.ds(DIL_SPAN + base - span, 2 * BAND, stride=dil), :].astype(BF16)
            lg = _nt_dot(q, kk) + bias_ref[p, first, 0]
            m = jnp.max(lg, axis=-1, keepdims=True)
            e = jnp.exp2(lg - m).astype(BF16)
            acc = _dot(e, jnp.concatenate([vv, jnp.ones((2 * BAND, LANES), BF16)], axis=1))
            ssum = acc[:, LANES:]
            out_scr[p, pl.ds(base, BAND, stride=dil), :] = acc[:, :LANES] / ssum
            lse_scr[p, pl.ds(base, BAND, stride=dil), :] = m + jnp.log2(ssum)
            return carry

        lax.fori_loop(0, DIL_SPAN // BAND, unit, 0, unroll=4)

    chunk = 256

    def combine(c, carry):
        rows = pl.ds(pl.multiple_of(c * chunk, chunk), chunk)
        l0, l1, l2 = lse_scr[0, rows, :], lse_scr[1, rows, :], lse_scr[2, rows, :]
        big = jnp.maximum(jnp.maximum(l0, l1), l2)
        e0, e1, e2 = jnp.exp2(l0 - big), jnp.exp2(l1 - big), jnp.exp2(l2 - big)
        num = e0 * out_scr[0, rows, :] + e1 * out_scr[1, rows, :] + e2 * out_scr[2, rows, :]
        o_ref[0, rows, :] = num / (e0 + e1 + e2)
        return carry

    lax.fori_loop(0, DIL_SPAN // chunk, combine, 0)


def _dilated(p1, bias, batch, seq):
    blk = (1, DIL_SPAN, LANES)
    prev = lambda nn: jnp.maximum(nn - 1, 0)
    return pl.pallas_call(
        _dilated_kernel,
        grid=(batch, A_HEADS, seq // DIL_SPAN),
        in_specs=[pl.BlockSpec(blk, lambda b, h, nn: (b, nn, P1_QA + h)),
                  pl.BlockSpec(blk, lambda b, h, nn: (b, nn, P1_KA + h)),
                  pl.BlockSpec(blk, lambda b, h, nn: (b, prev(nn), P1_KA + h)),
                  pl.BlockSpec(blk, lambda b, h, nn: (b, nn, P1_VA + h)),
                  pl.BlockSpec(blk, lambda b, h, nn: (b, prev(nn), P1_VA + h)),
                  pl.BlockSpec((3, 2, 1, BAND, 2 * BAND), lambda b, h, nn: (0, 0, h, 0, 0))],
        out_specs=pl.BlockSpec(blk, lambda b, h, nn: (b, nn, h)),
        out_shape=jax.ShapeDtypeStruct((batch, seq, A_WIDTH), F32),
        scratch_shapes=[pltpu.VMEM((2 * DIL_SPAN, LANES), F32),
                        pltpu.VMEM((2 * DIL_SPAN, LANES), F32),
                        pltpu.VMEM((3, DIL_SPAN, LANES), F32),
                        pltpu.VMEM((3, DIL_SPAN, LANES), F32)],
        compiler_params=_params(("arbitrary", "arbitrary", "arbitrary")),
        name="dilated_attn",
    )(p1, p1, p1, p1, p1, bias)


def _compress_kernel(t_ref, pos_ref, w1_ref, w2_ref, gain_ref, flag_ref, o_ref, lo_scr, hi_scr):
    n_blk = o_ref.shape[3]
    lo_scr[...] = jnp.zeros_like(lo_scr)
    hi_scr[...] = jnp.zeros_like(hi_scr)
    for i in range(CMP_STRIDE):
        rows = t_ref[0, pl.ds(i, n_blk, stride=CMP_STRIDE), :]
        a = (rows + pos_ref[i:i + 1, :]).astype(BF16)
        lo_scr[...] += _dot(a, w1_ref[0, i * HEAD_DIM:(i + 1) * HEAD_DIM, :])
        b = (rows + pos_ref[CMP_STRIDE + i:CMP_STRIDE + i + 1, :]).astype(BF16)
        hi_scr[...] += _dot(b, w1_ref[0, (CMP_STRIDE + i) * HEAD_DIM:(CMP_STRIDE + i + 1) * HEAD_DIM, :])
    hid = lo_scr[...] + pltpu.roll(hi_scr[...], n_blk - 1, 0)
    o = _dot(jax.nn.gelu(hid).astype(BF16), w2_ref[0])
    ms = jnp.mean(o * o, axis=-1, keepdims=True)
    inv = jnp.where(flag_ref[0] > 0.0, lax.rsqrt(ms + EPS), 1.0)
    o_ref[0, 0, 0] = o * inv * gain_ref[0]


def _compress(p1, pos, w1, w2, gain, flag, batch, seq):
    n_blk = seq // CMP_STRIDE
    return pl.pallas_call(
        _compress_kernel,
        grid=(batch, 2, NSA_KV_GROUPS),
        in_specs=[pl.BlockSpec((1, seq, LANES), lambda b, kv, g: (b, 0, P1_KC + 2 * kv + g)),
                  pl.BlockSpec((CMP_LEN, HEAD_DIM), lambda b, kv, g: (0, 0)),
                  pl.BlockSpec((1, CMP_LEN * HEAD_DIM, CMP_HIDDEN), lambda b, kv, g: (kv, 0, 0)),
                  pl.BlockSpec((1, CMP_HIDDEN, HEAD_DIM), lambda b, kv, g: (kv, 0, 0)),
                  pl.BlockSpec((1, 1, HEAD_DIM), lambda b, kv, g: (kv, 0, 0)),
                  pl.BlockSpec((1, 1, HEAD_DIM), lambda b, kv, g: (kv, 0, 0))],
        out_specs=pl.BlockSpec((1, 1, 1, n_blk, HEAD_DIM), lambda b, kv, g: (b, kv, g, 0, 0)),
        out_shape=jax.ShapeDtypeStruct((batch, 2, NSA_KV_GROUPS, n_blk, HEAD_DIM), F32),
        scratch_shapes=[pltpu.VMEM((n_blk, CMP_HIDDEN), F32), pltpu.VMEM((n_blk, CMP_HIDDEN), F32)],
        compiler_params=_params(("arbitrary", "arbitrary", "arbitrary")),
        name="nsa_compress",
    )(p1, pos, w1, w2, gain, flag)


def _nsa_kernel(q_ref, ks_ref, vs_ref, kw_ref, vw_ref, kc_ref, vc_ref, gl_ref, gb_ref,
                bias_ref, ovl_ref, et_ref, o_ref, m_scr, acc_scr, s_scr, p_scr, a_scr):
    qi = pl.program_id(2)
    tq, rep = NSA_TQ, NSA_REP
    n_cmp = kc_ref.shape[3]

    q2 = q_ref[0]
    q_heads = [q2[:, r * HEAD_DIM:(r + 1) * HEAD_DIM] for r in range(rep)]
    t_col = qi * tq + lax.broadcasted_iota(jnp.int32, (tq, 1), 0)

    cmp_end = lax.broadcasted_iota(jnp.int32, (1, n_cmp), 1) * CMP_STRIDE + (CMP_LEN - 1)
    cmask = cmp_end <= t_col
    kcb = kc_ref[0, 0, 0].astype(BF16)
    vcb = vc_ref[0, 0, 0].astype(BF16)
    o_cmp = []
    pc_sum = jnp.zeros((tq, n_cmp), F32)
    for r in range(rep):
        lc = jnp.where(cmask, _nt_dot(q_heads[r], kcb), NEG_INF)
        mc = jnp.max(lc, axis=-1, keepdims=True)
        pc = jnp.where(cmask, jnp.exp2(lc - mc), 0.0)
        pc = pc / jnp.maximum(jnp.sum(pc, axis=-1, keepdims=True), 1e-30)
        o_cmp.append(_dot(pc.astype(BF16), vcb))
        pc_sum = pc_sum + pc

    hi = pc_sum.astype(BF16)
    rem = pc_sum - hi.astype(F32)
    mid = rem.astype(BF16)
    low = (rem - mid.astype(F32)).astype(BF16)
    ovl_t = ovl_ref[...]
    imp_t = _nt_dot(ovl_t, hi) + _nt_dot(ovl_t, mid) + _nt_dot(ovl_t, low)

    n_wt = WIN_LEN // NSA_TK + 1
    w_keys = n_wt * NSA_TK
    j0 = jnp.maximum(qi - (n_wt - 1), 0)
    w_rows = pl.ds(pl.multiple_of(j0 * NSA_TK, NSA_TK), w_keys)
    kw_tile = kw_ref[0, w_rows, :]
    vw_aug = jnp.concatenate([vw_ref[0, w_rows, :], jnp.ones((w_keys, HEAD_DIM), BF16)], axis=1)
    dist = t_col - (j0 * NSA_TK + lax.broadcasted_iota(jnp.int32, (1, w_keys), 1))
    wmask = jnp.where(dist >= 0, dist, WIN_LEN) < WIN_LEN
    w_tiles = [jnp.clip(qi - j0 - i, 0, BIAS_FAR) for i in range(n_wt)]
    o_win = []
    for r in range(rep):
        s = _nt_dot(q_heads[r], kw_tile) + jnp.concatenate([bias_ref[t, r] for t in w_tiles], axis=-1)
        s = jnp.where(wmask, s, NEG_INF)
        p = jnp.exp2(s - jnp.max(s, axis=-1, keepdims=True)).astype(BF16)
        acc = _dot(p, vw_aug)
        o_win.append(acc[:, :HEAD_DIM] / acc[:, HEAD_DIM:])

    blk = lax.broadcasted_iota(jnp.int32, (LANES, 1), 0)
    blk_f = blk.astype(F32)
    t_row = qi * tq + lax.broadcasted_iota(jnp.int32, (1, tq), 1)
    cur = t_row >> 6
    valid = blk * SLC_LEN <= t_row
    forced = (blk == 0) | (blk == cur) | (blk == cur - 1)
    score = jnp.where(valid, jnp.where(forced, FORCE_SCORE, imp_t), -1.0)
    sel_t = jnp.zeros((LANES, tq), F32)
    for _ in range(SLC_TOPK):
        best = jnp.max(score, axis=0, keepdims=True)
        idx = jnp.min(jnp.where(score == best, blk_f, float(LANES)), axis=0, keepdims=True)
        hit = blk_f == idx
        sel_t = jnp.where(hit, 1.0, sel_t)
        score = jnp.where(hit, -3e38, score)
    unsel = (sel_t.T - 1.0).astype(BF16)
    lhs = [jnp.concatenate([q_heads[r], unsel], axis=1) for r in range(rep)]

    ones_v = jnp.ones((SEL_TK, HEAD_DIM), BF16)

    sub = SEL_TK // NSA_TK
    n_tiles = qi // sub + 1

    def key_rows(jj):
        return pl.ds(pl.multiple_of(jj * SEL_TK, SEL_TK), SEL_TK)

    def logits_stage(jj):
        rows = key_rows(jj)
        rhs = jnp.concatenate([ks_ref[0, rows, :], et_ref[rows, :]], axis=1)
        for r in range(rep):
            s_scr[r] = _nt_dot(lhs[r], rhs)

    def softmax_stage(jj):
        u = qi - sub * jj
        tiles = [jnp.where(u < i, BIAS_FUTURE, jnp.minimum(u - i, BIAS_FAR)) for i in range(sub)]
        for r in range(rep):
            s = s_scr[r] + jnp.concatenate([bias_ref[t, r] for t in tiles], axis=-1)
            m_old = m_scr[r]
            m_new = jnp.maximum(m_old, jnp.max(s, axis=-1, keepdims=True))
            a_scr[r] = jnp.exp2(m_old - m_new)
            p_scr[r] = jnp.exp2(s - pltpu.repeat(m_new, SEL_TK // LANES, 1)).astype(BF16)
            m_scr[r] = m_new

    def pv_stage(jj):
        v_aug = jnp.concatenate([vs_ref[0, key_rows(jj), :], ones_v], axis=1)
        for r in range(rep):
            acc_scr[r] = pltpu.repeat(a_scr[r], 2, 1) * acc_scr[r] + _dot(p_scr[r], v_aug)

    m_scr[...] = jnp.full(m_scr.shape, NEG_INF, F32)
    acc_scr[...] = jnp.zeros_like(acc_scr)
    p_scr[...] = jnp.zeros_like(p_scr)
    a_scr[...] = jnp.ones_like(a_scr)
    logits_stage(0)

    def sel_step(it, carry):
        pv_stage(jnp.maximum(it - 1, 0))
        softmax_stage(it)
        logits_stage(jnp.minimum(it + 1, n_tiles - 1))
        return carry

    lax.fori_loop(0, n_tiles, sel_step, 0)
    pv_stage(n_tiles - 1)

    gates = jax.nn.sigmoid(gl_ref[0] + gb_ref[0])
    o_slc = []
    for r in range(rep):
        acc = acc_scr[r]
        o_slc.append(acc[:, :HEAD_DIM] / acc[:, HEAD_DIM:])
    for r in range(rep):
        c = r * N_BRANCH
        out = (gates[:, c:c + 1] * o_cmp[r] + gates[:, c + 1:c + 2] * o_slc[r]
               + gates[:, c + 2:c + 3] * o_win[r])
        o_ref[0, :, r * HEAD_DIM:(r + 1) * HEAD_DIM] = out


def _nsa(p1, p2, kcv, gate_bias, bias, overlap_t, membership, batch, seq):
    tq = NSA_TQ
    n_cmp = seq // CMP_STRIDE
    kv_blk = (1, seq, LANES)
    grp_w = NSA_REP * HEAD_DIM
    return pl.pallas_call(
        _nsa_kernel,
        grid=(batch, NSA_KV_GROUPS, seq // tq),
        in_specs=[pl.BlockSpec((1, tq, grp_w), lambda b, g, i: (b, i, g)),
                  pl.BlockSpec(kv_blk, lambda b, g, i: (b, 0, P2_KS + g)),
                  pl.BlockSpec(kv_blk, lambda b, g, i: (b, 0, P2_VS + g)),
                  pl.BlockSpec(kv_blk, lambda b, g, i: (b, 0, P2_KW + g)),
                  pl.BlockSpec(kv_blk, lambda b, g, i: (b, 0, P2_VW + g)),
                  pl.BlockSpec((1, 1, 1, n_cmp, HEAD_DIM), lambda b, g, i: (b, 0, g, 0, 0)),
                  pl.BlockSpec((1, 1, 1, n_cmp, HEAD_DIM), lambda b, g, i: (b, 1, g, 0, 0)),
                  pl.BlockSpec((1, tq, LANES), lambda b, g, i: (b, i, P1_GL + g)),
                  pl.BlockSpec((1, 1, LANES), lambda b, g, i: (g, 0, 0)),
                  pl.BlockSpec((N_BIAS_TILES, NSA_REP, NSA_TQ, NSA_TK), lambda b, g, i: (0, g, 0, 0)),
                  pl.BlockSpec((LANES, n_cmp), lambda b, g, i: (0, 0)),
                  pl.BlockSpec((seq, LANES), lambda b, g, i: (0, 0))],
        out_specs=pl.BlockSpec((1, tq, grp_w), lambda b, g, i: (b, i, g)),
        out_shape=jax.ShapeDtypeStruct((batch, seq, NSA_WIDTH), F32),
        scratch_shapes=[pltpu.VMEM((NSA_REP, tq, LANES), F32),
                        pltpu.VMEM((NSA_REP, tq, 2 * HEAD_DIM), F32),
                        pltpu.VMEM((NSA_REP, tq, SEL_TK), F32),
                        pltpu.VMEM((NSA_REP, tq, SEL_TK), BF16),
                        pltpu.VMEM((NSA_REP, tq, LANES), F32)],
        compiler_params=_params(("arbitrary", "arbitrary", "arbitrary")),
        name="nsa_attn",
    )(p2, p2, p2, p2, p2, kcv, kcv, p1, gate_bias, bias, overlap_t, membership)


def _out_kernel(x_ref, oa_ref, on_ref, ga_ref, gn_ref, w_ref, o_ref):
    def norm(t, g):
        ms = jnp.mean(t * t, axis=-1, keepdims=True)
        return (t * lax.rsqrt(ms + EPS) * g).astype(BF16)

    ya = norm(oa_ref[...], ga_ref[...])
    yn = norm(on_ref[...], gn_ref[...])
    y = _dot(ya, w_ref[0:A_WIDTH, :]) + _dot(yn, w_ref[A_WIDTH:A_WIDTH + NSA_WIDTH, :])
    o_ref[...] = x_ref[...] + y


def _out_proj(x2, oa, on, gain, w, *, tm):
    n_tok, d = x2.shape
    return pl.pallas_call(
        _out_kernel,
        grid=(n_tok // tm,),
        in_specs=[pl.BlockSpec((tm, d), lambda i: (i, 0)),
                  pl.BlockSpec((tm, A_WIDTH), lambda i: (i, 0)),
                  pl.BlockSpec((tm, NSA_WIDTH), lambda i: (i, 0)),
                  pl.BlockSpec((1, A_WIDTH), lambda i: (0, 0)),
                  pl.BlockSpec((1, NSA_WIDTH), lambda i: (0, 0)),
                  pl.BlockSpec((d, d), lambda i: (0, 0))],
        out_specs=pl.BlockSpec((tm, d), lambda i: (i, 0)),
        out_shape=jax.ShapeDtypeStruct((n_tok, d), F32),
        compiler_params=_params(("arbitrary",)),
        name="out_proj",
    )(x2, oa, on, gain[:A_WIDTH].reshape(1, A_WIDTH), gain[A_WIDTH:].reshape(1, NSA_WIDTH), w)


def _pack_w_in(w_in, gate_bias, q_norm_a, k_norm_a, q_norm_nsa, k_norm_nsa):
    scale = HEAD_DIM ** -0.5
    sizes = (A_WIDTH,) * 3 + (NSA_WIDTH,) + (KV_WIDTH,) * 6 + (N_BRANCH * NSA_HEADS,)
    offs = np.concatenate([[0], np.cumsum(sizes)])
    qa, ka, va, qn, kc, vc, k_s, v_s, k_w, v_w, gl = [w_in[:, offs[i]:offs[i + 1]] for i in range(11)]
    per_grp = NSA_REP * N_BRANCH
    pad = jnp.zeros((w_in.shape[0], LANES - per_grp), w_in.dtype)
    gl_cols = [jnp.concatenate([gl[:, g * per_grp:(g + 1) * per_grp], pad], axis=1) for g in range(NSA_KV_GROUPS)]
    w1 = jnp.concatenate([qa, ka, va, kc, vc] + gl_cols, axis=1).astype(BF16)
    w2 = jnp.concatenate([qn, k_s, v_s, k_w, v_w], axis=1).astype(BF16)

    ones = jnp.ones((HEAD_DIM,), F32)
    tile = lambda v, n: jnp.tile(v, n)
    g1 = jnp.concatenate([tile(q_norm_a * (scale * LOG2E), A_HEADS), tile(k_norm_a, A_HEADS),
                          tile(ones, A_HEADS + 6)])
    f1 = jnp.concatenate([jnp.ones((2 * A_WIDTH,), F32), jnp.zeros((A_WIDTH + 6 * HEAD_DIM,), F32)])
    g2 = jnp.concatenate([tile(q_norm_nsa * (scale * LOG2E), NSA_HEADS), tile(k_norm_nsa, 2), tile(ones, 2),
                          tile(k_norm_nsa, 2), tile(ones, 2)])
    f2 = jnp.concatenate([jnp.ones((NSA_WIDTH + KV_WIDTH,), F32), jnp.zeros((KV_WIDTH,), F32),
                          jnp.ones((KV_WIDTH,), F32), jnp.zeros((KV_WIDTH,), F32)])
    gb = jnp.stack([jnp.concatenate([gate_bias[g * per_grp:(g + 1) * per_grp],
                                     jnp.zeros((LANES - per_grp,), F32)]) for g in range(NSA_KV_GROUPS)])
    return w1, g1, f1, w2, g2, f2, gb.reshape(NSA_KV_GROUPS, 1, LANES)


def _forward(x, ffn1_norm, ffn1_w1, ffn1_w3, ffn1_w2, mix_norm, w_in, gate_bias, q_norm_a, k_norm_a,
             q_norm_nsa, k_norm_nsa, cmp_pos, cmp_k_w1, cmp_k_w2, cmp_v_w1, cmp_v_w2, out_norm, w_out,
             ffn2_norm, ffn2_w1, ffn2_w3, ffn2_w2, rel_bias, *, tm_ffn, tf_ffn, tm_proj, tm_out):
    batch, seq, d = x.shape
    depth = ffn1_w1.shape[0]
    n_tok = batch * seq
    assert seq % DIL_SPAN == 0 and seq // SLC_LEN <= LANES

    bias_a = _expand_bias(rel_bias[:, :A_HEADS], jnp.asarray(_dilated_buckets()), scale=LOG2E)
    bias_a = bias_a.reshape(3, 2, A_HEADS, BAND, 2 * BAND)
    bias_n = _expand_bias(rel_bias[:, A_HEADS:], jnp.asarray(_nsa_buckets()),
                          scale=LOG2E, shift_far=True)
    overlap_t = jnp.asarray(_overlap_t_np(seq // CMP_STRIDE)).astype(BF16)
    membership = jnp.asarray(_block_membership_np(seq)).astype(BF16)

    x2 = x.reshape(n_tok, d)
    for l in range(depth):
        x2 = _ffn(x2, ffn1_norm[l], ffn1_w1[l].astype(BF16), ffn1_w3[l].astype(BF16),
                  ffn1_w2[l].astype(BF16), tm=tm_ffn, tf=tf_ffn)

        w1, g1, f1, w2, g2, f2, gb = _pack_w_in(w_in[l], gate_bias[l], q_norm_a[l], k_norm_a[l],
                                                q_norm_nsa[l], k_norm_nsa[l])
        p1 = _proj(x2, mix_norm[l], w1, g1, f1, F32, tm=tm_proj, tn=640).reshape(batch, seq, P1_HEADS * LANES)
        p2 = _proj(x2, mix_norm[l], w2, g2, f2, BF16, tm=tm_proj, tn=512).reshape(batch, seq, P2_HEADS * LANES)

        o_a = _dilated(p1, bias_a, batch, seq)

        cmp_w1 = jnp.stack([cmp_k_w1[l], cmp_v_w1[l]]).astype(BF16)
        cmp_w2 = jnp.stack([cmp_k_w2[l], cmp_v_w2[l]]).astype(BF16)
        cmp_gain = jnp.stack([k_norm_nsa[l], jnp.ones((HEAD_DIM,), F32)]).reshape(2, 1, HEAD_DIM)
        cmp_flag = jnp.stack([jnp.ones((HEAD_DIM,), F32), jnp.zeros((HEAD_DIM,), F32)]).reshape(2, 1, HEAD_DIM)
        kcv = _compress(p1, cmp_pos[l], cmp_w1, cmp_w2, cmp_gain, cmp_flag, batch, seq)

        o_n = _nsa(p1, p2, kcv, gb, bias_n, overlap_t, membership, batch, seq)

        x2 = _out_proj(x2, o_a.reshape(n_tok, A_WIDTH), o_n.reshape(n_tok, NSA_WIDTH), out_norm[l],
                       w_out[l].astype(BF16), tm=tm_out)

        x2 = _ffn(x2, ffn2_norm[l], ffn2_w1[l].astype(BF16), ffn2_w3[l].astype(BF16),
                  ffn2_w2[l].astype(BF16), tm=tm_ffn, tf=tf_ffn)
    return x2.reshape(batch, seq, d)


def kernel(x, ffn1_norm, ffn1_w1, ffn1_w3, ffn1_w2, mix_norm, w_in, gate_bias, q_norm_a, k_norm_a, q_norm_nsa, k_norm_nsa, cmp_pos, cmp_k_w1, cmp_k_w2, cmp_v_w1, cmp_v_w2, out_norm, w_out, ffn2_norm, ffn2_w1, ffn2_w3, ffn2_w2, rel_bias):
    return _forward(x, ffn1_norm, ffn1_w1, ffn1_w3, ffn1_w2, mix_norm, w_in, gate_bias, q_norm_a, k_norm_a,
                    q_norm_nsa, k_norm_nsa, cmp_pos, cmp_k_w1, cmp_k_w2, cmp_v_w1, cmp_v_w2, out_norm, w_out,
                    ffn2_norm, ffn2_w1, ffn2_w3, ffn2_w2, rel_bias,
                    tm_ffn=512, tf_ffn=512, tm_proj=512, tm_out=256)
```

```python
import functools
import math

import numpy as np
import jax
import jax.numpy as jnp
from jax import lax
from jax.experimental import pallas as pl
from jax.experimental.pallas import tpu as pltpu

D_MODEL = 2048
HEAD_DIM = 128
N_HEADS = 16
A_HEADS = 8
NSA_HEADS = 8
NSA_KV_GROUPS = 2
NSA_REP = 4
A_WIDTH = A_HEADS * HEAD_DIM
NSA_WIDTH = NSA_HEADS * HEAD_DIM
KV_WIDTH = NSA_KV_GROUPS * HEAD_DIM
N_BRANCH = 3
D_FF = 5632
DILATED_PATTERNS = ((128, 1), (512, 4), (2048, 16))
BAND = 128
CMP_STRIDE = 16
CMP_LEN = 32
CMP_HIDDEN = 512
SLC_LEN = 64
SLC_TOPK = 16
WIN_LEN = 512
N_BUCKETS = 32
MAX_DISTANCE = 2048
FORCE_SCORE = 1e6
NEG_INF = -1e30
EPS = 1e-6
LOG2E = 1.4426950408889634

LANES = 128
DIL_SPAN = 2048
NSA_TQ = 128
NSA_TK = 128
SEL_TK = 256
BIAS_FAR = 13
BIAS_FUTURE = 14
N_BIAS_TILES = 15
MASK_BIG = 1e30
VMEM_LIMIT = 56 * 1024 * 1024

F32 = jnp.float32
BF16 = jnp.bfloat16

P1_QA, P1_KA, P1_VA, P1_KC, P1_VC, P1_GL = 0, 8, 16, 24, 26, 28
P1_USED = 30
P1_HEADS = 32
P2_QN, P2_KS, P2_VS, P2_KW, P2_VW = 0, 8, 10, 12, 14
P2_HEADS = 16


def _params(sem, flags=None):
    return pltpu.CompilerParams(dimension_semantics=sem, vmem_limit_bytes=VMEM_LIMIT, flags=flags)


def _nt_dot(a, b):
    return lax.dot_general(a, b, (((1,), (1,)), ((), ())), preferred_element_type=F32)


def _dot(a, b):
    return jnp.dot(a, b, preferred_element_type=F32)


def _rel_bucket_np(dist):
    n = np.maximum(dist, 0)
    max_exact = N_BUCKETS // 2
    nf = np.maximum(n, 1).astype(np.float32)
    log_b = max_exact + (np.log(nf / np.float32(max_exact)) / np.float32(math.log(MAX_DISTANCE / max_exact))
                         * np.float32(N_BUCKETS - max_exact)).astype(np.int32)
    return np.where(n < max_exact, n, np.minimum(log_b, N_BUCKETS - 1)).astype(np.int32)


def _dilated_buckets():
    i = np.arange(BAND)[:, None]
    j = np.arange(2 * BAND)[None, :] - BAND
    diff = i - j
    out = []
    for window, dilation in DILATED_PATTERNS:
        mask = (diff >= 0) & (diff <= window // dilation)
        bkt = _rel_bucket_np(diff * dilation)
        out.append(np.where(mask, bkt, -1))
        out.append(np.where(mask & (j >= 0), bkt, -1))
    return np.stack(out).astype(np.int32)


def _nsa_buckets():
    i = np.arange(NSA_TQ)[:, None]
    j = np.arange(NSA_TK)[None, :]
    tiles = []
    for m in range(BIAS_FAR + 1):
        dist = NSA_TK * m + i - j
        tiles.append(np.where(dist >= 0, _rel_bucket_np(dist), -1))
    assert (tiles[BIAS_FAR] == N_BUCKETS - 1).all()
    tiles.append(np.full((NSA_TQ, NSA_TK), -1))
    assert len(tiles) == N_BIAS_TILES and BIAS_FUTURE == N_BIAS_TILES - 1
    return np.stack(tiles).astype(np.int32)


def _overlap_t_np(n_cmp_pad):
    c_start = np.arange(n_cmp_pad)[None, :] * CMP_STRIDE
    s_start = np.arange(LANES)[:, None] * SLC_LEN
    return ((c_start < s_start + SLC_LEN) & (c_start + CMP_LEN > s_start)).astype(np.float32)


def _block_membership_np(seq):
    k_blk = np.arange(seq)[:, None] // SLC_LEN
    return np.where(k_blk == np.arange(LANES)[None, :], MASK_BIG, 0.0).astype(np.float32)


def _bias_kernel(tbl_ref, bkt_ref, o_ref, *, n_heads, scale, shift_far):
    bkt = bkt_ref[0]
    for h in range(n_heads):
        acc = jnp.full(bkt.shape, NEG_INF, F32)
        base = tbl_ref[N_BUCKETS - 1, h] if shift_far else 0.0
        for b in range(N_BUCKETS):
            acc = jnp.where(bkt == b, (tbl_ref[b, h] - base) * scale, acc)
        o_ref[0, h] = acc


def _expand_bias(table, buckets, scale=1.0, shift_far=False):
    n_t, rows, width = buckets.shape
    nh = table.shape[1]
    return pl.pallas_call(
        functools.partial(_bias_kernel, n_heads=nh, scale=scale, shift_far=shift_far),
        grid=(n_t,),
        in_specs=[pl.BlockSpec(memory_space=pltpu.SMEM),
                  pl.BlockSpec((1, rows, width), lambda t: (t, 0, 0))],
        out_specs=pl.BlockSpec((1, nh, rows, width), lambda t: (t, 0, 0, 0)),
        out_shape=jax.ShapeDtypeStruct((n_t, nh, rows, width), F32),
        compiler_params=_params(("arbitrary",)),
        name="bias_expand",
    )(table, buckets)


def _ffn_kernel(x_ref, g_ref, w1_ref, w3_ref, w2_ref, o_ref, n_scr, a_scr):
    j = pl.program_id(1)
    last = pl.num_programs(1) - 1

    def hidden():
        n = n_scr[...]
        h1 = _dot(n, w1_ref[...])
        h3 = _dot(n, w3_ref[...])
        a_scr[...] = (h1 * jax.nn.sigmoid(h1) * h3).astype(BF16)

    def project():
        o_ref[...] += _dot(a_scr[...], w2_ref[...])

    @pl.when(j == 0)
    def _():
        x = x_ref[...]
        ms = jnp.mean(x * x, axis=-1, keepdims=True)
        n_scr[...] = (x * lax.rsqrt(ms + EPS) * g_ref[...]).astype(BF16)
        o_ref[...] = jnp.zeros_like(o_ref)
        hidden()

    @pl.when(jnp.logical_and(j > 0, j < last))
    def _():
        project()
        hidden()

    @pl.when(j == last)
    def _():
        project()
        o_ref[...] = x_ref[...] + 0.5 * o_ref[...]


def _ffn(x2, gain, w1, w3, w2, *, tm, tf):
    n_tok, d = x2.shape
    n_f = w1.shape[1] // tf
    return pl.pallas_call(
        _ffn_kernel,
        grid=(n_tok // tm, n_f + 1),
        in_specs=[pl.BlockSpec((tm, d), lambda i, j: (i, 0)),
                  pl.BlockSpec((1, d), lambda i, j: (0, 0)),
                  pl.BlockSpec((d, tf), lambda i, j: (0, jnp.minimum(j, n_f - 1))),
                  pl.BlockSpec((d, tf), lambda i, j: (0, jnp.minimum(j, n_f - 1))),
                  pl.BlockSpec((tf, d), lambda i, j: (jnp.maximum(j - 1, 0), 0))],
        out_specs=pl.BlockSpec((tm, d), lambda i, j: (i, 0)),
        out_shape=jax.ShapeDtypeStruct((n_tok, d), F32),
        scratch_shapes=[pltpu.VMEM((tm, d), BF16), pltpu.VMEM((tm, tf), BF16)],
        compiler_params=_params(("arbitrary", "arbitrary")),
        name="ffn",
    )(x2, gain.reshape(1, d), w1, w3, w2)


def _proj_kernel(x_ref, g_ref, w_ref, cg_ref, cf_ref, o1_ref, o2_ref, n_scr, acc_scr, *, n1):
    j = pl.program_id(1)
    last = pl.num_programs(1) - 1

    def matmul():
        acc_scr[...] = _dot(n_scr[...], w_ref[...])

    def epilogue(o_ref):
        for h in range(acc_scr.shape[1] // LANES):
            cols = slice(h * LANES, (h + 1) * LANES)
            a = acc_scr[:, cols]
            ms = jnp.mean(a * a, axis=-1, keepdims=True)
            inv = jnp.where(cf_ref[:, cols] > 0.0, lax.rsqrt(ms + EPS), 1.0)
            o_ref[:, cols] = (a * inv * cg_ref[:, cols]).astype(o_ref.dtype)

    @pl.when(j == 0)
    def _():
        x = x_ref[...]
        ms = jnp.mean(x * x, axis=-1, keepdims=True)
        n_scr[...] = (x * lax.rsqrt(ms + EPS) * g_ref[...]).astype(BF16)
        matmul()

    @pl.when(jnp.logical_and(j > 0, j <= n1))
    def _():
        epilogue(o1_ref)
        matmul()

    @pl.when(jnp.logical_and(j > n1, j < last))
    def _():
        epilogue(o2_ref)
        matmul()

    @pl.when(j == last)
    def _():
        epilogue(o2_ref)


def _proj(x2, gain, w, col_gain, col_flag, cols1, *, tm, tn):
    n_tok, d = x2.shape
    n_cols = w.shape[1]
    n1 = cols1 // tn
    n2 = (n_cols - cols1) // tn
    assert n1 * tn == cols1 and (n1 + n2) * tn == n_cols
    n_j = n1 + n2
    prev = lambda j: jnp.maximum(j - 1, 0)
    return pl.pallas_call(
        functools.partial(_proj_kernel, n1=n1),
        grid=(n_tok // tm, n_j + 1),
        in_specs=[pl.BlockSpec((tm, d), lambda i, j: (i, 0)),
                  pl.BlockSpec((1, d), lambda i, j: (0, 0)),
                  pl.BlockSpec((d, tn), lambda i, j: (0, jnp.minimum(j, n_j - 1))),
                  pl.BlockSpec((1, tn), lambda i, j: (0, prev(j))),
                  pl.BlockSpec((1, tn), lambda i, j: (0, prev(j)))],
        out_specs=[pl.BlockSpec((tm, tn), lambda i, j: (i, jnp.minimum(prev(j), n1 - 1))),
                   pl.BlockSpec((tm, tn), lambda i, j: (i, jnp.maximum(prev(j) - n1, 0)))],
        out_shape=[jax.ShapeDtypeStruct((n_tok, cols1), F32),
                   jax.ShapeDtypeStruct((n_tok, n_cols - cols1), BF16)],
        scratch_shapes=[pltpu.VMEM((tm, d), BF16), pltpu.VMEM((tm, tn), F32)],
        compiler_params=_params(("arbitrary", "arbitrary")),
        name="in_proj",
    )(x2, gain.reshape(1, d), w, col_gain.reshape(1, n_cols), col_flag.reshape(1, n_cols))


def _dilated_kernel(q_ref, kc_ref, kp_ref, vc_ref, vp_ref, bias_ref, o_ref,
                    k_scr, v_scr, out_scr, lse_scr):
    n = pl.program_id(2)
    k_scr[0:DIL_SPAN, :] = kp_ref[0]
    k_scr[DIL_SPAN:2 * DIL_SPAN, :] = kc_ref[0]
    v_scr[0:DIL_SPAN, :] = vp_ref[0]
    v_scr[DIL_SPAN:2 * DIL_SPAN, :] = vc_ref[0]

    for p, (_, dil) in enumerate(DILATED_PATTERNS):
        span = BAND * dil

        def unit(u, carry, p=p, dil=dil, span=span):
            s = u // dil
            r = u - s * dil
            base = s * span + r
            first = jnp.logical_and(n == 0, s == 0).astype(jnp.int32)
            q = q_ref[0, pl.ds(base, BAND, stride=dil), :].astype(BF16)
            kk = k_scr[pl.ds(DIL_SPAN + base - span, 2 * BAND, stride=dil), :].astype(BF16)
            vv = v_scr[pl.ds(DIL_SPAN + base - span, 2 * BAND, stride=dil), :].astype(BF16)
            lg = _nt_dot(q, kk) + bias_ref[p, first, 0]
            m = jnp.max(lg, axis=-1, keepdims=True)
            e = jnp.exp2(lg - m).astype(BF16)
            acc = _dot(e, jnp.concatenate([vv, jnp.ones((2 * BAND, LANES), BF16)], axis=1))
            ssum = acc[:, LANES:]
            out_scr[p, pl.ds(base, BAND, stride=dil), :] = acc[:, :LANES] / ssum
            lse_scr[p, pl.ds(base, BAND, stride=dil), :] = m + jnp.log2(ssum)
            return carry

        lax.fori_loop(0, DIL_SPAN // BAND, unit, 0, unroll=8)

    chunk = 256

    def combine(c, carry):
        rows = pl.ds(pl.multiple_of(c * chunk, chunk), chunk)
        l0, l1, l2 = lse_scr[0, rows, :], lse_scr[1, rows, :], lse_scr[2, rows, :]
        big = jnp.maximum(jnp.maximum(l0, l1), l2)
        e0, e1, e2 = jnp.exp2(l0 - big), jnp.exp2(l1 - big), jnp.exp2(l2 - big)
        num = e0 * out_scr[0, rows, :] + e1 * out_scr[1, rows, :] + e2 * out_scr[2, rows, :]
        o_ref[0, rows, :] = num / (e0 + e1 + e2)
        return carry

    lax.fori_loop(0, DIL_SPAN // chunk, combine, 0)


def _dilated(p1, bias, batch, seq):
    blk = (1, DIL_SPAN, LANES)
    prev = lambda nn: jnp.maximum(nn - 1, 0)
    return pl.pallas_call(
        _dilated_kernel,
        grid=(batch, A_HEADS, seq // DIL_SPAN),
        in_specs=[pl.BlockSpec(blk, lambda b, h, nn: (b, nn, P1_QA + h)),
                  pl.BlockSpec(blk, lambda b, h, nn: (b, nn, P1_KA + h)),
                  pl.BlockSpec(blk, lambda b, h, nn: (b, prev(nn), P1_KA + h)),
                  pl.BlockSpec(blk, lambda b, h, nn: (b, nn, P1_VA + h)),
                  pl.BlockSpec(blk, lambda b, h, nn: (b, prev(nn), P1_VA + h)),
                  pl.BlockSpec((3, 2, 1, BAND, 2 * BAND), lambda b, h, nn: (0, 0, h, 0, 0))],
        out_specs=pl.BlockSpec(blk, lambda b, h, nn: (b, nn, h)),
        out_shape=jax.ShapeDtypeStruct((batch, seq, A_WIDTH), F32),
        scratch_shapes=[pltpu.VMEM((2 * DIL_SPAN, LANES), F32),
                        pltpu.VMEM((2 * DIL_SPAN, LANES), F32),
                        pltpu.VMEM((3, DIL_SPAN, LANES), F32),
                        pltpu.VMEM((3, DIL_SPAN, LANES), F32)],
        compiler_params=_params(("arbitrary", "arbitrary", "arbitrary")),
        name="dilated_attn",
    )(p1, p1, p1, p1, p1, bias)


def _compress_kernel(t_ref, pos_ref, w1_ref, w2_ref, gain_ref, flag_ref, o_ref, lo_scr, hi_scr):
    n_blk = o_ref.shape[3]
    lo_scr[...] = jnp.zeros_like(lo_scr)
    hi_scr[...] = jnp.zeros_like(hi_scr)
    for i in range(CMP_STRIDE):
        rows = t_ref[0, pl.ds(i, n_blk, stride=CMP_STRIDE), :]
        a = (rows + pos_ref[i:i + 1, :]).astype(BF16)
        lo_scr[...] += _dot(a, w1_ref[0, i * HEAD_DIM:(i + 1) * HEAD_DIM, :])
        b = (rows + pos_ref[CMP_STRIDE + i:CMP_STRIDE + i + 1, :]).astype(BF16)
        hi_scr[...] += _dot(b, w1_ref[0, (CMP_STRIDE + i) * HEAD_DIM:(CMP_STRIDE + i + 1) * HEAD_DIM, :])
    hid = lo_scr[...] + pltpu.roll(hi_scr[...], n_blk - 1, 0)
    o = _dot(jax.nn.gelu(hid).astype(BF16), w2_ref[0])
    ms = jnp.mean(o * o, axis=-1, keepdims=True)
    inv = jnp.where(flag_ref[0] > 0.0, lax.rsqrt(ms + EPS), 1.0)
    o_ref[0, 0, 0] = o * inv * gain_ref[0]


def _compress(p1, pos, w1, w2, gain, flag, batch, seq):
    n_blk = seq // CMP_STRIDE
    return pl.pallas_call(
        _compress_kernel,
        grid=(batch, 2, NSA_KV_GROUPS),
        in_specs=[pl.BlockSpec((1, seq, LANES), lambda b, kv, g: (b, 0, P1_KC + 2 * kv + g)),
                  pl.BlockSpec((CMP_LEN, HEAD_DIM), lambda b, kv, g: (0, 0)),
                  pl.BlockSpec((1, CMP_LEN * HEAD_DIM, CMP_HIDDEN), lambda b, kv, g: (kv, 0, 0)),
                  pl.BlockSpec((1, CMP_HIDDEN, HEAD_DIM), lambda b, kv, g: (kv, 0, 0)),
                  pl.BlockSpec((1, 1, HEAD_DIM), lambda b, kv, g: (kv, 0, 0)),
                  pl.BlockSpec((1, 1, HEAD_DIM), lambda b, kv, g: (kv, 0, 0))],
        out_specs=pl.BlockSpec((1, 1, 1, n_blk, HEAD_DIM), lambda b, kv, g: (b, kv, g, 0, 0)),
        out_shape=jax.ShapeDtypeStruct((batch, 2, NSA_KV_GROUPS, n_blk, HEAD_DIM), F32),
        scratch_shapes=[pltpu.VMEM((n_blk, CMP_HIDDEN), F32), pltpu.VMEM((n_blk, CMP_HIDDEN), F32)],
        compiler_params=_params(("arbitrary", "arbitrary", "arbitrary")),
        name="nsa_compress",
    )(p1, pos, w1, w2, gain, flag)


def _nsa_kernel(q_ref, ks_ref, vs_ref, kw_ref, vw_ref, kc_ref, vc_ref, gl_ref, gb_ref,
                bias_ref, ovl_ref, et_ref, o_ref, m_scr, acc_scr, s_scr, p_scr, a_scr):
    qi = pl.program_id(2)
    tq, rep = NSA_TQ, NSA_REP
    n_cmp = kc_ref.shape[3]

    q2 = q_ref[0]
    q_heads = [q2[:, r * HEAD_DIM:(r + 1) * HEAD_DIM] for r in range(rep)]
    t_col = qi * tq + lax.broadcasted_iota(jnp.int32, (tq, 1), 0)

    cmp_end = lax.broadcasted_iota(jnp.int32, (1, n_cmp), 1) * CMP_STRIDE + (CMP_LEN - 1)
    cmask = cmp_end <= t_col
    kcb = kc_ref[0, 0, 0].astype(BF16)
    vcb = vc_ref[0, 0, 0].astype(BF16)
    o_cmp = []
    pc_sum = jnp.zeros((tq, n_cmp), F32)
    for r in range(rep):
        lc = jnp.where(cmask, _nt_dot(q_heads[r], kcb), NEG_INF)
        mc = jnp.max(lc, axis=-1, keepdims=True)
        pc = jnp.where(cmask, jnp.exp2(lc - mc), 0.0)
        pc = pc / jnp.maximum(jnp.sum(pc, axis=-1, keepdims=True), 1e-30)
        o_cmp.append(_dot(pc.astype(BF16), vcb))
        pc_sum = pc_sum + pc

    hi = pc_sum.astype(BF16)
    rem = pc_sum - hi.astype(F32)
    mid = rem.astype(BF16)
    low = (rem - mid.astype(F32)).astype(BF16)
    ovl_t = ovl_ref[...]
    imp_t = _nt_dot(ovl_t, hi) + _nt_dot(ovl_t, mid) + _nt_dot(ovl_t, low)

    n_wt = WIN_LEN // NSA_TK + 1
    w_keys = n_wt * NSA_TK
    j0 = jnp.maximum(qi - (n_wt - 1), 0)
    w_rows = pl.ds(pl.multiple_of(j0 * NSA_TK, NSA_TK), w_keys)
    kw_tile = kw_ref[0, w_rows, :]
    vw_aug = jnp.concatenate([vw_ref[0, w_rows, :], jnp.ones((w_keys, HEAD_DIM), BF16)], axis=1)
    dist = t_col - (j0 * NSA_TK + lax.broadcasted_iota(jnp.int32, (1, w_keys), 1))
    wmask = jnp.where(dist >= 0, dist, WIN_LEN) < WIN_LEN
    w_tiles = [jnp.clip(qi - j0 - i, 0, BIAS_FAR) for i in range(n_wt)]
    o_win = []
    for r in range(rep):
        s = _nt_dot(q_heads[r], kw_tile) + jnp.concatenate([bias_ref[t, r] for t in w_tiles], axis=-1)
        s = jnp.where(wmask, s, NEG_INF)
        p = jnp.exp2(s - jnp.max(s, axis=-1, keepdims=True)).astype(BF16)
        acc = _dot(p, vw_aug)
        o_win.append(acc[:, :HEAD_DIM] / acc[:, HEAD_DIM:])

    blk = lax.broadcasted_iota(jnp.int32, (LANES, 1), 0)
    blk_f = blk.astype(F32)
    t_row = qi * tq + lax.broadcasted_iota(jnp.int32, (1, tq), 1)
    cur = t_row >> 6
    valid = blk * SLC_LEN <= t_row
    forced = (blk == 0) | (blk == cur) | (blk == cur - 1)
    score = jnp.where(valid, jnp.where(forced, FORCE_SCORE, imp_t), -1.0)
    sel_t = jnp.zeros((LANES, tq), F32)
    for _ in range(SLC_TOPK):
        best = jnp.max(score, axis=0, keepdims=True)
        idx = jnp.min(jnp.where(score == best, blk_f, float(LANES)), axis=0, keepdims=True)
        hit = blk_f == idx
        sel_t = jnp.where(hit, 1.0, sel_t)
        score = jnp.where(hit, -3e38, score)
    unsel = (sel_t.T - 1.0).astype(BF16)
    lhs = [jnp.concatenate([q_heads[r], unsel], axis=1) for r in range(rep)]

    ones_v = jnp.ones((SEL_TK, HEAD_DIM), BF16)

    sub = SEL_TK // NSA_TK
    n_tiles = qi // sub + 1

    def key_rows(jj):
        return pl.ds(pl.multiple_of(jj * SEL_TK, SEL_TK), SEL_TK)

    def logits_stage(jj, heads):
        rows = key_rows(jj)
        rhs = jnp.concatenate([ks_ref[0, rows, :], et_ref[rows, :]], axis=1)
        for r in heads:
            s_scr[r] = _nt_dot(lhs[r], rhs)

    def softmax_stage(jj, heads):
        u = qi - sub * jj
        tiles = [jnp.where(u < i, BIAS_FUTURE, jnp.minimum(u - i, BIAS_FAR)) for i in range(sub)]
        for r in heads:
            s = s_scr[r] + jnp.concatenate([bias_ref[t, r] for t in tiles], axis=-1)
            m_old = m_scr[r]
            m_new = jnp.maximum(m_old, jnp.max(s, axis=-1, keepdims=True))
            a_scr[r] = jnp.exp2(m_old - m_new)
            p_scr[r] = jnp.exp2(s - jnp.concatenate([m_new] * (SEL_TK // LANES), axis=1)).astype(BF16)
            m_scr[r] = m_new

    def pv_stage(jj, heads):
        v_aug = jnp.concatenate([vs_ref[0, key_rows(jj), :], ones_v], axis=1)
        for r in heads:
            alpha = a_scr[r]
            acc_scr[r] = jnp.concatenate([alpha, alpha], axis=1) * acc_scr[r] + _dot(p_scr[r], v_aug)

    all_heads = range(rep)
    m_scr[...] = jnp.full(m_scr.shape, NEG_INF, F32)
    acc_scr[...] = jnp.zeros_like(acc_scr)
    p_scr[...] = jnp.zeros_like(p_scr)
    a_scr[...] = jnp.ones_like(a_scr)
    logits_stage(0, all_heads)

    def sel_step(it, carry):
        pv_stage(jnp.maximum(it - 1, 0), all_heads)
        softmax_stage(it, all_heads)
        logits_stage(jnp.minimum(it + 1, n_tiles - 1), all_heads)
        return carry

    lax.fori_loop(0, n_tiles, sel_step, 0)
    pv_stage(n_tiles - 1, all_heads)

    gates = jax.nn.sigmoid(gl_ref[0] + gb_ref[0])
    o_slc = []
    for r in range(rep):
        acc = acc_scr[r]
        o_slc.append(acc[:, :HEAD_DIM] / acc[:, HEAD_DIM:])
    for r in range(rep):
        c = r * N_BRANCH
        out = (gates[:, c:c + 1] * o_cmp[r] + gates[:, c + 1:c + 2] * o_slc[r]
               + gates[:, c + 2:c + 3] * o_win[r])
        o_ref[0, :, r * HEAD_DIM:(r + 1) * HEAD_DIM] = out


def _nsa(p1, p2, kcv, gate_bias, bias, overlap_t, membership, batch, seq):
    tq = NSA_TQ
    n_cmp = seq // CMP_STRIDE
    kv_blk = (1, seq, LANES)
    grp_w = NSA_REP * HEAD_DIM
    return pl.pallas_call(
        _nsa_kernel,
        grid=(batch, NSA_KV_GROUPS, seq // tq),
        in_specs=[pl.BlockSpec((1, tq, grp_w), lambda b, g, i: (b, i, g)),
                  pl.BlockSpec(kv_blk, lambda b, g, i: (b, 0, P2_KS + g)),
                  pl.BlockSpec(kv_blk, lambda b, g, i: (b, 0, P2_VS + g)),
                  pl.BlockSpec(kv_blk, lambda b, g, i: (b, 0, P2_KW + g)),
                  pl.BlockSpec(kv_blk, lambda b, g, i: (b, 0, P2_VW + g)),
                  pl.BlockSpec((1, 1, 1, n_cmp, HEAD_DIM), lambda b, g, i: (b, 0, g, 0, 0)),
                  pl.BlockSpec((1, 1, 1, n_cmp, HEAD_DIM), lambda b, g, i: (b, 1, g, 0, 0)),
                  pl.BlockSpec((1, tq, LANES), lambda b, g, i: (b, i, P1_GL + g)),
                  pl.BlockSpec((1, 1, LANES), lambda b, g, i: (g, 0, 0)),
                  pl.BlockSpec((N_BIAS_TILES, NSA_REP, NSA_TQ, NSA_TK), lambda b, g, i: (0, g, 0, 0)),
                  pl.BlockSpec((LANES, n_cmp), lambda b, g, i: (0, 0)),
                  pl.BlockSpec((seq, LANES), lambda b, g, i: (0, 0))],
        out_specs=pl.BlockSpec((1, tq, grp_w), lambda b, g, i: (b, i, g)),
        out_shape=jax.ShapeDtypeStruct((batch, seq, NSA_WIDTH), F32),
        scratch_shapes=[pltpu.VMEM((NSA_REP, tq, LANES), F32),
                        pltpu.VMEM((NSA_REP, tq, 2 * HEAD_DIM), F32),
                        pltpu.VMEM((NSA_REP, tq, SEL_TK), F32),
                        pltpu.VMEM((NSA_REP, tq, SEL_TK), BF16),
                        pltpu.VMEM((NSA_REP, tq, LANES), F32)],
        compiler_params=_params(("arbitrary", "arbitrary", "arbitrary")),
        name="nsa_attn",
    )(p2, p2, p2, p2, p2, kcv, kcv, p1, gate_bias, bias, overlap_t, membership)


def _out_kernel(x_ref, oa_ref, on_ref, ga_ref, gn_ref, w_ref, o_ref):
    def norm(t, g):
        ms = jnp.mean(t * t, axis=-1, keepdims=True)
        return (t * lax.rsqrt(ms + EPS) * g).astype(BF16)

    ya = norm(oa_ref[...], ga_ref[...])
    yn = norm(on_ref[...], gn_ref[...])
    y = _dot(ya, w_ref[0:A_WIDTH, :]) + _dot(yn, w_ref[A_WIDTH:A_WIDTH + NSA_WIDTH, :])
    o_ref[...] = x_ref[...] + y


def _out_proj(x2, oa, on, gain, w, *, tm):
    n_tok, d = x2.shape
    return pl.pallas_call(
        _out_kernel,
        grid=(n_tok // tm,),
        in_specs=[pl.BlockSpec((tm, d), lambda i: (i, 0)),
                  pl.BlockSpec((tm, A_WIDTH), lambda i: (i, 0)),
                  pl.BlockSpec((tm, NSA_WIDTH), lambda i: (i, 0)),
                  pl.BlockSpec((1, A_WIDTH), lambda i: (0, 0)),
                  pl.BlockSpec((1, NSA_WIDTH), lambda i: (0, 0)),
                  pl.BlockSpec((d, d), lambda i: (0, 0))],
        out_specs=pl.BlockSpec((tm, d), lambda i: (i, 0)),
        out_shape=jax.ShapeDtypeStruct((n_tok, d), F32),
        compiler_params=_params(("arbitrary",)),
        name="out_proj",
    )(x2, oa, on, gain[:A_WIDTH].reshape(1, A_WIDTH), gain[A_WIDTH:].reshape(1, NSA_WIDTH), w)


def _pack_w_in(w_in, gate_bias, q_norm_a, k_norm_a, q_norm_nsa, k_norm_nsa):
    scale = HEAD_DIM ** -0.5
    sizes = (A_WIDTH,) * 3 + (NSA_WIDTH,) + (KV_WIDTH,) * 6 + (N_BRANCH * NSA_HEADS,)
    offs = np.concatenate([[0], np.cumsum(sizes)])
    qa, ka, va, qn, kc, vc, k_s, v_s, k_w, v_w, gl = [w_in[:, offs[i]:offs[i + 1]] for i in range(11)]
    per_grp = NSA_REP * N_BRANCH
    pad = jnp.zeros((w_in.shape[0], LANES - per_grp), w_in.dtype)
    gl_cols = [jnp.concatenate([gl[:, g * per_grp:(g + 1) * per_grp], pad], axis=1) for g in range(NSA_KV_GROUPS)]
    n_pad = P1_HEADS - P1_USED
    w_pad = jnp.zeros((w_in.shape[0], n_pad * LANES), w_in.dtype)
    w = jnp.concatenate([qa, ka, va, kc, vc] + gl_cols + [w_pad, qn, k_s, v_s, k_w, v_w], axis=1).astype(BF16)

    ones = jnp.ones((HEAD_DIM,), F32)
    tile = lambda v, n: jnp.tile(v, n)
    gain = jnp.concatenate([tile(q_norm_a * (scale * LOG2E), A_HEADS), tile(k_norm_a, A_HEADS),
                            tile(ones, A_HEADS + 6 + n_pad),
                            tile(q_norm_nsa * (scale * LOG2E), NSA_HEADS), tile(k_norm_nsa, 2), tile(ones, 2),
                            tile(k_norm_nsa, 2), tile(ones, 2)])
    flag = jnp.concatenate([jnp.ones((2 * A_WIDTH,), F32), jnp.zeros((A_WIDTH + (6 + n_pad) * HEAD_DIM,), F32),
                            jnp.ones((NSA_WIDTH + KV_WIDTH,), F32), jnp.zeros((KV_WIDTH,), F32),
                            jnp.ones((KV_WIDTH,), F32), jnp.zeros((KV_WIDTH,), F32)])
    gb = jnp.stack([jnp.concatenate([gate_bias[g * per_grp:(g + 1) * per_grp],
                                     jnp.zeros((LANES - per_grp,), F32)]) for g in range(NSA_KV_GROUPS)])
    return w, gain, flag, gb.reshape(NSA_KV_GROUPS, 1, LANES)


def _forward(x, ffn1_norm, ffn1_w1, ffn1_w3, ffn1_w2, mix_norm, w_in, gate_bias, q_norm_a, k_norm_a,
             q_norm_nsa, k_norm_nsa, cmp_pos, cmp_k_w1, cmp_k_w2, cmp_v_w1, cmp_v_w2, out_norm, w_out,
             ffn2_norm, ffn2_w1, ffn2_w3, ffn2_w2, rel_bias, *, tm_ffn, tf_ffn, tm_proj, tm_out):
    batch, seq, d = x.shape
    depth = ffn1_w1.shape[0]
    n_tok = batch * seq
    assert seq % DIL_SPAN == 0 and seq // SLC_LEN <= LANES

    bias_a = _expand_bias(rel_bias[:, :A_HEADS], jnp.asarray(_dilated_buckets()), scale=LOG2E)
    bias_a = bias_a.reshape(3, 2, A_HEADS, BAND, 2 * BAND)
    bias_n = _expand_bias(rel_bias[:, A_HEADS:], jnp.asarray(_nsa_buckets()),
                          scale=LOG2E, shift_far=True)
    overlap_t = jnp.asarray(_overlap_t_np(seq // CMP_STRIDE)).astype(BF16)
    membership = jnp.asarray(_block_membership_np(seq)).astype(BF16)

    x2 = x.reshape(n_tok, d)
    for l in range(depth):
        x2 = _ffn(x2, ffn1_norm[l], ffn1_w1[l].astype(BF16), ffn1_w3[l].astype(BF16),
                  ffn1_w2[l].astype(BF16), tm=tm_ffn, tf=tf_ffn)

        w_p, g_p, f_p, gb = _pack_w_in(w_in[l], gate_bias[l], q_norm_a[l], k_norm_a[l],
                                       q_norm_nsa[l], k_norm_nsa[l])
        p1, p2 = _proj(x2, mix_norm[l], w_p, g_p, f_p, P1_HEADS * LANES, tm=tm_proj, tn=512)
        p1 = p1.reshape(batch, seq, P1_HEADS * LANES)
        p2 = p2.reshape(batch, seq, P2_HEADS * LANES)

        o_a = _dilated(p1, bias_a, batch, seq)

        cmp_w1 = jnp.stack([cmp_k_w1[l], cmp_v_w1[l]]).astype(BF16)
        cmp_w2 = jnp.stack([cmp_k_w2[l], cmp_v_w2[l]]).astype(BF16)
        cmp_gain = jnp.stack([k_norm_nsa[l], jnp.ones((HEAD_DIM,), F32)]).reshape(2, 1, HEAD_DIM)
        cmp_flag = jnp.stack([jnp.ones((HEAD_DIM,), F32), jnp.zeros((HEAD_DIM,), F32)]).reshape(2, 1, HEAD_DIM)
        kcv = _compress(p1, cmp_pos[l], cmp_w1, cmp_w2, cmp_gain, cmp_flag, batch, seq)

        o_n = _nsa(p1, p2, kcv, gb, bias_n, overlap_t, membership, batch, seq)

        x2 = _out_proj(x2, o_a.reshape(n_tok, A_WIDTH), o_n.reshape(n_tok, NSA_WIDTH), out_norm[l],
                       w_out[l].astype(BF16), tm=tm_out)

        x2 = _ffn(x2, ffn2_norm[l], ffn2_w1[l].astype(BF16), ffn2_w3[l].astype(BF16),
                  ffn2_w2[l].astype(BF16), tm=tm_ffn, tf=tf_ffn)
    return x2.reshape(batch, seq, d)


def kernel(x, ffn1_norm, ffn1_w1, ffn1_w3, ffn1_w2, mix_norm, w_in, gate_bias, q_norm_a, k_norm_a, q_norm_nsa, k_norm_nsa, cmp_pos, cmp_k_w1, cmp_k_w2, cmp_v_w1, cmp_v_w2, out_norm, w_out, ffn2_norm, ffn2_w1, ffn2_w3, ffn2_w2, rel_bias):
    return _forward(x, ffn1_norm, ffn1_w1, ffn1_w3, ffn1_w2, mix_norm, w_in, gate_bias, q_norm_a, k_norm_a,
                    q_norm_nsa, k_norm_nsa, cmp_pos, cmp_k_w1, cmp_k_w2, cmp_v_w1, cmp_v_w2, out_norm, w_out,
                    ffn2_norm, ffn2_w1, ffn2_w3, ffn2_w2, rel_bias,
                    tm_ffn=512, tf_ffn=512, tm_proj=1024, tm_out=256)
```

```python
import functools
import math

import numpy as np
import jax
import jax.numpy as jnp
from jax import lax
from jax.experimental import pallas as pl
from jax.experimental.pallas import tpu as pltpu

D_MODEL = 2048
HEAD_DIM = 128
N_HEADS = 16
A_HEADS = 8
NSA_HEADS = 8
NSA_KV_GROUPS = 2
NSA_REP = 4
A_WIDTH = A_HEADS * HEAD_DIM
NSA_WIDTH = NSA_HEADS * HEAD_DIM
KV_WIDTH = NSA_KV_GROUPS * HEAD_DIM
N_BRANCH = 3
D_FF = 5632
DILATED_PATTERNS = ((128, 1), (512, 4), (2048, 16))
BAND = 128
CMP_STRIDE = 16
CMP_LEN = 32
CMP_HIDDEN = 512
SLC_LEN = 64
SLC_TOPK = 16
WIN_LEN = 512
N_BUCKETS = 32
MAX_DISTANCE = 2048
FORCE_SCORE = 1e6
NEG_INF = -1e30
EPS = 1e-6
LOG2E = 1.4426950408889634

LANES = 128
DIL_SPAN = 2048
NSA_TQ = 128
NSA_TK = 128
SEL_TK = 512
CMP_ROWS = 32
BIAS_FAR = 13
BIAS_FUTURE = 14
N_BIAS_TILES = 15
MASK_BIG = 1e30
VMEM_LIMIT = 56 * 1024 * 1024

F32 = jnp.float32
BF16 = jnp.bfloat16

P1_QA, P1_KA, P1_VA, P1_KC, P1_VC, P1_GL = 0, 8, 16, 24, 26, 28
P1_USED = 30
P1_HEADS = 32
P2_QN, P2_KS, P2_VS, P2_KW, P2_VW = 0, 8, 10, 12, 14
P2_HEADS = 16


def _params(sem, flags=None):
    return pltpu.CompilerParams(dimension_semantics=sem, vmem_limit_bytes=VMEM_LIMIT, flags=flags)


def _nt_dot(a, b):
    return lax.dot_general(a, b, (((1,), (1,)), ((), ())), preferred_element_type=F32)


def _dot(a, b):
    return jnp.dot(a, b, preferred_element_type=F32)


def _rel_bucket_np(dist):
    n = np.maximum(dist, 0)
    max_exact = N_BUCKETS // 2
    nf = np.maximum(n, 1).astype(np.float32)
    log_b = max_exact + (np.log(nf / np.float32(max_exact)) / np.float32(math.log(MAX_DISTANCE / max_exact))
                         * np.float32(N_BUCKETS - max_exact)).astype(np.int32)
    return np.where(n < max_exact, n, np.minimum(log_b, N_BUCKETS - 1)).astype(np.int32)


def _dilated_buckets():
    i = np.arange(BAND)[:, None]
    j = np.arange(2 * BAND)[None, :] - BAND
    diff = i - j
    out = []
    for window, dilation in DILATED_PATTERNS:
        mask = (diff >= 0) & (diff <= window // dilation)
        bkt = _rel_bucket_np(diff * dilation)
        out.append(np.where(mask, bkt, -1))
        out.append(np.where(mask & (j >= 0), bkt, -1))
    return np.stack(out).astype(np.int32)


def _nsa_buckets():
    i = np.arange(NSA_TQ)[:, None]
    j = np.arange(NSA_TK)[None, :]
    tiles = []
    for m in range(BIAS_FAR + 1):
        dist = NSA_TK * m + i - j
        tiles.append(np.where(dist >= 0, _rel_bucket_np(dist), -1))
    assert (tiles[BIAS_FAR] == N_BUCKETS - 1).all()
    tiles.append(np.full((NSA_TQ, NSA_TK), -1))
    assert len(tiles) == N_BIAS_TILES and BIAS_FUTURE == N_BIAS_TILES - 1
    return np.stack(tiles).astype(np.int32)


def _overlap_t_np(n_cmp_pad):
    c_start = np.arange(n_cmp_pad)[None, :] * CMP_STRIDE
    s_start = np.arange(LANES)[:, None] * SLC_LEN
    return ((c_start < s_start + SLC_LEN) & (c_start + CMP_LEN > s_start)).astype(np.float32)


def _block_membership_np(seq):
    k_blk = np.arange(seq)[:, None] // SLC_LEN
    return np.where(k_blk == np.arange(LANES)[None, :], MASK_BIG, 0.0).astype(np.float32)


def _bias_kernel(tbl_ref, bkt_ref, o_ref, *, n_heads, scale, shift_far):
    bkt = bkt_ref[0]
    for h in range(n_heads):
        acc = jnp.full(bkt.shape, NEG_INF, F32)
        base = tbl_ref[N_BUCKETS - 1, h] if shift_far else 0.0
        for b in range(N_BUCKETS):
            acc = jnp.where(bkt == b, (tbl_ref[b, h] - base) * scale, acc)
        o_ref[0, h] = acc


def _expand_bias(table, buckets, scale=1.0, shift_far=False):
    n_t, rows, width = buckets.shape
    nh = table.shape[1]
    return pl.pallas_call(
        functools.partial(_bias_kernel, n_heads=nh, scale=scale, shift_far=shift_far),
        grid=(n_t,),
        in_specs=[pl.BlockSpec(memory_space=pltpu.SMEM),
                  pl.BlockSpec((1, rows, width), lambda t: (t, 0, 0))],
        out_specs=pl.BlockSpec((1, nh, rows, width), lambda t: (t, 0, 0, 0)),
        out_shape=jax.ShapeDtypeStruct((n_t, nh, rows, width), F32),
        compiler_params=_params(("arbitrary",)),
        name="bias_expand",
    )(table, buckets)


def _ffn_kernel(x_ref, g_ref, w1_ref, w3_ref, w2_ref, o_ref, n_scr):
    j = pl.program_id(1)

    @pl.when(j == 0)
    def _():
        x = x_ref[...]
        ms = jnp.mean(x * x, axis=-1, keepdims=True)
        n_scr[...] = (x * lax.rsqrt(ms + EPS) * g_ref[...]).astype(BF16)
        o_ref[...] = jnp.zeros_like(o_ref)

    n = n_scr[...]
    h1 = _dot(n, w1_ref[...])
    h3 = _dot(n, w3_ref[...])
    a = (h1 * jax.nn.sigmoid(h1) * h3).astype(BF16)
    o_ref[...] += _dot(a, w2_ref[...])

    @pl.when(j == pl.num_programs(1) - 1)
    def _():
        o_ref[...] = x_ref[...] + 0.5 * o_ref[...]


def _ffn(x2, gain, w1, w3, w2, *, tm, tf):
    n_tok, d = x2.shape
    f = w1.shape[1]
    return pl.pallas_call(
        _ffn_kernel,
        grid=(n_tok // tm, f // tf),
        in_specs=[pl.BlockSpec((tm, d), lambda i, j: (i, 0)),
                  pl.BlockSpec((1, d), lambda i, j: (0, 0)),
                  pl.BlockSpec((d, tf), lambda i, j: (0, j)),
                  pl.BlockSpec((d, tf), lambda i, j: (0, j)),
                  pl.BlockSpec((tf, d), lambda i, j: (j, 0))],
        out_specs=pl.BlockSpec((tm, d), lambda i, j: (i, 0)),
        out_shape=jax.ShapeDtypeStruct((n_tok, d), F32),
        scratch_shapes=[pltpu.VMEM((tm, d), BF16)],
        compiler_params=_params(("arbitrary", "arbitrary")),
        name="ffn",
    )(x2, gain.reshape(1, d), w1, w3, w2)


def _proj_kernel(x_ref, g_ref, w_ref, cg_ref, cf_ref, o1_ref, o2_ref, n_scr, acc_scr, *, n1):
    j = pl.program_id(1)
    last = pl.num_programs(1) - 1

    def matmul():
        acc_scr[...] = _dot(n_scr[...], w_ref[...])

    def epilogue(o_ref):
        for h in range(acc_scr.shape[1] // LANES):
            cols = slice(h * LANES, (h + 1) * LANES)
            a = acc_scr[:, cols]
            ms = jnp.mean(a * a, axis=-1, keepdims=True)
            inv = jnp.where(cf_ref[:, cols] > 0.0, lax.rsqrt(ms + EPS), 1.0)
            o_ref[:, cols] = (a * inv * cg_ref[:, cols]).astype(o_ref.dtype)

    @pl.when(j == 0)
    def _():
        x = x_ref[...]
        ms = jnp.mean(x * x, axis=-1, keepdims=True)
        n_scr[...] = (x * lax.rsqrt(ms + EPS) * g_ref[...]).astype(BF16)
        matmul()

    @pl.when(jnp.logical_and(j > 0, j <= n1))
    def _():
        epilogue(o1_ref)
        matmul()

    @pl.when(jnp.logical_and(j > n1, j < last))
    def _():
        epilogue(o2_ref)
        matmul()

    @pl.when(j == last)
    def _():
        epilogue(o2_ref)


def _proj(x2, gain, w, col_gain, col_flag, cols1, *, tm, tn):
    n_tok, d = x2.shape
    n_cols = w.shape[1]
    n1 = cols1 // tn
    n2 = (n_cols - cols1) // tn
    assert n1 * tn == cols1 and (n1 + n2) * tn == n_cols
    n_j = n1 + n2
    prev = lambda j: jnp.maximum(j - 1, 0)
    return pl.pallas_call(
        functools.partial(_proj_kernel, n1=n1),
        grid=(n_tok // tm, n_j + 1),
        in_specs=[pl.BlockSpec((tm, d), lambda i, j: (i, 0)),
                  pl.BlockSpec((1, d), lambda i, j: (0, 0)),
                  pl.BlockSpec((d, tn), lambda i, j: (0, jnp.minimum(j, n_j - 1))),
                  pl.BlockSpec((1, tn), lambda i, j: (0, prev(j))),
                  pl.BlockSpec((1, tn), lambda i, j: (0, prev(j)))],
        out_specs=[pl.BlockSpec((tm, tn), lambda i, j: (i, jnp.minimum(prev(j), n1 - 1))),
                   pl.BlockSpec((tm, tn), lambda i, j: (i, jnp.maximum(prev(j) - n1, 0)))],
        out_shape=[jax.ShapeDtypeStruct((n_tok, cols1), F32),
                   jax.ShapeDtypeStruct((n_tok, n_cols - cols1), BF16)],
        scratch_shapes=[pltpu.VMEM((tm, d), BF16), pltpu.VMEM((tm, tn), F32)],
        compiler_params=_params(("arbitrary", "arbitrary")),
        name="in_proj",
    )(x2, gain.reshape(1, d), w, col_gain.reshape(1, n_cols), col_flag.reshape(1, n_cols))


def _dilated_kernel(q_ref, kc_ref, kp_ref, vc_ref, vp_ref, bias_ref, o_ref,
                    k_scr, v_scr, out_scr, lse_scr):
    n = pl.program_id(2)
    k_scr[0:DIL_SPAN, :] = kp_ref[0]
    k_scr[DIL_SPAN:2 * DIL_SPAN, :] = kc_ref[0]
    v_scr[0:DIL_SPAN, :] = vp_ref[0]
    v_scr[DIL_SPAN:2 * DIL_SPAN, :] = vc_ref[0]

    for p, (_, dil) in enumerate(DILATED_PATTERNS):
        span = BAND * dil

        def unit(u, carry, p=p, dil=dil, span=span):
            s = u // dil
            r = u - s * dil
            base = s * span + r
            first = jnp.logical_and(n == 0, s == 0).astype(jnp.int32)
            q = q_ref[0, pl.ds(base, BAND, stride=dil), :].astype(BF16)
            kk = k_scr[pl.ds(DIL_SPAN + base - span, 2 * BAND, stride=dil), :].astype(BF16)
            vv = v_scr[pl.ds(DIL_SPAN + base - span, 2 * BAND, stride=dil), :].astype(BF16)
            lg = _nt_dot(q, kk) + bias_ref[p, first, 0]
            m = jnp.max(lg, axis=-1, keepdims=True)
            e = jnp.exp2(lg - m).astype(BF16)
            acc = _dot(e, jnp.concatenate([vv, jnp.ones((2 * BAND, LANES), BF16)], axis=1))
            ssum = acc[:, LANES:]
            out_scr[p, pl.ds(base, BAND, stride=dil), :] = acc[:, :LANES] / ssum
            lse_scr[p, pl.ds(base, BAND, stride=dil), :] = m + jnp.log2(ssum)
            return carry

        lax.fori_loop(0, DIL_SPAN // BAND, unit, 0, unroll=8)

    chunk = 256

    def combine(c, carry):
        rows = pl.ds(pl.multiple_of(c * chunk, chunk), chunk)
        l0, l1, l2 = lse_scr[0, rows, :], lse_scr[1, rows, :], lse_scr[2, rows, :]
        big = jnp.maximum(jnp.maximum(l0, l1), l2)
        e0, e1, e2 = jnp.exp2(l0 - big), jnp.exp2(l1 - big), jnp.exp2(l2 - big)
        num = e0 * out_scr[0, rows, :] + e1 * out_scr[1, rows, :] + e2 * out_scr[2, rows, :]
        o_ref[0, rows, :] = num / (e0 + e1 + e2)
        return carry

    lax.fori_loop(0, DIL_SPAN // chunk, combine, 0)


def _dilated(p1, bias, batch, seq):
    blk = (1, DIL_SPAN, LANES)
    prev = lambda nn: jnp.maximum(nn - 1, 0)
    return pl.pallas_call(
        _dilated_kernel,
        grid=(batch, A_HEADS, seq // DIL_SPAN),
        in_specs=[pl.BlockSpec(blk, lambda b, h, nn: (b, nn, P1_QA + h)),
                  pl.BlockSpec(blk, lambda b, h, nn: (b, nn, P1_KA + h)),
                  pl.BlockSpec(blk, lambda b, h, nn: (b, prev(nn), P1_KA + h)),
                  pl.BlockSpec(blk, lambda b, h, nn: (b, nn, P1_VA + h)),
                  pl.BlockSpec(blk, lambda b, h, nn: (b, prev(nn), P1_VA + h)),
                  pl.BlockSpec((3, 2, 1, BAND, 2 * BAND), lambda b, h, nn: (0, 0, h, 0, 0))],
        out_specs=pl.BlockSpec(blk, lambda b, h, nn: (b, nn, h)),
        out_shape=jax.ShapeDtypeStruct((batch, seq, A_WIDTH), F32),
        scratch_shapes=[pltpu.VMEM((2 * DIL_SPAN, LANES), F32),
                        pltpu.VMEM((2 * DIL_SPAN, LANES), F32),
                        pltpu.VMEM((3, DIL_SPAN, LANES), F32),
                        pltpu.VMEM((3, DIL_SPAN, LANES), F32)],
        compiler_params=_params(("arbitrary", "arbitrary", "arbitrary")),
        name="dilated_attn",
    )(p1, p1, p1, p1, p1, bias)


def _compress_kernel(t_ref, pos_ref, w1_ref, w2_ref, gain_ref, flag_ref, o_ref, lo_scr, hi_scr):
    n_blk = o_ref.shape[3]
    lo_scr[...] = jnp.zeros_like(lo_scr)
    hi_scr[...] = jnp.zeros_like(hi_scr)
    for i in range(CMP_STRIDE):
        rows = t_ref[0, pl.ds(i, n_blk, stride=CMP_STRIDE), :]
        a = (rows + pos_ref[i:i + 1, :]).astype(BF16)
        lo_scr[...] += _dot(a, w1_ref[0, i * HEAD_DIM:(i + 1) * HEAD_DIM, :])
        b = (rows + pos_ref[CMP_STRIDE + i:CMP_STRIDE + i + 1, :]).astype(BF16)
        hi_scr[...] += _dot(b, w1_ref[0, (CMP_STRIDE + i) * HEAD_DIM:(CMP_STRIDE + i + 1) * HEAD_DIM, :])
    hid = lo_scr[...] + pltpu.roll(hi_scr[...], n_blk - 1, 0)
    o = _dot(jax.nn.gelu(hid).astype(BF16), w2_ref[0])
    ms = jnp.mean(o * o, axis=-1, keepdims=True)
    inv = jnp.where(flag_ref[0] > 0.0, lax.rsqrt(ms + EPS), 1.0)
    o_ref[0, 0, 0] = o * inv * gain_ref[0]


def _compress(p1, pos, w1, w2, gain, flag, batch, seq):
    n_blk = seq // CMP_STRIDE
    return pl.pallas_call(
        _compress_kernel,
        grid=(batch, 2, NSA_KV_GROUPS),
        in_specs=[pl.BlockSpec((1, seq, LANES), lambda b, kv, g: (b, 0, P1_KC + 2 * kv + g)),
                  pl.BlockSpec((CMP_LEN, HEAD_DIM), lambda b, kv, g: (0, 0)),
                  pl.BlockSpec((1, CMP_LEN * HEAD_DIM, CMP_HIDDEN), lambda b, kv, g: (kv, 0, 0)),
                  pl.BlockSpec((1, CMP_HIDDEN, HEAD_DIM), lambda b, kv, g: (kv, 0, 0)),
                  pl.BlockSpec((1, 1, HEAD_DIM), lambda b, kv, g: (kv, 0, 0)),
                  pl.BlockSpec((1, 1, HEAD_DIM), lambda b, kv, g: (kv, 0, 0))],
        out_specs=pl.BlockSpec((1, 1, 1, n_blk, HEAD_DIM), lambda b, kv, g: (b, kv, g, 0, 0)),
        out_shape=jax.ShapeDtypeStruct((batch, 2, NSA_KV_GROUPS, n_blk, HEAD_DIM), F32),
        scratch_shapes=[pltpu.VMEM((n_blk, CMP_HIDDEN), F32), pltpu.VMEM((n_blk, CMP_HIDDEN), F32)],
        compiler_params=_params(("arbitrary", "arbitrary", "arbitrary")),
        name="nsa_compress",
    )(p1, pos, w1, w2, gain, flag)


def _nsa_kernel(q_ref, ks_ref, vs_ref, kw_ref, vw_ref, kc_ref, vc_ref, gl_ref, gb_ref,
                bias_ref, ovl_ref, et_ref, o_ref, m_scr, acc_scr, s_scr, p_scr, a_scr, ocmp_scr, imp_scr,
                lc_scr, pcb_scr, split_scr):
    qi = pl.program_id(2)
    tq, rep = NSA_TQ, NSA_REP
    n_cmp = kc_ref.shape[3]

    q2 = q_ref[0]
    q_heads = [q2[:, r * HEAD_DIM:(r + 1) * HEAD_DIM] for r in range(rep)]
    t_col = qi * tq + lax.broadcasted_iota(jnp.int32, (tq, 1), 0)

    def compressed(width):
        cmp_end = lax.broadcasted_iota(jnp.int32, (1, width), 1) * CMP_STRIDE + (CMP_LEN - 1)
        kcb = kc_ref[0, 0, 0, 0:width, :].astype(BF16)
        vcb = vc_ref[0, 0, 0, 0:width, :].astype(BF16)
        for r in range(rep):
            lc_scr[r, :, 0:width] = _nt_dot(q_heads[r], kcb)
        for c in range(tq // CMP_ROWS):
            rows = slice(c * CMP_ROWS, (c + 1) * CMP_ROWS)
            cmask = cmp_end <= t_col[rows]
            pc_sum = jnp.zeros((CMP_ROWS, width), F32)
            for r in range(rep):
                lc = jnp.where(cmask, lc_scr[r, rows, 0:width], NEG_INF)
                mc = jnp.max(lc, axis=-1, keepdims=True)
                pc = jnp.where(cmask, jnp.exp2(lc - mc), 0.0)
                pc = pc / jnp.maximum(jnp.sum(pc, axis=-1, keepdims=True), 1e-30)
                pcb_scr[r, rows, 0:width] = pc.astype(BF16)
                pc_sum = pc_sum + pc
            hi = pc_sum.astype(BF16)
            rem = pc_sum - hi.astype(F32)
            mid = rem.astype(BF16)
            split_scr[0, rows, 0:width] = hi
            split_scr[1, rows, 0:width] = mid
            split_scr[2, rows, 0:width] = (rem - mid.astype(F32)).astype(BF16)
        for r in range(rep):
            ocmp_scr[r] = _dot(pcb_scr[r, :, 0:width], vcb)
        ovl_t = ovl_ref[:, 0:width]
        imp_scr[...] = (_nt_dot(ovl_t, split_scr[0, :, 0:width]) + _nt_dot(ovl_t, split_scr[1, :, 0:width])
                        + _nt_dot(ovl_t, split_scr[2, :, 0:width]))

    n_grp = n_cmp // LANES
    need = jnp.minimum(qi // (LANES * CMP_STRIDE // tq) + 1, n_grp)
    for k in range(1, n_grp + 1):
        pl.when(need == k)(functools.partial(compressed, k * LANES))
    o_cmp = [ocmp_scr[r] for r in range(rep)]
    imp_t = imp_scr[...]

    n_wt = WIN_LEN // NSA_TK + 1
    w_keys = n_wt * NSA_TK
    j0 = jnp.maximum(qi - (n_wt - 1), 0)
    w_rows = pl.ds(pl.multiple_of(j0 * NSA_TK, NSA_TK), w_keys)
    kw_tile = kw_ref[0, w_rows, :]
    vw_aug = jnp.concatenate([vw_ref[0, w_rows, :], jnp.ones((w_keys, HEAD_DIM), BF16)], axis=1)
    dist = t_col - (j0 * NSA_TK + lax.broadcasted_iota(jnp.int32, (1, w_keys), 1))
    wmask = jnp.where(dist >= 0, dist, WIN_LEN) < WIN_LEN
    w_tiles = [jnp.clip(qi - j0 - i, 0, BIAS_FAR) for i in range(n_wt)]
    o_win = []
    for r in range(rep):
        s = _nt_dot(q_heads[r], kw_tile) + jnp.concatenate([bias_ref[t, r] for t in w_tiles], axis=-1)
        s = jnp.where(wmask, s, NEG_INF)
        p = jnp.exp2(s - jnp.max(s, axis=-1, keepdims=True)).astype(BF16)
        acc = _dot(p, vw_aug)
        o_win.append(acc[:, :HEAD_DIM] / acc[:, HEAD_DIM:])

    blk = lax.broadcasted_iota(jnp.int32, (LANES, 1), 0)
    blk_f = blk.astype(F32)
    t_row = qi * tq + lax.broadcasted_iota(jnp.int32, (1, tq), 1)
    cur = t_row >> 6
    valid = blk * SLC_LEN <= t_row
    forced = (blk == 0) | (blk == cur) | (blk == cur - 1)
    score = jnp.where(valid, jnp.where(forced, FORCE_SCORE, imp_t), -1.0)
    sel_t = jnp.zeros((LANES, tq), F32)
    for _ in range(SLC_TOPK):
        best = jnp.max(score, axis=0, keepdims=True)
        idx = jnp.min(jnp.where(score == best, blk_f, float(LANES)), axis=0, keepdims=True)
        hit = blk_f == idx
        sel_t = jnp.where(hit, 1.0, sel_t)
        score = jnp.where(hit, -3e38, score)
    unsel = (sel_t.T - 1.0).astype(BF16)
    lhs = [jnp.concatenate([q_heads[r], unsel], axis=1) for r in range(rep)]

    ones_v = jnp.ones((SEL_TK, HEAD_DIM), BF16)

    sub = SEL_TK // NSA_TK
    n_tiles = qi // sub + 1

    def key_rows(jj):
        return pl.ds(pl.multiple_of(jj * SEL_TK, SEL_TK), SEL_TK)

    def logits_stage(jj, heads):
        rows = key_rows(jj)
        rhs = jnp.concatenate([ks_ref[0, rows, :], et_ref[rows, :]], axis=1)
        for r in heads:
            s_scr[r] = _nt_dot(lhs[r], rhs)

    def softmax_stage(jj, heads):
        u = qi - sub * jj
        tiles = [jnp.where(u < i, BIAS_FUTURE, jnp.minimum(u - i, BIAS_FAR)) for i in range(sub)]
        for r in heads:
            s = s_scr[r] + jnp.concatenate([bias_ref[t, r] for t in tiles], axis=-1)
            m_old = m_scr[r]
            m_new = jnp.maximum(m_old, jnp.max(s, axis=-1, keepdims=True))
            a_scr[r] = jnp.exp2(m_old - m_new)
            p_scr[r] = jnp.exp2(s - jnp.concatenate([m_new] * (SEL_TK // LANES), axis=1)).astype(BF16)
            m_scr[r] = m_new

    def pv_stage(jj, heads):
        v_aug = jnp.concatenate([vs_ref[0, key_rows(jj), :], ones_v], axis=1)
        for r in heads:
            alpha = a_scr[r]
            acc_scr[r] = jnp.concatenate([alpha, alpha], axis=1) * acc_scr[r] + _dot(p_scr[r], v_aug)

    all_heads = range(rep)
    m_scr[...] = jnp.full(m_scr.shape, NEG_INF, F32)
    acc_scr[...] = jnp.zeros_like(acc_scr)
    p_scr[...] = jnp.zeros_like(p_scr)
    a_scr[...] = jnp.ones_like(a_scr)
    logits_stage(0, all_heads)

    def sel_step(it, carry):
        pv_stage(jnp.maximum(it - 1, 0), all_heads)
        softmax_stage(it, all_heads)
        logits_stage(jnp.minimum(it + 1, n_tiles - 1), all_heads)
        return carry

    lax.fori_loop(0, n_tiles, sel_step, 0)
    pv_stage(n_tiles - 1, all_heads)

    gates = jax.nn.sigmoid(gl_ref[0] + gb_ref[0])
    o_slc = []
    for r in range(rep):
        acc = acc_scr[r]
        o_slc.append(acc[:, :HEAD_DIM] / acc[:, HEAD_DIM:])
    for r in range(rep):
        c = r * N_BRANCH
        out = (gates[:, c:c + 1] * o_cmp[r] + gates[:, c + 1:c + 2] * o_slc[r]
               + gates[:, c + 2:c + 3] * o_win[r])
        o_ref[0, :, r * HEAD_DIM:(r + 1) * HEAD_DIM] = out


def _nsa(p1, p2, kcv, gate_bias, bias, overlap_t, membership, batch, seq):
    tq = NSA_TQ
    n_cmp = seq // CMP_STRIDE
    kv_blk = (1, seq, LANES)
    grp_w = NSA_REP * HEAD_DIM
    return pl.pallas_call(
        _nsa_kernel,
        grid=(batch, NSA_KV_GROUPS, seq // tq),
        in_specs=[pl.BlockSpec((1, tq, grp_w), lambda b, g, i: (b, i, g)),
                  pl.BlockSpec(kv_blk, lambda b, g, i: (b, 0, P2_KS + g)),
                  pl.BlockSpec(kv_blk, lambda b, g, i: (b, 0, P2_VS + g)),
                  pl.BlockSpec(kv_blk, lambda b, g, i: (b, 0, P2_KW + g)),
                  pl.BlockSpec(kv_blk, lambda b, g, i: (b, 0, P2_VW + g)),
                  pl.BlockSpec((1, 1, 1, n_cmp, HEAD_DIM), lambda b, g, i: (b, 0, g, 0, 0)),
                  pl.BlockSpec((1, 1, 1, n_cmp, HEAD_DIM), lambda b, g, i: (b, 1, g, 0, 0)),
                  pl.BlockSpec((1, tq, LANES), lambda b, g, i: (b, i, P1_GL + g)),
                  pl.BlockSpec((1, 1, LANES), lambda b, g, i: (g, 0, 0)),
                  pl.BlockSpec((N_BIAS_TILES, NSA_REP, NSA_TQ, NSA_TK), lambda b, g, i: (0, g, 0, 0)),
                  pl.BlockSpec((LANES, n_cmp), lambda b, g, i: (0, 0)),
                  pl.BlockSpec((seq, LANES), lambda b, g, i: (0, 0))],
        out_specs=pl.BlockSpec((1, tq, grp_w), lambda b, g, i: (b, i, g)),
        out_shape=jax.ShapeDtypeStruct((batch, seq, NSA_WIDTH), F32),
        scratch_shapes=[pltpu.VMEM((NSA_REP, tq, LANES), F32),
                        pltpu.VMEM((NSA_REP, tq, 2 * HEAD_DIM), F32),
                        pltpu.VMEM((NSA_REP, tq, SEL_TK), F32),
                        pltpu.VMEM((NSA_REP, tq, SEL_TK), BF16),
                        pltpu.VMEM((NSA_REP, tq, LANES), F32),
                        pltpu.VMEM((NSA_REP, tq, HEAD_DIM), F32),
                        pltpu.VMEM((LANES, tq), F32),
                        pltpu.VMEM((NSA_REP, tq, n_cmp), F32),
                        pltpu.VMEM((NSA_REP, tq, n_cmp), BF16),
                        pltpu.VMEM((3, tq, n_cmp), BF16)],
        compiler_params=_params(("arbitrary", "arbitrary", "arbitrary")),
        name="nsa_attn",
    )(p2, p2, p2, p2, p2, kcv, kcv, p1, gate_bias, bias, overlap_t, membership)


def _out_kernel(x_ref, oa_ref, on_ref, ga_ref, gn_ref, w_ref, o_ref):
    def norm(t, g):
        ms = jnp.mean(t * t, axis=-1, keepdims=True)
        return (t * lax.rsqrt(ms + EPS) * g).astype(BF16)

    ya = norm(oa_ref[...], ga_ref[...])
    yn = norm(on_ref[...], gn_ref[...])
    y = _dot(ya, w_ref[0:A_WIDTH, :]) + _dot(yn, w_ref[A_WIDTH:A_WIDTH + NSA_WIDTH, :])
    o_ref[...] = x_ref[...] + y


def _out_proj(x2, oa, on, gain, w, *, tm):
    n_tok, d = x2.shape
    return pl.pallas_call(
        _out_kernel,
        grid=(n_tok // tm,),
        in_specs=[pl.BlockSpec((tm, d), lambda i: (i, 0)),
                  pl.BlockSpec((tm, A_WIDTH), lambda i: (i, 0)),
                  pl.BlockSpec((tm, NSA_WIDTH), lambda i: (i, 0)),
                  pl.BlockSpec((1, A_WIDTH), lambda i: (0, 0)),
                  pl.BlockSpec((1, NSA_WIDTH), lambda i: (0, 0)),
                  pl.BlockSpec((d, d), lambda i: (0, 0))],
        out_specs=pl.BlockSpec((tm, d), lambda i: (i, 0)),
        out_shape=jax.ShapeDtypeStruct((n_tok, d), F32),
        compiler_params=_params(("arbitrary",)),
        name="out_proj",
    )(x2, oa, on, gain[:A_WIDTH].reshape(1, A_WIDTH), gain[A_WIDTH:].reshape(1, NSA_WIDTH), w)


def _pack_w_in(w_in, gate_bias, q_norm_a, k_norm_a, q_norm_nsa, k_norm_nsa):
    scale = HEAD_DIM ** -0.5
    sizes = (A_WIDTH,) * 3 + (NSA_WIDTH,) + (KV_WIDTH,) * 6 + (N_BRANCH * NSA_HEADS,)
    offs = np.concatenate([[0], np.cumsum(sizes)])
    qa, ka, va, qn, kc, vc, k_s, v_s, k_w, v_w, gl = [w_in[:, offs[i]:offs[i + 1]] for i in range(11)]
    per_grp = NSA_REP * N_BRANCH
    pad = jnp.zeros((w_in.shape[0], LANES - per_grp), w_in.dtype)
    gl_cols = [jnp.concatenate([gl[:, g * per_grp:(g + 1) * per_grp], pad], axis=1) for g in range(NSA_KV_GROUPS)]
    n_pad = P1_HEADS - P1_USED
    w_pad = jnp.zeros((w_in.shape[0], n_pad * LANES), w_in.dtype)
    w = jnp.concatenate([qa, ka, va, kc, vc] + gl_cols + [w_pad, qn, k_s, v_s, k_w, v_w], axis=1).astype(BF16)

    ones = jnp.ones((HEAD_DIM,), F32)
    tile = lambda v, n: jnp.tile(v, n)
    gain = jnp.concatenate([tile(q_norm_a * (scale * LOG2E), A_HEADS), tile(k_norm_a, A_HEADS),
                            tile(ones, A_HEADS + 6 + n_pad),
                            tile(q_norm_nsa * (scale * LOG2E), NSA_HEADS), tile(k_norm_nsa, 2), tile(ones, 2),
                            tile(k_norm_nsa, 2), tile(ones, 2)])
    flag = jnp.concatenate([jnp.ones((2 * A_WIDTH,), F32), jnp.zeros((A_WIDTH + (6 + n_pad) * HEAD_DIM,), F32),
                            jnp.ones((NSA_WIDTH + KV_WIDTH,), F32), jnp.zeros((KV_WIDTH,), F32),
                            jnp.ones((KV_WIDTH,), F32), jnp.zeros((KV_WIDTH,), F32)])
    gb = jnp.stack([jnp.concatenate([gate_bias[g * per_grp:(g + 1) * per_grp],
                                     jnp.zeros((LANES - per_grp,), F32)]) for g in range(NSA_KV_GROUPS)])
    return w, gain, flag, gb.reshape(NSA_KV_GROUPS, 1, LANES)


def _forward(x, ffn1_norm, ffn1_w1, ffn1_w3, ffn1_w2, mix_norm, w_in, gate_bias, q_norm_a, k_norm_a,
             q_norm_nsa, k_norm_nsa, cmp_pos, cmp_k_w1, cmp_k_w2, cmp_v_w1, cmp_v_w2, out_norm, w_out,
             ffn2_norm, ffn2_w1, ffn2_w3, ffn2_w2, rel_bias, *, tm_ffn, tf_ffn, tm_proj, tm_out):
    batch, seq, d = x.shape
    depth = ffn1_w1.shape[0]
    n_tok = batch * seq
    assert seq % DIL_SPAN == 0 and seq // SLC_LEN <= LANES

    bias_a = _expand_bias(rel_bias[:, :A_HEADS], jnp.asarray(_dilated_buckets()), scale=LOG2E)
    bias_a = bias_a.reshape(3, 2, A_HEADS, BAND, 2 * BAND)
    bias_n = _expand_bias(rel_bias[:, A_HEADS:], jnp.asarray(_nsa_buckets()),
                          scale=LOG2E, shift_far=True)
    overlap_t = jnp.asarray(_overlap_t_np(seq // CMP_STRIDE)).astype(BF16)
    membership = jnp.asarray(_block_membership_np(seq)).astype(BF16)

    x2 = x.reshape(n_tok, d)
    for l in range(depth):
        x2 = _ffn(x2, ffn1_norm[l], ffn1_w1[l].astype(BF16), ffn1_w3[l].astype(BF16),
                  ffn1_w2[l].astype(BF16), tm=tm_ffn, tf=tf_ffn)

        w_p, g_p, f_p, gb = _pack_w_in(w_in[l], gate_bias[l], q_norm_a[l], k_norm_a[l],
                                       q_norm_nsa[l], k_norm_nsa[l])
        p1, p2 = _proj(x2, mix_norm[l], w_p, g_p, f_p, P1_HEADS * LANES, tm=tm_proj, tn=512)
        p1 = p1.reshape(batch, seq, P1_HEADS * LANES)
        p2 = p2.reshape(batch, seq, P2_HEADS * LANES)

        o_a = _dilated(p1, bias_a, batch, seq)

        cmp_w1 = jnp.stack([cmp_k_w1[l], cmp_v_w1[l]]).astype(BF16)
        cmp_w2 = jnp.stack([cmp_k_w2[l], cmp_v_w2[l]]).astype(BF16)
        cmp_gain = jnp.stack([k_norm_nsa[l], jnp.ones((HEAD_DIM,), F32)]).reshape(2, 1, HEAD_DIM)
        cmp_flag = jnp.stack([jnp.ones((HEAD_DIM,), F32), jnp.zeros((HEAD_DIM,), F32)]).reshape(2, 1, HEAD_DIM)
        kcv = _compress(p1, cmp_pos[l], cmp_w1, cmp_w2, cmp_gain, cmp_flag, batch, seq)

        o_n = _nsa(p1, p2, kcv, gb, bias_n, overlap_t, membership, batch, seq)

        x2 = _out_proj(x2, o_a.reshape(n_tok, A_WIDTH), o_n.reshape(n_tok, NSA_WIDTH), out_norm[l],
                       w_out[l].astype(BF16), tm=tm_out)

        x2 = _ffn(x2, ffn2_norm[l], ffn2_w1[l].astype(BF16), ffn2_w3[l].astype(BF16),
                  ffn2_w2[l].astype(BF16), tm=tm_ffn, tf=tf_ffn)
    return x2.reshape(batch, seq, d)


def kernel(x, ffn1_norm, ffn1_w1, ffn1_w3, ffn1_w2, mix_norm, w_in, gate_bias, q_norm_a, k_norm_a, q_norm_nsa, k_norm_nsa, cmp_pos, cmp_k_w1, cmp_k_w2, cmp_v_w1, cmp_v_w2, out_norm, w_out, ffn2_norm, ffn2_w1, ffn2_w3, ffn2_w2, rel_bias):
    return _forward(x, ffn1_norm, ffn1_w1, ffn1_w3, ffn1_w2, mix_norm, w_in, gate_bias, q_norm_a, k_norm_a,
                    q_norm_nsa, k_norm_nsa, cmp_pos, cmp_k_w1, cmp_k_w2, cmp_v_w1, cmp_v_w2, out_norm, w_out,
                    ffn2_norm, ffn2_w1, ffn2_w3, ffn2_w2, rel_bias,
                    tm_ffn=512, tf_ffn=512, tm_proj=1024, tm_out=256)
```

```python
import functools
import math

import numpy as np
import jax
import jax.numpy as jnp
from jax import lax
from jax.experimental import pallas as pl
from jax.experimental.pallas import tpu as pltpu

D_MODEL = 2048
HEAD_DIM = 128
N_HEADS = 16
A_HEADS = 8
NSA_HEADS = 8
NSA_KV_GROUPS = 2
NSA_REP = 4
A_WIDTH = A_HEADS * HEAD_DIM
NSA_WIDTH = NSA_HEADS * HEAD_DIM
KV_WIDTH = NSA_KV_GROUPS * HEAD_DIM
N_BRANCH = 3
D_FF = 5632
DILATED_PATTERNS = ((128, 1), (512, 4), (2048, 16))
BAND = 128
CMP_STRIDE = 16
CMP_LEN = 32
CMP_HIDDEN = 512
SLC_LEN = 64
SLC_TOPK = 16
WIN_LEN = 512
N_BUCKETS = 32
MAX_DISTANCE = 2048
FORCE_SCORE = 1e6
NEG_INF = -1e30
EPS = 1e-6
LOG2E = 1.4426950408889634

LANES = 128
DIL_SPAN = 2048
NSA_TQ = 256
NSA_TK = 128
SEL_TK = 512
CMP_ROWS = 32
BIAS_FAR = 13
BIAS_FUTURE = 14
N_BIAS_TILES = 15
MASK_BIG = 1e30
VMEM_LIMIT = 56 * 1024 * 1024

F32 = jnp.float32
BF16 = jnp.bfloat16

P1_QA, P1_KA, P1_VA, P1_KC, P1_VC, P1_GL = 0, 8, 16, 24, 26, 28
P1_USED = 30
P1_HEADS = 32
P2_QN, P2_KS, P2_VS, P2_KW, P2_VW = 0, 8, 10, 12, 14
P2_HEADS = 16


def _params(sem, flags=None):
    return pltpu.CompilerParams(dimension_semantics=sem, vmem_limit_bytes=VMEM_LIMIT, flags=flags)


def _nt_dot(a, b):
    return lax.dot_general(a, b, (((1,), (1,)), ((), ())), preferred_element_type=F32)


def _dot(a, b):
    return jnp.dot(a, b, preferred_element_type=F32)


def _rel_bucket_np(dist):
    n = np.maximum(dist, 0)
    max_exact = N_BUCKETS // 2
    nf = np.maximum(n, 1).astype(np.float32)
    log_b = max_exact + (np.log(nf / np.float32(max_exact)) / np.float32(math.log(MAX_DISTANCE / max_exact))
                         * np.float32(N_BUCKETS - max_exact)).astype(np.int32)
    return np.where(n < max_exact, n, np.minimum(log_b, N_BUCKETS - 1)).astype(np.int32)


def _dilated_buckets():
    i = np.arange(BAND)[:, None]
    j = np.arange(2 * BAND)[None, :] - BAND
    diff = i - j
    out = []
    for window, dilation in DILATED_PATTERNS:
        mask = (diff >= 0) & (diff <= window // dilation)
        bkt = _rel_bucket_np(diff * dilation)
        out.append(np.where(mask, bkt, -1))
        out.append(np.where(mask & (j >= 0), bkt, -1))
    return np.stack(out).astype(np.int32)


def _nsa_buckets():
    i = np.arange(NSA_TK)[:, None]
    j = np.arange(NSA_TK)[None, :]
    tiles = []
    for m in range(BIAS_FAR + 1):
        dist = NSA_TK * m + i - j
        tiles.append(np.where(dist >= 0, _rel_bucket_np(dist), -1))
    assert (tiles[BIAS_FAR] == N_BUCKETS - 1).all()
    tiles.append(np.full((NSA_TK, NSA_TK), -1))
    assert len(tiles) == N_BIAS_TILES and BIAS_FUTURE == N_BIAS_TILES - 1
    return np.stack(tiles).astype(np.int32)


def _overlap_t_np(n_cmp_pad):
    c_start = np.arange(n_cmp_pad)[None, :] * CMP_STRIDE
    s_start = np.arange(LANES)[:, None] * SLC_LEN
    return ((c_start < s_start + SLC_LEN) & (c_start + CMP_LEN > s_start)).astype(np.float32)


def _block_membership_np(seq):
    k_blk = np.arange(seq)[:, None] // SLC_LEN
    return np.where(k_blk == np.arange(LANES)[None, :], MASK_BIG, 0.0).astype(np.float32)


def _bias_kernel(tbl_ref, bkt_ref, o_ref, *, n_heads, scale, shift_far):
    bkt = bkt_ref[0]
    for h in range(n_heads):
        acc = jnp.full(bkt.shape, NEG_INF, F32)
        base = tbl_ref[N_BUCKETS - 1, h] if shift_far else 0.0
        for b in range(N_BUCKETS):
            acc = jnp.where(bkt == b, (tbl_ref[b, h] - base) * scale, acc)
        o_ref[0, h] = acc


def _expand_bias(table, buckets, scale=1.0, shift_far=False):
    n_t, rows, width = buckets.shape
    nh = table.shape[1]
    return pl.pallas_call(
        functools.partial(_bias_kernel, n_heads=nh, scale=scale, shift_far=shift_far),
        grid=(n_t,),
        in_specs=[pl.BlockSpec(memory_space=pltpu.SMEM),
                  pl.BlockSpec((1, rows, width), lambda t: (t, 0, 0))],
        out_specs=pl.BlockSpec((1, nh, rows, width), lambda t: (t, 0, 0, 0)),
        out_shape=jax.ShapeDtypeStruct((n_t, nh, rows, width), F32),
        compiler_params=_params(("arbitrary",)),
        name="bias_expand",
    )(table, buckets)


def _ffn_kernel(x_ref, g_ref, w1_ref, w3_ref, w2_ref, o_ref, n_scr):
    j = pl.program_id(1)

    @pl.when(j == 0)
    def _():
        x = x_ref[...]
        ms = jnp.mean(x * x, axis=-1, keepdims=True)
        n_scr[...] = (x * lax.rsqrt(ms + EPS) * g_ref[...]).astype(BF16)
        o_ref[...] = jnp.zeros_like(o_ref)

    n = n_scr[...]
    h1 = _dot(n, w1_ref[...])
    h3 = _dot(n, w3_ref[...])
    a = (h1 * jax.nn.sigmoid(h1) * h3).astype(BF16)
    o_ref[...] += _dot(a, w2_ref[...])

    @pl.when(j == pl.num_programs(1) - 1)
    def _():
        o_ref[...] = x_ref[...] + 0.5 * o_ref[...]


def _ffn(x2, gain, w1, w3, w2, *, tm, tf):
    n_tok, d = x2.shape
    f = w1.shape[1]
    return pl.pallas_call(
        _ffn_kernel,
        grid=(n_tok // tm, f // tf),
        in_specs=[pl.BlockSpec((tm, d), lambda i, j: (i, 0)),
                  pl.BlockSpec((1, d), lambda i, j: (0, 0)),
                  pl.BlockSpec((d, tf), lambda i, j: (0, j)),
                  pl.BlockSpec((d, tf), lambda i, j: (0, j)),
                  pl.BlockSpec((tf, d), lambda i, j: (j, 0))],
        out_specs=pl.BlockSpec((tm, d), lambda i, j: (i, 0)),
        out_shape=jax.ShapeDtypeStruct((n_tok, d), F32),
        scratch_shapes=[pltpu.VMEM((tm, d), BF16)],
        compiler_params=_params(("arbitrary", "arbitrary")),
        name="ffn",
    )(x2, gain.reshape(1, d), w1, w3, w2)


def _proj_kernel(x_ref, g_ref, w_ref, cg_ref, cf_ref, o1_ref, o2_ref, n_scr, acc_scr, *, n1):
    j = pl.program_id(1)
    last = pl.num_programs(1) - 1

    def matmul():
        acc_scr[...] = _dot(n_scr[...], w_ref[...])

    def epilogue(o_ref):
        for h in range(acc_scr.shape[1] // LANES):
            cols = slice(h * LANES, (h + 1) * LANES)
            a = acc_scr[:, cols]
            ms = jnp.mean(a * a, axis=-1, keepdims=True)
            inv = jnp.where(cf_ref[:, cols] > 0.0, lax.rsqrt(ms + EPS), 1.0)
            o_ref[:, cols] = (a * inv * cg_ref[:, cols]).astype(o_ref.dtype)

    @pl.when(j == 0)
    def _():
        x = x_ref[...]
        ms = jnp.mean(x * x, axis=-1, keepdims=True)
        n_scr[...] = (x * lax.rsqrt(ms + EPS) * g_ref[...]).astype(BF16)
        matmul()

    @pl.when(jnp.logical_and(j > 0, j <= n1))
    def _():
        epilogue(o1_ref)
        matmul()

    @pl.when(jnp.logical_and(j > n1, j < last))
    def _():
        epilogue(o2_ref)
        matmul()

    @pl.when(j == last)
    def _():
        epilogue(o2_ref)


def _proj(x2, gain, w, col_gain, col_flag, cols1, *, tm, tn):
    n_tok, d = x2.shape
    n_cols = w.shape[1]
    n1 = cols1 // tn
    n2 = (n_cols - cols1) // tn
    assert n1 * tn == cols1 and (n1 + n2) * tn == n_cols
    n_j = n1 + n2
    prev = lambda j: jnp.maximum(j - 1, 0)
    return pl.pallas_call(
        functools.partial(_proj_kernel, n1=n1),
        grid=(n_tok // tm, n_j + 1),
        in_specs=[pl.BlockSpec((tm, d), lambda i, j: (i, 0)),
                  pl.BlockSpec((1, d), lambda i, j: (0, 0)),
                  pl.BlockSpec((d, tn), lambda i, j: (0, jnp.minimum(j, n_j - 1))),
                  pl.BlockSpec((1, tn), lambda i, j: (0, prev(j))),
                  pl.BlockSpec((1, tn), lambda i, j: (0, prev(j)))],
        out_specs=[pl.BlockSpec((tm, tn), lambda i, j: (i, jnp.minimum(prev(j), n1 - 1))),
                   pl.BlockSpec((tm, tn), lambda i, j: (i, jnp.maximum(prev(j) - n1, 0)))],
        out_shape=[jax.ShapeDtypeStruct((n_tok, cols1), F32),
                   jax.ShapeDtypeStruct((n_tok, n_cols - cols1), BF16)],
        scratch_shapes=[pltpu.VMEM((tm, d), BF16), pltpu.VMEM((tm, tn), F32)],
        compiler_params=_params(("arbitrary", "arbitrary")),
        name="in_proj",
    )(x2, gain.reshape(1, d), w, col_gain.reshape(1, n_cols), col_flag.reshape(1, n_cols))


def _dilated_kernel(q_ref, kc_ref, kp_ref, vc_ref, vp_ref, bias_ref, o_ref,
                    k_scr, v_scr, out_scr, lse_scr):
    n = pl.program_id(2)
    k_scr[0:DIL_SPAN, :] = kp_ref[0]
    k_scr[DIL_SPAN:2 * DIL_SPAN, :] = kc_ref[0]
    v_scr[0:DIL_SPAN, :] = vp_ref[0]
    v_scr[DIL_SPAN:2 * DIL_SPAN, :] = vc_ref[0]

    for p, (_, dil) in enumerate(DILATED_PATTERNS):
        span = BAND * dil

        def unit(u, carry, p=p, dil=dil, span=span):
            s = u // dil
            r = u - s * dil
            base = s * span + r
            first = jnp.logical_and(n == 0, s == 0).astype(jnp.int32)
            q = q_ref[0, pl.ds(base, BAND, stride=dil), :].astype(BF16)
            kk = k_scr[pl.ds(DIL_SPAN + base - span, 2 * BAND, stride=dil), :].astype(BF16)
            vv = v_scr[pl.ds(DIL_SPAN + base - span, 2 * BAND, stride=dil), :].astype(BF16)
            lg = _nt_dot(q, kk) + bias_ref[p, first, 0]
            m = jnp.max(lg, axis=-1, keepdims=True)
            e = jnp.exp2(lg - m).astype(BF16)
            acc = _dot(e, jnp.concatenate([vv, jnp.ones((2 * BAND, LANES), BF16)], axis=1))
            ssum = acc[:, LANES:]
            out_scr[p, pl.ds(base, BAND, stride=dil), :] = acc[:, :LANES] / ssum
            lse_scr[p, pl.ds(base, BAND, stride=dil), :] = m + jnp.log2(ssum)
            return carry

        lax.fori_loop(0, DIL_SPAN // BAND, unit, 0, unroll=8)

    chunk = 256

    def combine(c, carry):
        rows = pl.ds(pl.multiple_of(c * chunk, chunk), chunk)
        l0, l1, l2 = lse_scr[0, rows, :], lse_scr[1, rows, :], lse_scr[2, rows, :]
        big = jnp.maximum(jnp.maximum(l0, l1), l2)
        e0, e1, e2 = jnp.exp2(l0 - big), jnp.exp2(l1 - big), jnp.exp2(l2 - big)
        num = e0 * out_scr[0, rows, :] + e1 * out_scr[1, rows, :] + e2 * out_scr[2, rows, :]
        o_ref[0, rows, :] = num / (e0 + e1 + e2)
        return carry

    lax.fori_loop(0, DIL_SPAN // chunk, combine, 0)


def _dilated(p1, bias, batch, seq):
    blk = (1, DIL_SPAN, LANES)
    prev = lambda nn: jnp.maximum(nn - 1, 0)
    return pl.pallas_call(
        _dilated_kernel,
        grid=(batch, A_HEADS, seq // DIL_SPAN),
        in_specs=[pl.BlockSpec(blk, lambda b, h, nn: (b, nn, P1_QA + h)),
                  pl.BlockSpec(blk, lambda b, h, nn: (b, nn, P1_KA + h)),
                  pl.BlockSpec(blk, lambda b, h, nn: (b, prev(nn), P1_KA + h)),
                  pl.BlockSpec(blk, lambda b, h, nn: (b, nn, P1_VA + h)),
                  pl.BlockSpec(blk, lambda b, h, nn: (b, prev(nn), P1_VA + h)),
                  pl.BlockSpec((3, 2, 1, BAND, 2 * BAND), lambda b, h, nn: (0, 0, h, 0, 0))],
        out_specs=pl.BlockSpec(blk, lambda b, h, nn: (b, nn, h)),
        out_shape=jax.ShapeDtypeStruct((batch, seq, A_WIDTH), F32),
        scratch_shapes=[pltpu.VMEM((2 * DIL_SPAN, LANES), F32),
                        pltpu.VMEM((2 * DIL_SPAN, LANES), F32),
                        pltpu.VMEM((3, DIL_SPAN, LANES), F32),
                        pltpu.VMEM((3, DIL_SPAN, LANES), F32)],
        compiler_params=_params(("arbitrary", "arbitrary", "arbitrary")),
        name="dilated_attn",
    )(p1, p1, p1, p1, p1, bias)


def _compress_kernel(t_ref, pos_ref, w1_ref, w2_ref, gain_ref, flag_ref, o_ref, lo_scr, hi_scr):
    n_blk = o_ref.shape[3]
    lo_scr[...] = jnp.zeros_like(lo_scr)
    hi_scr[...] = jnp.zeros_like(hi_scr)
    for i in range(CMP_STRIDE):
        rows = t_ref[0, pl.ds(i, n_blk, stride=CMP_STRIDE), :]
        a = (rows + pos_ref[i:i + 1, :]).astype(BF16)
        lo_scr[...] += _dot(a, w1_ref[0, i * HEAD_DIM:(i + 1) * HEAD_DIM, :])
        b = (rows + pos_ref[CMP_STRIDE + i:CMP_STRIDE + i + 1, :]).astype(BF16)
        hi_scr[...] += _dot(b, w1_ref[0, (CMP_STRIDE + i) * HEAD_DIM:(CMP_STRIDE + i + 1) * HEAD_DIM, :])
    hid = lo_scr[...] + pltpu.roll(hi_scr[...], n_blk - 1, 0)
    o = _dot(jax.nn.gelu(hid).astype(BF16), w2_ref[0])
    ms = jnp.mean(o * o, axis=-1, keepdims=True)
    inv = jnp.where(flag_ref[0] > 0.0, lax.rsqrt(ms + EPS), 1.0)
    o_ref[0, 0, 0] = o * inv * gain_ref[0]


def _compress(p1, pos, w1, w2, gain, flag, batch, seq):
    n_blk = seq // CMP_STRIDE
    return pl.pallas_call(
        _compress_kernel,
        grid=(batch, 2, NSA_KV_GROUPS),
        in_specs=[pl.BlockSpec((1, seq, LANES), lambda b, kv, g: (b, 0, P1_KC + 2 * kv + g)),
                  pl.BlockSpec((CMP_LEN, HEAD_DIM), lambda b, kv, g: (0, 0)),
                  pl.BlockSpec((1, CMP_LEN * HEAD_DIM, CMP_HIDDEN), lambda b, kv, g: (kv, 0, 0)),
                  pl.BlockSpec((1, CMP_HIDDEN, HEAD_DIM), lambda b, kv, g: (kv, 0, 0)),
                  pl.BlockSpec((1, 1, HEAD_DIM), lambda b, kv, g: (kv, 0, 0)),
                  pl.BlockSpec((1, 1, HEAD_DIM), lambda b, kv, g: (kv, 0, 0))],
        out_specs=pl.BlockSpec((1, 1, 1, n_blk, HEAD_DIM), lambda b, kv, g: (b, kv, g, 0, 0)),
        out_shape=jax.ShapeDtypeStruct((batch, 2, NSA_KV_GROUPS, n_blk, HEAD_DIM), F32),
        scratch_shapes=[pltpu.VMEM((n_blk, CMP_HIDDEN), F32), pltpu.VMEM((n_blk, CMP_HIDDEN), F32)],
        compiler_params=_params(("arbitrary", "arbitrary", "arbitrary")),
        name="nsa_compress",
    )(p1, pos, w1, w2, gain, flag)


def _nsa_kernel(q_ref, ks_ref, vs_ref, kw_ref, vw_ref, kc_ref, vc_ref, gl_ref, gb_ref,
                bias_ref, ovl_ref, et_ref, o_ref, m_scr, acc_scr, s_scr, p_scr, a_scr, ocmp_scr, imp_scr,
                lc_scr, pcb_scr, split_scr):
    qi = pl.program_id(2)
    tq, rep = NSA_TQ, NSA_REP
    n_cmp = kc_ref.shape[3]

    q2 = q_ref[0]
    q_heads = [q2[:, r * HEAD_DIM:(r + 1) * HEAD_DIM] for r in range(rep)]
    t_col = qi * tq + lax.broadcasted_iota(jnp.int32, (tq, 1), 0)

    def compressed(width):
        cmp_end = lax.broadcasted_iota(jnp.int32, (1, width), 1) * CMP_STRIDE + (CMP_LEN - 1)
        kcb = kc_ref[0, 0, 0, 0:width, :].astype(BF16)
        vcb = vc_ref[0, 0, 0, 0:width, :].astype(BF16)
        for r in range(rep):
            lc_scr[r, :, 0:width] = _nt_dot(q_heads[r], kcb)
        for c in range(tq // CMP_ROWS):
            rows = slice(c * CMP_ROWS, (c + 1) * CMP_ROWS)
            cmask = cmp_end <= t_col[rows]
            pc_sum = jnp.zeros((CMP_ROWS, width), F32)
            for r in range(rep):
                lc = jnp.where(cmask, lc_scr[r, rows, 0:width], NEG_INF)
                mc = jnp.max(lc, axis=-1, keepdims=True)
                pc = jnp.where(cmask, jnp.exp2(lc - mc), 0.0)
                pc = pc / jnp.maximum(jnp.sum(pc, axis=-1, keepdims=True), 1e-30)
                pcb_scr[r, rows, 0:width] = pc.astype(BF16)
                pc_sum = pc_sum + pc
            hi = pc_sum.astype(BF16)
            rem = pc_sum - hi.astype(F32)
            mid = rem.astype(BF16)
            split_scr[0, rows, 0:width] = hi
            split_scr[1, rows, 0:width] = mid
            split_scr[2, rows, 0:width] = (rem - mid.astype(F32)).astype(BF16)
        for r in range(rep):
            ocmp_scr[r] = _dot(pcb_scr[r, :, 0:width], vcb)
        ovl_t = ovl_ref[:, 0:width]
        imp_scr[...] = (_nt_dot(ovl_t, split_scr[0, :, 0:width]) + _nt_dot(ovl_t, split_scr[1, :, 0:width])
                        + _nt_dot(ovl_t, split_scr[2, :, 0:width]))

    n_grp = n_cmp // LANES
    need = jnp.minimum(((qi + 1) * tq - CMP_LEN) // (CMP_STRIDE * LANES) + 1, n_grp)
    for k in range(1, n_grp + 1):
        pl.when(need == k)(functools.partial(compressed, k * LANES))
    o_cmp = [ocmp_scr[r] for r in range(rep)]
    imp_t = imp_scr[...]

    n_rb = tq // NSA_TK
    n_wt = WIN_LEN // NSA_TK + n_rb
    w_keys = n_wt * NSA_TK
    j0 = jnp.maximum(qi * n_rb - WIN_LEN // NSA_TK, 0)
    w_rows = pl.ds(pl.multiple_of(j0 * NSA_TK, NSA_TK), w_keys)
    kw_tile = kw_ref[0, w_rows, :]
    vw_aug = jnp.concatenate([vw_ref[0, w_rows, :], jnp.ones((w_keys, HEAD_DIM), BF16)], axis=1)
    kpos = j0 * NSA_TK + lax.broadcasted_iota(jnp.int32, (1, w_keys), 1)
    wmasks = []
    for a in range(n_rb):
        dist = t_col[a * NSA_TK:(a + 1) * NSA_TK] - kpos
        wmasks.append(jnp.where(dist >= 0, dist, WIN_LEN) < WIN_LEN)
    o_win = []
    for r in range(rep):
        s = _nt_dot(q_heads[r], kw_tile)
        p = []
        for a in range(n_rb):
            rows = slice(a * NSA_TK, (a + 1) * NSA_TK)
            tiles = [jnp.clip(qi * n_rb + a - j0 - i, 0, BIAS_FAR) for i in range(n_wt)]
            sa = s[rows] + jnp.concatenate([bias_ref[t, r] for t in tiles], axis=-1)
            sa = jnp.where(wmasks[a], sa, NEG_INF)
            p.append(jnp.exp2(sa - jnp.max(sa, axis=-1, keepdims=True)).astype(BF16))
        acc = _dot(jnp.concatenate(p, axis=0), vw_aug)
        o_win.append(acc[:, :HEAD_DIM] / acc[:, HEAD_DIM:])

    blk = lax.broadcasted_iota(jnp.int32, (LANES, 1), 0)
    blk_f = blk.astype(F32)
    t_row = qi * tq + lax.broadcasted_iota(jnp.int32, (1, tq), 1)
    cur = t_row >> 6
    valid = blk * SLC_LEN <= t_row
    forced = (blk == 0) | (blk == cur) | (blk == cur - 1)
    score = jnp.where(valid, jnp.where(forced, FORCE_SCORE, imp_t), -1.0)
    sel_t = jnp.zeros((LANES, tq), F32)
    for _ in range(SLC_TOPK):
        best = jnp.max(score, axis=0, keepdims=True)
        idx = jnp.min(jnp.where(score == best, blk_f, float(LANES)), axis=0, keepdims=True)
        hit = blk_f == idx
        sel_t = jnp.where(hit, 1.0, sel_t)
        score = jnp.where(hit, -3e38, score)
    unsel = (sel_t.T - 1.0).astype(BF16)
    lhs = [jnp.concatenate([q_heads[r], unsel], axis=1) for r in range(rep)]

    ones_v = jnp.ones((SEL_TK, HEAD_DIM), BF16)

    sub = SEL_TK // NSA_TK
    n_tiles = ((qi + 1) * n_rb - 1) // sub + 1

    def key_rows(jj):
        return pl.ds(pl.multiple_of(jj * SEL_TK, SEL_TK), SEL_TK)

    def logits_stage(jj, heads):
        rows = key_rows(jj)
        rhs = jnp.concatenate([ks_ref[0, rows, :], et_ref[rows, :]], axis=1)
        for r in heads:
            s_scr[r] = _nt_dot(lhs[r], rhs)

    def softmax_stage(jj, heads):
        for a in range(n_rb):
            rows = slice(a * NSA_TK, (a + 1) * NSA_TK)
            u = qi * n_rb + a - sub * jj
            tiles = [jnp.where(u < i, BIAS_FUTURE, jnp.minimum(u - i, BIAS_FAR)) for i in range(sub)]
            for r in heads:
                s = s_scr[r, rows, :] + jnp.concatenate([bias_ref[t, r] for t in tiles], axis=-1)
                m_old = m_scr[r, rows, :]
                m_new = jnp.maximum(m_old, jnp.max(s, axis=-1, keepdims=True))
                a_scr[r, rows, :] = jnp.exp2(m_old - m_new)
                p_scr[r, rows, :] = jnp.exp2(s - jnp.concatenate([m_new] * sub, axis=1)).astype(BF16)
                m_scr[r, rows, :] = m_new

    def pv_stage(jj, heads):
        v_aug = jnp.concatenate([vs_ref[0, key_rows(jj), :], ones_v], axis=1)
        for r in heads:
            alpha = a_scr[r]
            acc_scr[r] = jnp.concatenate([alpha, alpha], axis=1) * acc_scr[r] + _dot(p_scr[r], v_aug)

    all_heads = range(rep)
    m_scr[...] = jnp.full(m_scr.shape, NEG_INF, F32)
    acc_scr[...] = jnp.zeros_like(acc_scr)
    p_scr[...] = jnp.zeros_like(p_scr)
    a_scr[...] = jnp.ones_like(a_scr)
    logits_stage(0, all_heads)

    def sel_step(it, carry):
        pv_stage(jnp.maximum(it - 1, 0), all_heads)
        softmax_stage(it, all_heads)
        logits_stage(jnp.minimum(it + 1, n_tiles - 1), all_heads)
        return carry

    lax.fori_loop(0, n_tiles, sel_step, 0)
    pv_stage(n_tiles - 1, all_heads)

    gates = jax.nn.sigmoid(gl_ref[0] + gb_ref[0])
    o_slc = []
    for r in range(rep):
        acc = acc_scr[r]
        o_slc.append(acc[:, :HEAD_DIM] / acc[:, HEAD_DIM:])
    for r in range(rep):
        c = r * N_BRANCH
        out = (gates[:, c:c + 1] * o_cmp[r] + gates[:, c + 1:c + 2] * o_slc[r]
               + gates[:, c + 2:c + 3] * o_win[r])
        o_ref[0, :, r * HEAD_DIM:(r + 1) * HEAD_DIM] = out


def _nsa(p1, p2, kcv, gate_bias, bias, overlap_t, membership, batch, seq):
    tq = NSA_TQ
    n_cmp = seq // CMP_STRIDE
    kv_blk = (1, seq, LANES)
    grp_w = NSA_REP * HEAD_DIM
    return pl.pallas_call(
        _nsa_kernel,
        grid=(batch, NSA_KV_GROUPS, seq // tq),
        in_specs=[pl.BlockSpec((1, tq, grp_w), lambda b, g, i: (b, i, g)),
                  pl.BlockSpec(kv_blk, lambda b, g, i: (b, 0, P2_KS + g)),
                  pl.BlockSpec(kv_blk, lambda b, g, i: (b, 0, P2_VS + g)),
                  pl.BlockSpec(kv_blk, lambda b, g, i: (b, 0, P2_KW + g)),
                  pl.BlockSpec(kv_blk, lambda b, g, i: (b, 0, P2_VW + g)),
                  pl.BlockSpec((1, 1, 1, n_cmp, HEAD_DIM), lambda b, g, i: (b, 0, g, 0, 0)),
                  pl.BlockSpec((1, 1, 1, n_cmp, HEAD_DIM), lambda b, g, i: (b, 1, g, 0, 0)),
                  pl.BlockSpec((1, tq, LANES), lambda b, g, i: (b, i, P1_GL + g)),
                  pl.BlockSpec((1, 1, LANES), lambda b, g, i: (g, 0, 0)),
                  pl.BlockSpec((N_BIAS_TILES, NSA_REP, NSA_TK, NSA_TK), lambda b, g, i: (0, g, 0, 0)),
                  pl.BlockSpec((LANES, n_cmp), lambda b, g, i: (0, 0)),
                  pl.BlockSpec((seq, LANES), lambda b, g, i: (0, 0))],
        out_specs=pl.BlockSpec((1, tq, grp_w), lambda b, g, i: (b, i, g)),
        out_shape=jax.ShapeDtypeStruct((batch, seq, NSA_WIDTH), F32),
        scratch_shapes=[pltpu.VMEM((NSA_REP, tq, LANES), F32),
                        pltpu.VMEM((NSA_REP, tq, 2 * HEAD_DIM), F32),
                        pltpu.VMEM((NSA_REP, tq, SEL_TK), F32),
                        pltpu.VMEM((NSA_REP, tq, SEL_TK), BF16),
                        pltpu.VMEM((NSA_REP, tq, LANES), F32),
                        pltpu.VMEM((NSA_REP, tq, HEAD_DIM), F32),
                        pltpu.VMEM((LANES, tq), F32),
                        pltpu.VMEM((NSA_REP, tq, n_cmp), F32),
                        pltpu.VMEM((NSA_REP, tq, n_cmp), BF16),
                        pltpu.VMEM((3, tq, n_cmp), BF16)],
        compiler_params=_params(("arbitrary", "arbitrary", "arbitrary")),
        name="nsa_attn",
    )(p2, p2, p2, p2, p2, kcv, kcv, p1, gate_bias, bias, overlap_t, membership)


def _out_kernel(x_ref, oa_ref, on_ref, ga_ref, gn_ref, w_ref, o_ref):
    def norm(t, g):
        ms = jnp.mean(t * t, axis=-1, keepdims=True)
        return (t * lax.rsqrt(ms + EPS) * g).astype(BF16)

    ya = norm(oa_ref[...], ga_ref[...])
    yn = norm(on_ref[...], gn_ref[...])
    y = _dot(ya, w_ref[0:A_WIDTH, :]) + _dot(yn, w_ref[A_WIDTH:A_WIDTH + NSA_WIDTH, :])
    o_ref[...] = x_ref[...] + y


def _out_proj(x2, oa, on, gain, w, *, tm):
    n_tok, d = x2.shape
    return pl.pallas_call(
        _out_kernel,
        grid=(n_tok // tm,),
        in_specs=[pl.BlockSpec((tm, d), lambda i: (i, 0)),
                  pl.BlockSpec((tm, A_WIDTH), lambda i: (i, 0)),
                  pl.BlockSpec((tm, NSA_WIDTH), lambda i: (i, 0)),
                  pl.BlockSpec((1, A_WIDTH), lambda i: (0, 0)),
                  pl.BlockSpec((1, NSA_WIDTH), lambda i: (0, 0)),
                  pl.BlockSpec((d, d), lambda i: (0, 0))],
        out_specs=pl.BlockSpec((tm, d), lambda i: (i, 0)),
        out_shape=jax.ShapeDtypeStruct((n_tok, d), F32),
        compiler_params=_params(("arbitrary",)),
        name="out_proj",
    )(x2, oa, on, gain[:A_WIDTH].reshape(1, A_WIDTH), gain[A_WIDTH:].reshape(1, NSA_WIDTH), w)


def _pack_w_in(w_in, gate_bias, q_norm_a, k_norm_a, q_norm_nsa, k_norm_nsa):
    scale = HEAD_DIM ** -0.5
    sizes = (A_WIDTH,) * 3 + (NSA_WIDTH,) + (KV_WIDTH,) * 6 + (N_BRANCH * NSA_HEADS,)
    offs = np.concatenate([[0], np.cumsum(sizes)])
    qa, ka, va, qn, kc, vc, k_s, v_s, k_w, v_w, gl = [w_in[:, offs[i]:offs[i + 1]] for i in range(11)]
    per_grp = NSA_REP * N_BRANCH
    pad = jnp.zeros((w_in.shape[0], LANES - per_grp), w_in.dtype)
    gl_cols = [jnp.concatenate([gl[:, g * per_grp:(g + 1) * per_grp], pad], axis=1) for g in range(NSA_KV_GROUPS)]
    n_pad = P1_HEADS - P1_USED
    w_pad = jnp.zeros((w_in.shape[0], n_pad * LANES), w_in.dtype)
    w = jnp.concatenate([qa, ka, va, kc, vc] + gl_cols + [w_pad, qn, k_s, v_s, k_w, v_w], axis=1).astype(BF16)

    ones = jnp.ones((HEAD_DIM,), F32)
    tile = lambda v, n: jnp.tile(v, n)
    gain = jnp.concatenate([tile(q_norm_a * (scale * LOG2E), A_HEADS), tile(k_norm_a, A_HEADS),
                            tile(ones, A_HEADS + 6 + n_pad),
                            tile(q_norm_nsa * (scale * LOG2E), NSA_HEADS), tile(k_norm_nsa, 2), tile(ones, 2),
                            tile(k_norm_nsa, 2), tile(ones, 2)])
    flag = jnp.concatenate([jnp.ones((2 * A_WIDTH,), F32), jnp.zeros((A_WIDTH + (6 + n_pad) * HEAD_DIM,), F32),
                            jnp.ones((NSA_WIDTH + KV_WIDTH,), F32), jnp.zeros((KV_WIDTH,), F32),
                            jnp.ones((KV_WIDTH,), F32), jnp.zeros((KV_WIDTH,), F32)])
    gb = jnp.stack([jnp.concatenate([gate_bias[g * per_grp:(g + 1) * per_grp],
                                     jnp.zeros((LANES - per_grp,), F32)]) for g in range(NSA_KV_GROUPS)])
    return w, gain, flag, gb.reshape(NSA_KV_GROUPS, 1, LANES)


def _forward(x, ffn1_norm, ffn1_w1, ffn1_w3, ffn1_w2, mix_norm, w_in, gate_bias, q_norm_a, k_norm_a,
             q_norm_nsa, k_norm_nsa, cmp_pos, cmp_k_w1, cmp_k_w2, cmp_v_w1, cmp_v_w2, out_norm, w_out,
             ffn2_norm, ffn2_w1, ffn2_w3, ffn2_w2, rel_bias, *, tm_ffn, tf_ffn, tm_proj, tm_out):
    batch, seq, d = x.shape
    depth = ffn1_w1.shape[0]
    n_tok = batch * seq
    assert seq % DIL_SPAN == 0 and seq // SLC_LEN <= LANES

    bias_a = _expand_bias(rel_bias[:, :A_HEADS], jnp.asarray(_dilated_buckets()), scale=LOG2E)
    bias_a = bias_a.reshape(3, 2, A_HEADS, BAND, 2 * BAND)
    bias_n = _expand_bias(rel_bias[:, A_HEADS:], jnp.asarray(_nsa_buckets()),
                          scale=LOG2E, shift_far=True)
    overlap_t = jnp.asarray(_overlap_t_np(seq // CMP_STRIDE)).astype(BF16)
    membership = jnp.asarray(_block_membership_np(seq)).astype(BF16)

    x2 = x.reshape(n_tok, d)
    for l in range(depth):
        x2 = _ffn(x2, ffn1_norm[l], ffn1_w1[l].astype(BF16), ffn1_w3[l].astype(BF16),
                  ffn1_w2[l].astype(BF16), tm=tm_ffn, tf=tf_ffn)

        w_p, g_p, f_p, gb = _pack_w_in(w_in[l], gate_bias[l], q_norm_a[l], k_norm_a[l],
                                       q_norm_nsa[l], k_norm_nsa[l])
        p1, p2 = _proj(x2, mix_norm[l], w_p, g_p, f_p, P1_HEADS * LANES, tm=tm_proj, tn=512)
        p1 = p1.reshape(batch, seq, P1_HEADS * LANES)
        p2 = p2.reshape(batch, seq, P2_HEADS * LANES)

        o_a = _dilated(p1, bias_a, batch, seq)

        cmp_w1 = jnp.stack([cmp_k_w1[l], cmp_v_w1[l]]).astype(BF16)
        cmp_w2 = jnp.stack([cmp_k_w2[l], cmp_v_w2[l]]).astype(BF16)
        cmp_gain = jnp.stack([k_norm_nsa[l], jnp.ones((HEAD_DIM,), F32)]).reshape(2, 1, HEAD_DIM)
        cmp_flag = jnp.stack([jnp.ones((HEAD_DIM,), F32), jnp.zeros((HEAD_DIM,), F32)]).reshape(2, 1, HEAD_DIM)
        kcv = _compress(p1, cmp_pos[l], cmp_w1, cmp_w2, cmp_gain, cmp_flag, batch, seq)

        o_n = _nsa(p1, p2, kcv, gb, bias_n, overlap_t, membership, batch, seq)

        x2 = _out_proj(x2, o_a.reshape(n_tok, A_WIDTH), o_n.reshape(n_tok, NSA_WIDTH), out_norm[l],
                       w_out[l].astype(BF16), tm=tm_out)

        x2 = _ffn(x2, ffn2_norm[l], ffn2_w1[l].astype(BF16), ffn2_w3[l].astype(BF16),
                  ffn2_w2[l].astype(BF16), tm=tm_ffn, tf=tf_ffn)
    return x2.reshape(batch, seq, d)


def kernel(x, ffn1_norm, ffn1_w1, ffn1_w3, ffn1_w2, mix_norm, w_in, gate_bias, q_norm_a, k_norm_a, q_norm_nsa, k_norm_nsa, cmp_pos, cmp_k_w1, cmp_k_w2, cmp_v_w1, cmp_v_w2, out_norm, w_out, ffn2_norm, ffn2_w1, ffn2_w3, ffn2_w2, rel_bias):
    return _forward(x, ffn1_norm, ffn1_w1, ffn1_w3, ffn1_w2, mix_norm, w_in, gate_bias, q_norm_a, k_norm_a,
                    q_norm_nsa, k_norm_nsa, cmp_pos, cmp_k_w1, cmp_k_w2, cmp_v_w1, cmp_v_w2, out_norm, w_out,
                    ffn2_norm, ffn2_w1, ffn2_w3, ffn2_w2, rel_bias,
                    tm_ffn=512, tf_ffn=512, tm_proj=1024, tm_out=256)
```

```python
import functools
import math

import numpy as np
import jax
import jax.numpy as jnp
from jax import lax
from jax.experimental import pallas as pl
from jax.experimental.pallas import tpu as pltpu

D_MODEL = 2048
HEAD_DIM = 128
N_HEADS = 16
A_HEADS = 8
NSA_HEADS = 8
NSA_KV_GROUPS = 2
NSA_REP = 4
A_WIDTH = A_HEADS * HEAD_DIM
NSA_WIDTH = NSA_HEADS * HEAD_DIM
KV_WIDTH = NSA_KV_GROUPS * HEAD_DIM
N_BRANCH = 3
D_FF = 5632
DILATED_PATTERNS = ((128, 1), (512, 4), (2048, 16))
BAND = 128
CMP_STRIDE = 16
CMP_LEN = 32
CMP_HIDDEN = 512
SLC_LEN = 64
SLC_TOPK = 16
WIN_LEN = 512
N_BUCKETS = 32
MAX_DISTANCE = 2048
FORCE_SCORE = 1e6
NEG_INF = -1e30
EPS = 1e-6
LOG2E = 1.4426950408889634

LANES = 128
DIL_SPAN = 2048
NSA_TQ = 512
NSA_TK = 128
SEL_TK = 512
CMP_ROWS = 32
BIAS_FAR = 13
BIAS_FUTURE = 14
N_BIAS_TILES = 15
MASK_BIG = 1e30
VMEM_LIMIT = 56 * 1024 * 1024

F32 = jnp.float32
BF16 = jnp.bfloat16

P1_QA, P1_KA, P1_VA, P1_KC, P1_VC, P1_GL = 0, 8, 16, 24, 26, 28
P1_USED = 30
P1_HEADS = 32
P2_QN, P2_KS, P2_VS, P2_KW, P2_VW = 0, 8, 10, 12, 14
P2_HEADS = 16


def _params(sem, flags=None):
    return pltpu.CompilerParams(dimension_semantics=sem, vmem_limit_bytes=VMEM_LIMIT, flags=flags)


def _nt_dot(a, b):
    return lax.dot_general(a, b, (((1,), (1,)), ((), ())), preferred_element_type=F32)


def _dot(a, b):
    return jnp.dot(a, b, preferred_element_type=F32)


def _rel_bucket_np(dist):
    n = np.maximum(dist, 0)
    max_exact = N_BUCKETS // 2
    nf = np.maximum(n, 1).astype(np.float32)
    log_b = max_exact + (np.log(nf / np.float32(max_exact)) / np.float32(math.log(MAX_DISTANCE / max_exact))
                         * np.float32(N_BUCKETS - max_exact)).astype(np.int32)
    return np.where(n < max_exact, n, np.minimum(log_b, N_BUCKETS - 1)).astype(np.int32)


def _dilated_buckets():
    i = np.arange(BAND)[:, None]
    j = np.arange(2 * BAND)[None, :] - BAND
    diff = i - j
    out = []
    for window, dilation in DILATED_PATTERNS:
        mask = (diff >= 0) & (diff <= window // dilation)
        bkt = _rel_bucket_np(diff * dilation)
        out.append(np.where(mask, bkt, -1))
        out.append(np.where(mask & (j >= 0), bkt, -1))
    return np.stack(out).astype(np.int32)


def _nsa_buckets():
    i = np.arange(NSA_TK)[:, None]
    j = np.arange(NSA_TK)[None, :]
    tiles = []
    for m in range(BIAS_FAR + 1):
        dist = NSA_TK * m + i - j
        tiles.append(np.where(dist >= 0, _rel_bucket_np(dist), -1))
    assert (tiles[BIAS_FAR] == N_BUCKETS - 1).all()
    tiles.append(np.full((NSA_TK, NSA_TK), -1))
    assert len(tiles) == N_BIAS_TILES and BIAS_FUTURE == N_BIAS_TILES - 1
    return np.stack(tiles).astype(np.int32)


def _overlap_t_np(n_cmp_pad):
    c_start = np.arange(n_cmp_pad)[None, :] * CMP_STRIDE
    s_start = np.arange(LANES)[:, None] * SLC_LEN
    return ((c_start < s_start + SLC_LEN) & (c_start + CMP_LEN > s_start)).astype(np.float32)


def _block_membership_np(seq):
    k_blk = np.arange(seq)[:, None] // SLC_LEN
    return np.where(k_blk == np.arange(LANES)[None, :], MASK_BIG, 0.0).astype(np.float32)


def _bias_kernel(tbl_ref, bkt_ref, o_ref, *, n_heads, scale, shift_far):
    bkt = bkt_ref[0]
    for h in range(n_heads):
        acc = jnp.full(bkt.shape, NEG_INF, F32)
        base = tbl_ref[N_BUCKETS - 1, h] if shift_far else 0.0
        for b in range(N_BUCKETS):
            acc = jnp.where(bkt == b, (tbl_ref[b, h] - base) * scale, acc)
        o_ref[0, h] = acc


def _expand_bias(table, buckets, scale=1.0, shift_far=False):
    n_t, rows, width = buckets.shape
    nh = table.shape[1]
    return pl.pallas_call(
        functools.partial(_bias_kernel, n_heads=nh, scale=scale, shift_far=shift_far),
        grid=(n_t,),
        in_specs=[pl.BlockSpec(memory_space=pltpu.SMEM),
                  pl.BlockSpec((1, rows, width), lambda t: (t, 0, 0))],
        out_specs=pl.BlockSpec((1, nh, rows, width), lambda t: (t, 0, 0, 0)),
        out_shape=jax.ShapeDtypeStruct((n_t, nh, rows, width), F32),
        compiler_params=_params(("arbitrary",)),
        name="bias_expand",
    )(table, buckets)


def _ffn_kernel(x_ref, g_ref, w1_ref, w3_ref, w2_ref, o_ref, n_scr):
    j = pl.program_id(1)

    @pl.when(j == 0)
    def _():
        x = x_ref[...]
        ms = jnp.mean(x * x, axis=-1, keepdims=True)
        n_scr[...] = (x * lax.rsqrt(ms + EPS) * g_ref[...]).astype(BF16)
        o_ref[...] = jnp.zeros_like(o_ref)

    n = n_scr[...]
    h1 = _dot(n, w1_ref[...])
    h3 = _dot(n, w3_ref[...])
    a = (h1 * jax.nn.sigmoid(h1) * h3).astype(BF16)
    o_ref[...] += _dot(a, w2_ref[...])

    @pl.when(j == pl.num_programs(1) - 1)
    def _():
        o_ref[...] = x_ref[...] + 0.5 * o_ref[...]


def _ffn(x2, gain, w1, w3, w2, *, tm, tf):
    n_tok, d = x2.shape
    f = w1.shape[1]
    return pl.pallas_call(
        _ffn_kernel,
        grid=(n_tok // tm, f // tf),
        in_specs=[pl.BlockSpec((tm, d), lambda i, j: (i, 0)),
                  pl.BlockSpec((1, d), lambda i, j: (0, 0)),
                  pl.BlockSpec((d, tf), lambda i, j: (0, j)),
                  pl.BlockSpec((d, tf), lambda i, j: (0, j)),
                  pl.BlockSpec((tf, d), lambda i, j: (j, 0))],
        out_specs=pl.BlockSpec((tm, d), lambda i, j: (i, 0)),
        out_shape=jax.ShapeDtypeStruct((n_tok, d), F32),
        scratch_shapes=[pltpu.VMEM((tm, d), BF16)],
        compiler_params=_params(("arbitrary", "arbitrary")),
        name="ffn",
    )(x2, gain.reshape(1, d), w1, w3, w2)


def _proj_kernel(x_ref, g_ref, w_ref, cg_ref, cf_ref, o1_ref, o2_ref, n_scr, acc_scr, *, n1):
    j = pl.program_id(1)
    last = pl.num_programs(1) - 1

    def matmul():
        acc_scr[...] = _dot(n_scr[...], w_ref[...])

    def epilogue(o_ref):
        for h in range(acc_scr.shape[1] // LANES):
            cols = slice(h * LANES, (h + 1) * LANES)
            a = acc_scr[:, cols]
            ms = jnp.mean(a * a, axis=-1, keepdims=True)
            inv = jnp.where(cf_ref[:, cols] > 0.0, lax.rsqrt(ms + EPS), 1.0)
            o_ref[:, cols] = (a * inv * cg_ref[:, cols]).astype(o_ref.dtype)

    @pl.when(j == 0)
    def _():
        x = x_ref[...]
        ms = jnp.mean(x * x, axis=-1, keepdims=True)
        n_scr[...] = (x * lax.rsqrt(ms + EPS) * g_ref[...]).astype(BF16)
        matmul()

    @pl.when(jnp.logical_and(j > 0, j <= n1))
    def _():
        epilogue(o1_ref)
        matmul()

    @pl.when(jnp.logical_and(j > n1, j < last))
    def _():
        epilogue(o2_ref)
        matmul()

    @pl.when(j == last)
    def _():
        epilogue(o2_ref)


def _proj(x2, gain, w, col_gain, col_flag, cols1, *, tm, tn):
    n_tok, d = x2.shape
    n_cols = w.shape[1]
    n1 = cols1 // tn
    n2 = (n_cols - cols1) // tn
    assert n1 * tn == cols1 and (n1 + n2) * tn == n_cols
    n_j = n1 + n2
    prev = lambda j: jnp.maximum(j - 1, 0)
    return pl.pallas_call(
        functools.partial(_proj_kernel, n1=n1),
        grid=(n_tok // tm, n_j + 1),
        in_specs=[pl.BlockSpec((tm, d), lambda i, j: (i, 0)),
                  pl.BlockSpec((1, d), lambda i, j: (0, 0)),
                  pl.BlockSpec((d, tn), lambda i, j: (0, jnp.minimum(j, n_j - 1))),
                  pl.BlockSpec((1, tn), lambda i, j: (0, prev(j))),
                  pl.BlockSpec((1, tn), lambda i, j: (0, prev(j)))],
        out_specs=[pl.BlockSpec((tm, tn), lambda i, j: (i, jnp.minimum(prev(j), n1 - 1))),
                   pl.BlockSpec((tm, tn), lambda i, j: (i, jnp.maximum(prev(j) - n1, 0)))],
        out_shape=[jax.ShapeDtypeStruct((n_tok, cols1), F32),
                   jax.ShapeDtypeStruct((n_tok, n_cols - cols1), BF16)],
        scratch_shapes=[pltpu.VMEM((tm, d), BF16), pltpu.VMEM((tm, tn), F32)],
        compiler_params=_params(("arbitrary", "arbitrary")),
        name="in_proj",
    )(x2, gain.reshape(1, d), w, col_gain.reshape(1, n_cols), col_flag.reshape(1, n_cols))


def _dilated_kernel(q_ref, kc_ref, kp_ref, vc_ref, vp_ref, bias_ref, o_ref,
                    k_scr, v_scr, out_scr, lse_scr):
    n = pl.program_id(2)
    k_scr[0:DIL_SPAN, :] = kp_ref[0]
    k_scr[DIL_SPAN:2 * DIL_SPAN, :] = kc_ref[0]
    v_scr[0:DIL_SPAN, :] = vp_ref[0]
    v_scr[DIL_SPAN:2 * DIL_SPAN, :] = vc_ref[0]

    for p, (_, dil) in enumerate(DILATED_PATTERNS):
        span = BAND * dil

        def unit(u, carry, p=p, dil=dil, span=span):
            s = u // dil
            r = u - s * dil
            base = s * span + r
            first = jnp.logical_and(n == 0, s == 0).astype(jnp.int32)
            q = q_ref[0, pl.ds(base, BAND, stride=dil), :].astype(BF16)
            kk = k_scr[pl.ds(DIL_SPAN + base - span, 2 * BAND, stride=dil), :].astype(BF16)
            vv = v_scr[pl.ds(DIL_SPAN + base - span, 2 * BAND, stride=dil), :].astype(BF16)
            lg = _nt_dot(q, kk) + bias_ref[p, first, 0]
            m = jnp.max(lg, axis=-1, keepdims=True)
            e = jnp.exp2(lg - m).astype(BF16)
            acc = _dot(e, jnp.concatenate([vv, jnp.ones((2 * BAND, LANES), BF16)], axis=1))
            ssum = acc[:, LANES:]
            out_scr[p, pl.ds(base, BAND, stride=dil), :] = acc[:, :LANES] / ssum
            lse_scr[p, pl.ds(base, BAND, stride=dil), :] = m + jnp.log2(ssum)
            return carry

        lax.fori_loop(0, DIL_SPAN // BAND, unit, 0, unroll=8)

    chunk = 256

    def combine(c, carry):
        rows = pl.ds(pl.multiple_of(c * chunk, chunk), chunk)
        l0, l1, l2 = lse_scr[0, rows, :], lse_scr[1, rows, :], lse_scr[2, rows, :]
        big = jnp.maximum(jnp.maximum(l0, l1), l2)
        e0, e1, e2 = jnp.exp2(l0 - big), jnp.exp2(l1 - big), jnp.exp2(l2 - big)
        num = e0 * out_scr[0, rows, :] + e1 * out_scr[1, rows, :] + e2 * out_scr[2, rows, :]
        o_ref[0, rows, :] = num / (e0 + e1 + e2)
        return carry

    lax.fori_loop(0, DIL_SPAN // chunk, combine, 0)


def _dilated(p1, bias, batch, seq):
    blk = (1, DIL_SPAN, LANES)
    prev = lambda nn: jnp.maximum(nn - 1, 0)
    return pl.pallas_call(
        _dilated_kernel,
        grid=(batch, A_HEADS, seq // DIL_SPAN),
        in_specs=[pl.BlockSpec(blk, lambda b, h, nn: (b, nn, P1_QA + h)),
                  pl.BlockSpec(blk, lambda b, h, nn: (b, nn, P1_KA + h)),
                  pl.BlockSpec(blk, lambda b, h, nn: (b, prev(nn), P1_KA + h)),
                  pl.BlockSpec(blk, lambda b, h, nn: (b, nn, P1_VA + h)),
                  pl.BlockSpec(blk, lambda b, h, nn: (b, prev(nn), P1_VA + h)),
                  pl.BlockSpec((3, 2, 1, BAND, 2 * BAND), lambda b, h, nn: (0, 0, h, 0, 0))],
        out_specs=pl.BlockSpec(blk, lambda b, h, nn: (b, nn, h)),
        out_shape=jax.ShapeDtypeStruct((batch, seq, A_WIDTH), F32),
        scratch_shapes=[pltpu.VMEM((2 * DIL_SPAN, LANES), F32),
                        pltpu.VMEM((2 * DIL_SPAN, LANES), F32),
                        pltpu.VMEM((3, DIL_SPAN, LANES), F32),
                        pltpu.VMEM((3, DIL_SPAN, LANES), F32)],
        compiler_params=_params(("arbitrary", "arbitrary", "arbitrary")),
        name="dilated_attn",
    )(p1, p1, p1, p1, p1, bias)


def _compress_kernel(t_ref, pos_ref, w1_ref, w2_ref, gain_ref, flag_ref, o_ref, lo_scr, hi_scr):
    n_blk = o_ref.shape[3]
    lo_scr[...] = jnp.zeros_like(lo_scr)
    hi_scr[...] = jnp.zeros_like(hi_scr)
    for i in range(CMP_STRIDE):
        rows = t_ref[0, pl.ds(i, n_blk, stride=CMP_STRIDE), :]
        a = (rows + pos_ref[i:i + 1, :]).astype(BF16)
        lo_scr[...] += _dot(a, w1_ref[0, i * HEAD_DIM:(i + 1) * HEAD_DIM, :])
        b = (rows + pos_ref[CMP_STRIDE + i:CMP_STRIDE + i + 1, :]).astype(BF16)
        hi_scr[...] += _dot(b, w1_ref[0, (CMP_STRIDE + i) * HEAD_DIM:(CMP_STRIDE + i + 1) * HEAD_DIM, :])
    hid = lo_scr[...] + pltpu.roll(hi_scr[...], n_blk - 1, 0)
    o = _dot(jax.nn.gelu(hid).astype(BF16), w2_ref[0])
    ms = jnp.mean(o * o, axis=-1, keepdims=True)
    inv = jnp.where(flag_ref[0] > 0.0, lax.rsqrt(ms + EPS), 1.0)
    o_ref[0, 0, 0] = o * inv * gain_ref[0]


def _compress(p1, pos, w1, w2, gain, flag, batch, seq):
    n_blk = seq // CMP_STRIDE
    return pl.pallas_call(
        _compress_kernel,
        grid=(batch, 2, NSA_KV_GROUPS),
        in_specs=[pl.BlockSpec((1, seq, LANES), lambda b, kv, g: (b, 0, P1_KC + 2 * kv + g)),
                  pl.BlockSpec((CMP_LEN, HEAD_DIM), lambda b, kv, g: (0, 0)),
                  pl.BlockSpec((1, CMP_LEN * HEAD_DIM, CMP_HIDDEN), lambda b, kv, g: (kv, 0, 0)),
                  pl.BlockSpec((1, CMP_HIDDEN, HEAD_DIM), lambda b, kv, g: (kv, 0, 0)),
                  pl.BlockSpec((1, 1, HEAD_DIM), lambda b, kv, g: (kv, 0, 0)),
                  pl.BlockSpec((1, 1, HEAD_DIM), lambda b, kv, g: (kv, 0, 0))],
        out_specs=pl.BlockSpec((1, 1, 1, n_blk, HEAD_DIM), lambda b, kv, g: (b, kv, g, 0, 0)),
        out_shape=jax.ShapeDtypeStruct((batch, 2, NSA_KV_GROUPS, n_blk, HEAD_DIM), F32),
        scratch_shapes=[pltpu.VMEM((n_blk, CMP_HIDDEN), F32), pltpu.VMEM((n_blk, CMP_HIDDEN), F32)],
        compiler_params=_params(("arbitrary", "arbitrary", "arbitrary")),
        name="nsa_compress",
    )(p1, pos, w1, w2, gain, flag)


def _nsa_kernel(q_ref, ks_ref, vs_ref, kw_ref, vw_ref, kc_ref, vc_ref, gl_ref, gb_ref,
                bias_ref, ovl_ref, et_ref, o_ref, m_scr, acc_scr, s_scr, p_scr, a_scr, ocmp_scr, imp_scr,
                lc_scr, pcb_scr, split_scr):
    qi = pl.program_id(2)
    tq, rep = NSA_TQ, NSA_REP
    n_cmp = kc_ref.shape[3]

    q2 = q_ref[0]
    q_heads = [q2[:, r * HEAD_DIM:(r + 1) * HEAD_DIM] for r in range(rep)]
    t_col = qi * tq + lax.broadcasted_iota(jnp.int32, (tq, 1), 0)

    def compressed(width):
        cmp_end = lax.broadcasted_iota(jnp.int32, (1, width), 1) * CMP_STRIDE + (CMP_LEN - 1)
        kcb = kc_ref[0, 0, 0, 0:width, :].astype(BF16)
        vcb = vc_ref[0, 0, 0, 0:width, :].astype(BF16)
        for r in range(rep):
            lc_scr[r, :, 0:width] = _nt_dot(q_heads[r], kcb)
        for c in range(tq // CMP_ROWS):
            rows = slice(c * CMP_ROWS, (c + 1) * CMP_ROWS)
            cmask = cmp_end <= t_col[rows]
            pc_sum = jnp.zeros((CMP_ROWS, width), F32)
            for r in range(rep):
                lc = jnp.where(cmask, lc_scr[r, rows, 0:width], NEG_INF)
                mc = jnp.max(lc, axis=-1, keepdims=True)
                pc = jnp.where(cmask, jnp.exp2(lc - mc), 0.0)
                pc = pc / jnp.maximum(jnp.sum(pc, axis=-1, keepdims=True), 1e-30)
                pcb_scr[r, rows, 0:width] = pc.astype(BF16)
                pc_sum = pc_sum + pc
            hi = pc_sum.astype(BF16)
            rem = pc_sum - hi.astype(F32)
            mid = rem.astype(BF16)
            split_scr[0, rows, 0:width] = hi
            split_scr[1, rows, 0:width] = mid
            split_scr[2, rows, 0:width] = (rem - mid.astype(F32)).astype(BF16)
        for r in range(rep):
            ocmp_scr[r] = _dot(pcb_scr[r, :, 0:width], vcb)
        ovl_t = ovl_ref[:, 0:width]
        imp_scr[...] = (_nt_dot(ovl_t, split_scr[0, :, 0:width]) + _nt_dot(ovl_t, split_scr[1, :, 0:width])
                        + _nt_dot(ovl_t, split_scr[2, :, 0:width]))

    n_grp = n_cmp // LANES
    need = jnp.minimum(((qi + 1) * tq - CMP_LEN) // (CMP_STRIDE * LANES) + 1, n_grp)
    for k in range(1, n_grp + 1):
        pl.when(need == k)(functools.partial(compressed, k * LANES))
    o_cmp = [ocmp_scr[r] for r in range(rep)]
    imp_t = imp_scr[...]

    n_rb = tq // NSA_TK
    n_wt = WIN_LEN // NSA_TK + n_rb
    w_keys = n_wt * NSA_TK
    j0 = jnp.maximum(qi * n_rb - WIN_LEN // NSA_TK, 0)
    w_rows = pl.ds(pl.multiple_of(j0 * NSA_TK, NSA_TK), w_keys)
    kw_tile = kw_ref[0, w_rows, :]
    vw_aug = jnp.concatenate([vw_ref[0, w_rows, :], jnp.ones((w_keys, HEAD_DIM), BF16)], axis=1)
    kpos = j0 * NSA_TK + lax.broadcasted_iota(jnp.int32, (1, w_keys), 1)
    wmasks = []
    for a in range(n_rb):
        dist = t_col[a * NSA_TK:(a + 1) * NSA_TK] - kpos
        wmasks.append(jnp.where(dist >= 0, dist, WIN_LEN) < WIN_LEN)
    o_win = []
    for r in range(rep):
        s = _nt_dot(q_heads[r], kw_tile)
        p = []
        for a in range(n_rb):
            rows = slice(a * NSA_TK, (a + 1) * NSA_TK)
            tiles = [jnp.clip(qi * n_rb + a - j0 - i, 0, BIAS_FAR) for i in range(n_wt)]
            sa = s[rows] + jnp.concatenate([bias_ref[t, r] for t in tiles], axis=-1)
            sa = jnp.where(wmasks[a], sa, NEG_INF)
            p.append(jnp.exp2(sa - jnp.max(sa, axis=-1, keepdims=True)).astype(BF16))
        acc = _dot(jnp.concatenate(p, axis=0), vw_aug)
        o_win.append(acc[:, :HEAD_DIM] / acc[:, HEAD_DIM:])

    blk = lax.broadcasted_iota(jnp.int32, (LANES, 1), 0)
    blk_f = blk.astype(F32)
    t_row = qi * tq + lax.broadcasted_iota(jnp.int32, (1, tq), 1)
    cur = t_row >> 6
    valid = blk * SLC_LEN <= t_row
    forced = (blk == 0) | (blk == cur) | (blk == cur - 1)
    score = jnp.where(valid, jnp.where(forced, FORCE_SCORE, imp_t), -1.0)
    sel_t = jnp.zeros((LANES, tq), F32)
    for _ in range(SLC_TOPK):
        best = jnp.max(score, axis=0, keepdims=True)
        idx = jnp.min(jnp.where(score == best, blk_f, float(LANES)), axis=0, keepdims=True)
        hit = blk_f == idx
        sel_t = jnp.where(hit, 1.0, sel_t)
        score = jnp.where(hit, -3e38, score)
    unsel = (sel_t.T - 1.0).astype(BF16)
    lhs = [jnp.concatenate([q_heads[r], unsel], axis=1) for r in range(rep)]

    ones_v = jnp.ones((SEL_TK, HEAD_DIM), BF16)

    sub = SEL_TK // NSA_TK
    n_tiles = ((qi + 1) * n_rb - 1) // sub + 1

    def key_rows(jj):
        return pl.ds(pl.multiple_of(jj * SEL_TK, SEL_TK), SEL_TK)

    def logits_stage(jj, heads):
        rows = key_rows(jj)
        rhs = jnp.concatenate([ks_ref[0, rows, :], et_ref[rows, :]], axis=1)
        for r in heads:
            s_scr[r] = _nt_dot(lhs[r], rhs)

    def softmax_stage(jj, heads):
        for a in range(n_rb):
            rows = slice(a * NSA_TK, (a + 1) * NSA_TK)
            u = qi * n_rb + a - sub * jj
            tiles = [jnp.where(u < i, BIAS_FUTURE, jnp.minimum(u - i, BIAS_FAR)) for i in range(sub)]
            for r in heads:
                s = s_scr[r, rows, :] + jnp.concatenate([bias_ref[t, r] for t in tiles], axis=-1)
                m_old = m_scr[r, rows, :]
                m_new = jnp.maximum(m_old, jnp.max(s, axis=-1, keepdims=True))
                a_scr[r, rows, :] = jnp.exp2(m_old - m_new)
                p_scr[r, rows, :] = jnp.exp2(s - jnp.concatenate([m_new] * sub, axis=1)).astype(BF16)
                m_scr[r, rows, :] = m_new

    def pv_stage(jj, heads):
        v_aug = jnp.concatenate([vs_ref[0, key_rows(jj), :], ones_v], axis=1)
        for r in heads:
            alpha = a_scr[r]
            acc_scr[r] = jnp.concatenate([alpha, alpha], axis=1) * acc_scr[r] + _dot(p_scr[r], v_aug)

    all_heads = range(rep)
    m_scr[...] = jnp.full(m_scr.shape, NEG_INF, F32)
    acc_scr[...] = jnp.zeros_like(acc_scr)
    p_scr[...] = jnp.zeros_like(p_scr)
    a_scr[...] = jnp.ones_like(a_scr)
    logits_stage(0, all_heads)

    def sel_step(it, carry):
        pv_stage(jnp.maximum(it - 1, 0), all_heads)
        softmax_stage(it, all_heads)
        logits_stage(jnp.minimum(it + 1, n_tiles - 1), all_heads)
        return carry

    lax.fori_loop(0, n_tiles, sel_step, 0)
    pv_stage(n_tiles - 1, all_heads)

    gates = jax.nn.sigmoid(gl_ref[0] + gb_ref[0])
    o_slc = []
    for r in range(rep):
        acc = acc_scr[r]
        o_slc.append(acc[:, :HEAD_DIM] / acc[:, HEAD_DIM:])
    for r in range(rep):
        c = r * N_BRANCH
        out = (gates[:, c:c + 1] * o_cmp[r] + gates[:, c + 1:c + 2] * o_slc[r]
               + gates[:, c + 2:c + 3] * o_win[r])
        o_ref[0, :, r * HEAD_DIM:(r + 1) * HEAD_DIM] = out


def _nsa(p1, p2, kcv, gate_bias, bias, overlap_t, membership, batch, seq):
    tq = NSA_TQ
    n_cmp = seq // CMP_STRIDE
    kv_blk = (1, seq, LANES)
    grp_w = NSA_REP * HEAD_DIM
    once = pl.Buffered(1)
    return pl.pallas_call(
        _nsa_kernel,
        grid=(batch, NSA_KV_GROUPS, seq // tq),
        in_specs=[pl.BlockSpec((1, tq, grp_w), lambda b, g, i: (b, i, g)),
                  pl.BlockSpec(kv_blk, lambda b, g, i: (b, 0, P2_KS + g), pipeline_mode=once),
                  pl.BlockSpec(kv_blk, lambda b, g, i: (b, 0, P2_VS + g), pipeline_mode=once),
                  pl.BlockSpec(kv_blk, lambda b, g, i: (b, 0, P2_KW + g), pipeline_mode=once),
                  pl.BlockSpec(kv_blk, lambda b, g, i: (b, 0, P2_VW + g), pipeline_mode=once),
                  pl.BlockSpec((1, 1, 1, n_cmp, HEAD_DIM), lambda b, g, i: (b, 0, g, 0, 0)),
                  pl.BlockSpec((1, 1, 1, n_cmp, HEAD_DIM), lambda b, g, i: (b, 1, g, 0, 0)),
                  pl.BlockSpec((1, tq, LANES), lambda b, g, i: (b, i, P1_GL + g)),
                  pl.BlockSpec((1, 1, LANES), lambda b, g, i: (g, 0, 0)),
                  pl.BlockSpec((N_BIAS_TILES, NSA_REP, NSA_TK, NSA_TK), lambda b, g, i: (0, g, 0, 0),
                               pipeline_mode=once),
                  pl.BlockSpec((LANES, n_cmp), lambda b, g, i: (0, 0)),
                  pl.BlockSpec((seq, LANES), lambda b, g, i: (0, 0))],
        out_specs=pl.BlockSpec((1, tq, grp_w), lambda b, g, i: (b, i, g)),
        out_shape=jax.ShapeDtypeStruct((batch, seq, NSA_WIDTH), F32),
        scratch_shapes=[pltpu.VMEM((NSA_REP, tq, LANES), F32),
                        pltpu.VMEM((NSA_REP, tq, 2 * HEAD_DIM), F32),
                        pltpu.VMEM((NSA_REP, tq, SEL_TK), F32),
                        pltpu.VMEM((NSA_REP, tq, SEL_TK), BF16),
                        pltpu.VMEM((NSA_REP, tq, LANES), F32),
                        pltpu.VMEM((NSA_REP, tq, HEAD_DIM), F32),
                        pltpu.VMEM((LANES, tq), F32),
                        pltpu.VMEM((NSA_REP, tq, n_cmp), F32),
                        pltpu.VMEM((NSA_REP, tq, n_cmp), BF16),
                        pltpu.VMEM((3, tq, n_cmp), BF16)],
        compiler_params=_params(("arbitrary", "arbitrary", "arbitrary")),
        name="nsa_attn",
    )(p2, p2, p2, p2, p2, kcv, kcv, p1, gate_bias, bias, overlap_t, membership)


def _out_kernel(x_ref, oa_ref, on_ref, ga_ref, gn_ref, w_ref, o_ref):
    def norm(t, g):
        ms = jnp.mean(t * t, axis=-1, keepdims=True)
        return (t * lax.rsqrt(ms + EPS) * g).astype(BF16)

    ya = norm(oa_ref[...], ga_ref[...])
    yn = norm(on_ref[...], gn_ref[...])
    y = _dot(ya, w_ref[0:A_WIDTH, :]) + _dot(yn, w_ref[A_WIDTH:A_WIDTH + NSA_WIDTH, :])
    o_ref[...] = x_ref[...] + y


def _out_proj(x2, oa, on, gain, w, *, tm):
    n_tok, d = x2.shape
    return pl.pallas_call(
        _out_kernel,
        grid=(n_tok // tm,),
        in_specs=[pl.BlockSpec((tm, d), lambda i: (i, 0)),
                  pl.BlockSpec((tm, A_WIDTH), lambda i: (i, 0)),
                  pl.BlockSpec((tm, NSA_WIDTH), lambda i: (i, 0)),
                  pl.BlockSpec((1, A_WIDTH), lambda i: (0, 0)),
                  pl.BlockSpec((1, NSA_WIDTH), lambda i: (0, 0)),
                  pl.BlockSpec((d, d), lambda i: (0, 0))],
        out_specs=pl.BlockSpec((tm, d), lambda i: (i, 0)),
        out_shape=jax.ShapeDtypeStruct((n_tok, d), F32),
        compiler_params=_params(("arbitrary",)),
        name="out_proj",
    )(x2, oa, on, gain[:A_WIDTH].reshape(1, A_WIDTH), gain[A_WIDTH:].reshape(1, NSA_WIDTH), w)


def _pack_w_in(w_in, gate_bias, q_norm_a, k_norm_a, q_norm_nsa, k_norm_nsa):
    scale = HEAD_DIM ** -0.5
    sizes = (A_WIDTH,) * 3 + (NSA_WIDTH,) + (KV_WIDTH,) * 6 + (N_BRANCH * NSA_HEADS,)
    offs = np.concatenate([[0], np.cumsum(sizes)])
    qa, ka, va, qn, kc, vc, k_s, v_s, k_w, v_w, gl = [w_in[:, offs[i]:offs[i + 1]] for i in range(11)]
    per_grp = NSA_REP * N_BRANCH
    pad = jnp.zeros((w_in.shape[0], LANES - per_grp), w_in.dtype)
    gl_cols = [jnp.concatenate([gl[:, g * per_grp:(g + 1) * per_grp], pad], axis=1) for g in range(NSA_KV_GROUPS)]
    n_pad = P1_HEADS - P1_USED
    w_pad = jnp.zeros((w_in.shape[0], n_pad * LANES), w_in.dtype)
    w = jnp.concatenate([qa, ka, va, kc, vc] + gl_cols + [w_pad, qn, k_s, v_s, k_w, v_w], axis=1).astype(BF16)

    ones = jnp.ones((HEAD_DIM,), F32)
    tile = lambda v, n: jnp.tile(v, n)
    gain = jnp.concatenate([tile(q_norm_a * (scale * LOG2E), A_HEADS), tile(k_norm_a, A_HEADS),
                            tile(ones, A_HEADS + 6 + n_pad),
                            tile(q_norm_nsa * (scale * LOG2E), NSA_HEADS), tile(k_norm_nsa, 2), tile(ones, 2),
                            tile(k_norm_nsa, 2), tile(ones, 2)])
    flag = jnp.concatenate([jnp.ones((2 * A_WIDTH,), F32), jnp.zeros((A_WIDTH + (6 + n_pad) * HEAD_DIM,), F32),
                            jnp.ones((NSA_WIDTH + KV_WIDTH,), F32), jnp.zeros((KV_WIDTH,), F32),
                            jnp.ones((KV_WIDTH,), F32), jnp.zeros((KV_WIDTH,), F32)])
    gb = jnp.stack([jnp.concatenate([gate_bias[g * per_grp:(g + 1) * per_grp],
                                     jnp.zeros((LANES - per_grp,), F32)]) for g in range(NSA_KV_GROUPS)])
    return w, gain, flag, gb.reshape(NSA_KV_GROUPS, 1, LANES)


def _forward(x, ffn1_norm, ffn1_w1, ffn1_w3, ffn1_w2, mix_norm, w_in, gate_bias, q_norm_a, k_norm_a,
             q_norm_nsa, k_norm_nsa, cmp_pos, cmp_k_w1, cmp_k_w2, cmp_v_w1, cmp_v_w2, out_norm, w_out,
             ffn2_norm, ffn2_w1, ffn2_w3, ffn2_w2, rel_bias, *, tm_ffn, tf_ffn, tm_proj, tm_out):
    batch, seq, d = x.shape
    depth = ffn1_w1.shape[0]
    n_tok = batch * seq
    assert seq % DIL_SPAN == 0 and seq // SLC_LEN <= LANES

    bias_a = _expand_bias(rel_bias[:, :A_HEADS], jnp.asarray(_dilated_buckets()), scale=LOG2E)
    bias_a = bias_a.reshape(3, 2, A_HEADS, BAND, 2 * BAND)
    bias_n = _expand_bias(rel_bias[:, A_HEADS:], jnp.asarray(_nsa_buckets()),
                          scale=LOG2E, shift_far=True)
    overlap_t = jnp.asarray(_overlap_t_np(seq // CMP_STRIDE)).astype(BF16)
    membership = jnp.asarray(_block_membership_np(seq)).astype(BF16)

    x2 = x.reshape(n_tok, d)
    for l in range(depth):
        x2 = _ffn(x2, ffn1_norm[l], ffn1_w1[l].astype(BF16), ffn1_w3[l].astype(BF16),
                  ffn1_w2[l].astype(BF16), tm=tm_ffn, tf=tf_ffn)

        w_p, g_p, f_p, gb = _pack_w_in(w_in[l], gate_bias[l], q_norm_a[l], k_norm_a[l],
                                       q_norm_nsa[l], k_norm_nsa[l])
        p1, p2 = _proj(x2, mix_norm[l], w_p, g_p, f_p, P1_HEADS * LANES, tm=tm_proj, tn=512)
        p1 = p1.reshape(batch, seq, P1_HEADS * LANES)
        p2 = p2.reshape(batch, seq, P2_HEADS * LANES)

        o_a = _dilated(p1, bias_a, batch, seq)

        cmp_w1 = jnp.stack([cmp_k_w1[l], cmp_v_w1[l]]).astype(BF16)
        cmp_w2 = jnp.stack([cmp_k_w2[l], cmp_v_w2[l]]).astype(BF16)
        cmp_gain = jnp.stack([k_norm_nsa[l], jnp.ones((HEAD_DIM,), F32)]).reshape(2, 1, HEAD_DIM)
        cmp_flag = jnp.stack([jnp.ones((HEAD_DIM,), F32), jnp.zeros((HEAD_DIM,), F32)]).reshape(2, 1, HEAD_DIM)
        kcv = _compress(p1, cmp_pos[l], cmp_w1, cmp_w2, cmp_gain, cmp_flag, batch, seq)

        o_n = _nsa(p1, p2, kcv, gb, bias_n, overlap_t, membership, batch, seq)

        x2 = _out_proj(x2, o_a.reshape(n_tok, A_WIDTH), o_n.reshape(n_tok, NSA_WIDTH), out_norm[l],
                       w_out[l].astype(BF16), tm=tm_out)

        x2 = _ffn(x2, ffn2_norm[l], ffn2_w1[l].astype(BF16), ffn2_w3[l].astype(BF16),
                  ffn2_w2[l].astype(BF16), tm=tm_ffn, tf=tf_ffn)
    return x2.reshape(batch, seq, d)


def kernel(x, ffn1_norm, ffn1_w1, ffn1_w3, ffn1_w2, mix_norm, w_in, gate_bias, q_norm_a, k_norm_a, q_norm_nsa, k_norm_nsa, cmp_pos, cmp_k_w1, cmp_k_w2, cmp_v_w1, cmp_v_w2, out_norm, w_out, ffn2_norm, ffn2_w1, ffn2_w3, ffn2_w2, rel_bias):
    return _forward(x, ffn1_norm, ffn1_w1, ffn1_w3, ffn1_w2, mix_norm, w_in, gate_bias, q_norm_a, k_norm_a,
                    q_norm_nsa, k_norm_nsa, cmp_pos, cmp_k_w1, cmp_k_w2, cmp_v_w1, cmp_v_w2, out_norm, w_out,
                    ffn2_norm, ffn2_w1, ffn2_w3, ffn2_w2, rel_bias,
                    tm_ffn=512, tf_ffn=512, tm_proj=1024, tm_out=512)
```

```python
import functools
import math

import numpy as np
import jax
import jax.numpy as jnp
from jax import lax
from jax.experimental import pallas as pl
from jax.experimental.pallas import tpu as pltpu

D_MODEL = 2048
HEAD_DIM = 128
N_HEADS = 16
A_HEADS = 8
NSA_HEADS = 8
NSA_KV_GROUPS = 2
NSA_REP = 4
A_WIDTH = A_HEADS * HEAD_DIM
NSA_WIDTH = NSA_HEADS * HEAD_DIM
KV_WIDTH = NSA_KV_GROUPS * HEAD_DIM
N_BRANCH = 3
D_FF = 5632
DILATED_PATTERNS = ((128, 1), (512, 4), (2048, 16))
BAND = 128
CMP_STRIDE = 16
CMP_LEN = 32
CMP_HIDDEN = 512
SLC_LEN = 64
SLC_TOPK = 16
WIN_LEN = 512
N_BUCKETS = 32
MAX_DISTANCE = 2048
FORCE_SCORE = 1e6
NEG_INF = -1e30
EPS = 1e-6
LOG2E = 1.4426950408889634

LANES = 128
DIL_SPAN = 2048
NSA_TQ = 512
NSA_TK = 128
SEL_TK = 512
CMP_ROWS = 32
BIAS_FAR = 13
BIAS_FUTURE = 14
N_BIAS_TILES = 15
MASK_BIG = 1e30
VMEM_LIMIT = 56 * 1024 * 1024

F32 = jnp.float32
BF16 = jnp.bfloat16

P1_QA, P1_KA, P1_VA, P1_KC, P1_VC, P1_GL = 0, 8, 16, 24, 26, 28
P1_USED = 30
P1_HEADS = 32
P2_QN, P2_KS, P2_VS, P2_KW, P2_VW = 0, 8, 10, 12, 14
P2_HEADS = 16


def _params(sem, flags=None):
    return pltpu.CompilerParams(dimension_semantics=sem, vmem_limit_bytes=VMEM_LIMIT, flags=flags)


def _nt_dot(a, b):
    return lax.dot_general(a, b, (((1,), (1,)), ((), ())), preferred_element_type=F32)


def _dot(a, b):
    return jnp.dot(a, b, preferred_element_type=F32)


def _rel_bucket_np(dist):
    n = np.maximum(dist, 0)
    max_exact = N_BUCKETS // 2
    nf = np.maximum(n, 1).astype(np.float32)
    log_b = max_exact + (np.log(nf / np.float32(max_exact)) / np.float32(math.log(MAX_DISTANCE / max_exact))
                         * np.float32(N_BUCKETS - max_exact)).astype(np.int32)
    return np.where(n < max_exact, n, np.minimum(log_b, N_BUCKETS - 1)).astype(np.int32)


def _dilated_buckets():
    i = np.arange(BAND)[:, None]
    j = np.arange(2 * BAND)[None, :] - BAND
    diff = i - j
    out = []
    for window, dilation in DILATED_PATTERNS:
        mask = (diff >= 0) & (diff <= window // dilation)
        bkt = _rel_bucket_np(diff * dilation)
        out.append(np.where(mask, bkt, -1))
        out.append(np.where(mask & (j >= 0), bkt, -1))
    return np.stack(out).astype(np.int32)


def _nsa_buckets():
    i = np.arange(NSA_TK)[:, None]
    j = np.arange(NSA_TK)[None, :]
    tiles = []
    for m in range(BIAS_FAR + 1):
        dist = NSA_TK * m + i - j
        tiles.append(np.where(dist >= 0, _rel_bucket_np(dist), -1))
    assert (tiles[BIAS_FAR] == N_BUCKETS - 1).all()
    tiles.append(np.full((NSA_TK, NSA_TK), -1))
    assert len(tiles) == N_BIAS_TILES and BIAS_FUTURE == N_BIAS_TILES - 1
    return np.stack(tiles).astype(np.int32)


def _overlap_t_np(n_cmp_pad):
    c_start = np.arange(n_cmp_pad)[None, :] * CMP_STRIDE
    s_start = np.arange(LANES)[:, None] * SLC_LEN
    return ((c_start < s_start + SLC_LEN) & (c_start + CMP_LEN > s_start)).astype(np.float32)


def _block_membership_np(seq):
    k_blk = np.arange(seq)[:, None] // SLC_LEN
    return np.where(k_blk == np.arange(LANES)[None, :], MASK_BIG, 0.0).astype(np.float32)


def _bias_kernel(tbl_ref, bkt_ref, o_ref, *, n_heads, scale, shift_far):
    bkt = bkt_ref[0]
    for h in range(n_heads):
        acc = jnp.full(bkt.shape, NEG_INF, F32)
        base = tbl_ref[N_BUCKETS - 1, h] if shift_far else 0.0
        for b in range(N_BUCKETS):
            acc = jnp.where(bkt == b, (tbl_ref[b, h] - base) * scale, acc)
        o_ref[0, h] = acc


def _expand_bias(table, buckets, scale=1.0, shift_far=False):
    n_t, rows, width = buckets.shape
    nh = table.shape[1]
    return pl.pallas_call(
        functools.partial(_bias_kernel, n_heads=nh, scale=scale, shift_far=shift_far),
        grid=(n_t,),
        in_specs=[pl.BlockSpec(memory_space=pltpu.SMEM),
                  pl.BlockSpec((1, rows, width), lambda t: (t, 0, 0))],
        out_specs=pl.BlockSpec((1, nh, rows, width), lambda t: (t, 0, 0, 0)),
        out_shape=jax.ShapeDtypeStruct((n_t, nh, rows, width), F32),
        compiler_params=_params(("arbitrary",)),
        name="bias_expand",
    )(table, buckets)


def _ffn_kernel(x_ref, g_ref, w1_ref, w3_ref, w2_ref, o_ref, n_scr):
    j = pl.program_id(1)

    @pl.when(j == 0)
    def _():
        x = x_ref[...]
        ms = jnp.mean(x * x, axis=-1, keepdims=True)
        n_scr[...] = (x * lax.rsqrt(ms + EPS) * g_ref[...]).astype(BF16)
        o_ref[...] = jnp.zeros_like(o_ref)

    n = n_scr[...]
    h1 = _dot(n, w1_ref[...])
    h3 = _dot(n, w3_ref[...])
    a = (h1 * jax.nn.sigmoid(h1) * h3).astype(BF16)
    o_ref[...] += _dot(a, w2_ref[...])

    @pl.when(j == pl.num_programs(1) - 1)
    def _():
        o_ref[...] = x_ref[...] + 0.5 * o_ref[...]


def _ffn(x2, gain, w1, w3, w2, *, tm, tf):
    n_tok, d = x2.shape
    f = w1.shape[1]
    return pl.pallas_call(
        _ffn_kernel,
        grid=(n_tok // tm, f // tf),
        in_specs=[pl.BlockSpec((tm, d), lambda i, j: (i, 0)),
                  pl.BlockSpec((1, d), lambda i, j: (0, 0)),
                  pl.BlockSpec((d, tf), lambda i, j: (0, j)),
                  pl.BlockSpec((d, tf), lambda i, j: (0, j)),
                  pl.BlockSpec((tf, d), lambda i, j: (j, 0))],
        out_specs=pl.BlockSpec((tm, d), lambda i, j: (i, 0)),
        out_shape=jax.ShapeDtypeStruct((n_tok, d), F32),
        scratch_shapes=[pltpu.VMEM((tm, d), BF16)],
        compiler_params=_params(("arbitrary", "arbitrary")),
        name="ffn",
    )(x2, gain.reshape(1, d), w1, w3, w2)


def _proj_kernel(x_ref, g_ref, w_ref, cg_ref, cf_ref, o1_ref, o2_ref, n_scr, acc_scr, *, n1):
    j = pl.program_id(1)
    last = pl.num_programs(1) - 1

    def matmul():
        acc_scr[...] = _dot(n_scr[...], w_ref[...])

    def epilogue(o_ref):
        for h in range(acc_scr.shape[1] // LANES):
            cols = slice(h * LANES, (h + 1) * LANES)
            a = acc_scr[:, cols]
            ms = jnp.mean(a * a, axis=-1, keepdims=True)
            inv = jnp.where(cf_ref[:, cols] > 0.0, lax.rsqrt(ms + EPS), 1.0)
            o_ref[:, cols] = (a * inv * cg_ref[:, cols]).astype(o_ref.dtype)

    @pl.when(j == 0)
    def _():
        x = x_ref[...]
        ms = jnp.mean(x * x, axis=-1, keepdims=True)
        n_scr[...] = (x * lax.rsqrt(ms + EPS) * g_ref[...]).astype(BF16)
        matmul()

    @pl.when(jnp.logical_and(j > 0, j <= n1))
    def _():
        epilogue(o1_ref)
        matmul()

    @pl.when(jnp.logical_and(j > n1, j < last))
    def _():
        epilogue(o2_ref)
        matmul()

    @pl.when(j == last)
    def _():
        epilogue(o2_ref)


def _proj(x2, gain, w, col_gain, col_flag, cols1, *, tm, tn):
    n_tok, d = x2.shape
    n_cols = w.shape[1]
    n1 = cols1 // tn
    n2 = (n_cols - cols1) // tn
    assert n1 * tn == cols1 and (n1 + n2) * tn == n_cols
    n_j = n1 + n2
    prev = lambda j: jnp.maximum(j - 1, 0)
    return pl.pallas_call(
        functools.partial(_proj_kernel, n1=n1),
        grid=(n_tok // tm, n_j + 1),
        in_specs=[pl.BlockSpec((tm, d), lambda i, j: (i, 0)),
                  pl.BlockSpec((1, d), lambda i, j: (0, 0)),
                  pl.BlockSpec((d, tn), lambda i, j: (0, jnp.minimum(j, n_j - 1))),
                  pl.BlockSpec((1, tn), lambda i, j: (0, prev(j))),
                  pl.BlockSpec((1, tn), lambda i, j: (0, prev(j)))],
        out_specs=[pl.BlockSpec((tm, tn), lambda i, j: (i, jnp.minimum(prev(j), n1 - 1))),
                   pl.BlockSpec((tm, tn), lambda i, j: (i, jnp.maximum(prev(j) - n1, 0)))],
        out_shape=[jax.ShapeDtypeStruct((n_tok, cols1), F32),
                   jax.ShapeDtypeStruct((n_tok, n_cols - cols1), BF16)],
        scratch_shapes=[pltpu.VMEM((tm, d), BF16), pltpu.VMEM((tm, tn), F32)],
        compiler_params=_params(("arbitrary", "arbitrary")),
        name="in_proj",
    )(x2, gain.reshape(1, d), w, col_gain.reshape(1, n_cols), col_flag.reshape(1, n_cols))


def _dilated_kernel(q_ref, kc_ref, kp_ref, vc_ref, vp_ref, bias_ref, o_ref,
                    k_scr, v_scr, out_scr, lse_scr):
    n = pl.program_id(2)
    k_scr[0:DIL_SPAN, :] = kp_ref[0]
    k_scr[DIL_SPAN:2 * DIL_SPAN, :] = kc_ref[0]
    v_scr[0:DIL_SPAN, :] = vp_ref[0]
    v_scr[DIL_SPAN:2 * DIL_SPAN, :] = vc_ref[0]

    for p, (_, dil) in enumerate(DILATED_PATTERNS):
        span = BAND * dil

        def unit(u, carry, p=p, dil=dil, span=span):
            s = u // dil
            r = u - s * dil
            base = s * span + r
            first = jnp.logical_and(n == 0, s == 0).astype(jnp.int32)
            q = q_ref[0, pl.ds(base, BAND, stride=dil), :].astype(BF16)
            kk = k_scr[pl.ds(DIL_SPAN + base - span, 2 * BAND, stride=dil), :].astype(BF16)
            vv = v_scr[pl.ds(DIL_SPAN + base - span, 2 * BAND, stride=dil), :].astype(BF16)
            lg = _nt_dot(q, kk) + bias_ref[p, first, 0]
            m = jnp.max(lg, axis=-1, keepdims=True)
            e = jnp.exp2(lg - m).astype(BF16)
            acc = _dot(e, jnp.concatenate([vv, jnp.ones((2 * BAND, LANES), BF16)], axis=1))
            ssum = acc[:, LANES:]
            out_scr[p, pl.ds(base, BAND, stride=dil), :] = acc[:, :LANES] / ssum
            lse_scr[p, pl.ds(base, BAND, stride=dil), :] = m + jnp.log2(ssum)
            return carry

        lax.fori_loop(0, DIL_SPAN // BAND, unit, 0, unroll=8)

    chunk = 256

    def combine(c, carry):
        rows = pl.ds(pl.multiple_of(c * chunk, chunk), chunk)
        l0, l1, l2 = lse_scr[0, rows, :], lse_scr[1, rows, :], lse_scr[2, rows, :]
        big = jnp.maximum(jnp.maximum(l0, l1), l2)
        e0, e1, e2 = jnp.exp2(l0 - big), jnp.exp2(l1 - big), jnp.exp2(l2 - big)
        num = e0 * out_scr[0, rows, :] + e1 * out_scr[1, rows, :] + e2 * out_scr[2, rows, :]
        o_ref[0, rows, :] = num / (e0 + e1 + e2)
        return carry

    lax.fori_loop(0, DIL_SPAN // chunk, combine, 0)


def _dilated(p1, bias, batch, seq):
    blk = (1, DIL_SPAN, LANES)
    prev = lambda nn: jnp.maximum(nn - 1, 0)
    return pl.pallas_call(
        _dilated_kernel,
        grid=(batch, A_HEADS, seq // DIL_SPAN),
        in_specs=[pl.BlockSpec(blk, lambda b, h, nn: (b, nn, P1_QA + h)),
                  pl.BlockSpec(blk, lambda b, h, nn: (b, nn, P1_KA + h)),
                  pl.BlockSpec(blk, lambda b, h, nn: (b, prev(nn), P1_KA + h)),
                  pl.BlockSpec(blk, lambda b, h, nn: (b, nn, P1_VA + h)),
                  pl.BlockSpec(blk, lambda b, h, nn: (b, prev(nn), P1_VA + h)),
                  pl.BlockSpec((3, 2, 1, BAND, 2 * BAND), lambda b, h, nn: (0, 0, h, 0, 0))],
        out_specs=pl.BlockSpec(blk, lambda b, h, nn: (b, nn, h)),
        out_shape=jax.ShapeDtypeStruct((batch, seq, A_WIDTH), F32),
        scratch_shapes=[pltpu.VMEM((2 * DIL_SPAN, LANES), F32),
                        pltpu.VMEM((2 * DIL_SPAN, LANES), F32),
                        pltpu.VMEM((3, DIL_SPAN, LANES), F32),
                        pltpu.VMEM((3, DIL_SPAN, LANES), F32)],
        compiler_params=_params(("arbitrary", "arbitrary", "arbitrary")),
        name="dilated_attn",
    )(p1, p1, p1, p1, p1, bias)


def _compress_kernel(t_ref, pos_ref, w1_ref, w2_ref, gain_ref, flag_ref, o_ref, lo_scr, hi_scr):
    n_blk = o_ref.shape[3]
    lo_scr[...] = jnp.zeros_like(lo_scr)
    hi_scr[...] = jnp.zeros_like(hi_scr)
    for i in range(CMP_STRIDE):
        rows = t_ref[0, pl.ds(i, n_blk, stride=CMP_STRIDE), :]
        a = (rows + pos_ref[i:i + 1, :]).astype(BF16)
        lo_scr[...] += _dot(a, w1_ref[0, i * HEAD_DIM:(i + 1) * HEAD_DIM, :])
        b = (rows + pos_ref[CMP_STRIDE + i:CMP_STRIDE + i + 1, :]).astype(BF16)
        hi_scr[...] += _dot(b, w1_ref[0, (CMP_STRIDE + i) * HEAD_DIM:(CMP_STRIDE + i + 1) * HEAD_DIM, :])
    hid = lo_scr[...] + pltpu.roll(hi_scr[...], n_blk - 1, 0)
    o = _dot(jax.nn.gelu(hid).astype(BF16), w2_ref[0])
    ms = jnp.mean(o * o, axis=-1, keepdims=True)
    inv = jnp.where(flag_ref[0] > 0.0, lax.rsqrt(ms + EPS), 1.0)
    o_ref[0, 0, 0] = o * inv * gain_ref[0]


def _compress(p1, pos, w1, w2, gain, flag, batch, seq):
    n_blk = seq // CMP_STRIDE
    return pl.pallas_call(
        _compress_kernel,
        grid=(batch, 2, NSA_KV_GROUPS),
        in_specs=[pl.BlockSpec((1, seq, LANES), lambda b, kv, g: (b, 0, P1_KC + 2 * kv + g)),
                  pl.BlockSpec((CMP_LEN, HEAD_DIM), lambda b, kv, g: (0, 0)),
                  pl.BlockSpec((1, CMP_LEN * HEAD_DIM, CMP_HIDDEN), lambda b, kv, g: (kv, 0, 0)),
                  pl.BlockSpec((1, CMP_HIDDEN, HEAD_DIM), lambda b, kv, g: (kv, 0, 0)),
                  pl.BlockSpec((1, 1, HEAD_DIM), lambda b, kv, g: (kv, 0, 0)),
                  pl.BlockSpec((1, 1, HEAD_DIM), lambda b, kv, g: (kv, 0, 0))],
        out_specs=pl.BlockSpec((1, 1, 1, n_blk, HEAD_DIM), lambda b, kv, g: (b, kv, g, 0, 0)),
        out_shape=jax.ShapeDtypeStruct((batch, 2, NSA_KV_GROUPS, n_blk, HEAD_DIM), F32),
        scratch_shapes=[pltpu.VMEM((n_blk, CMP_HIDDEN), F32), pltpu.VMEM((n_blk, CMP_HIDDEN), F32)],
        compiler_params=_params(("arbitrary", "arbitrary", "arbitrary")),
        name="nsa_compress",
    )(p1, pos, w1, w2, gain, flag)


def _nsa_kernel(q_ref, ks_ref, vs_ref, kw_ref, vw_ref, kc_ref, vc_ref, gl_ref, gb_ref,
                bias_ref, ovl_ref, et_ref, o_ref, m_scr, acc_scr, s_scr, p_scr, a_scr, ocmp_scr, imp_scr,
                lc_scr, pcb_scr, split_scr):
    qi = pl.program_id(2)
    tq, rep = NSA_TQ, NSA_REP
    n_cmp = kc_ref.shape[3]

    q2 = q_ref[0]
    q_heads = [q2[:, r * HEAD_DIM:(r + 1) * HEAD_DIM] for r in range(rep)]
    t_col = qi * tq + lax.broadcasted_iota(jnp.int32, (tq, 1), 0)

    def compressed(width):
        cmp_end = lax.broadcasted_iota(jnp.int32, (1, width), 1) * CMP_STRIDE + (CMP_LEN - 1)
        kcb = kc_ref[0, 0, 0, 0:width, :].astype(BF16)
        vcb = vc_ref[0, 0, 0, 0:width, :].astype(BF16)
        for r in range(rep):
            lc_scr[r, :, 0:width] = _nt_dot(q_heads[r], kcb)
        for c in range(tq // CMP_ROWS):
            rows = slice(c * CMP_ROWS, (c + 1) * CMP_ROWS)
            cmask = cmp_end <= t_col[rows]
            pc_sum = jnp.zeros((CMP_ROWS, width), F32)
            for r in range(rep):
                lc = jnp.where(cmask, lc_scr[r, rows, 0:width], NEG_INF)
                mc = jnp.max(lc, axis=-1, keepdims=True)
                pc = jnp.where(cmask, jnp.exp2(lc - mc), 0.0)
                pc = pc / jnp.maximum(jnp.sum(pc, axis=-1, keepdims=True), 1e-30)
                pcb_scr[r, rows, 0:width] = pc.astype(BF16)
                pc_sum = pc_sum + pc
            hi = pc_sum.astype(BF16)
            rem = pc_sum - hi.astype(F32)
            mid = rem.astype(BF16)
            split_scr[0, rows, 0:width] = hi
            split_scr[1, rows, 0:width] = mid
            split_scr[2, rows, 0:width] = (rem - mid.astype(F32)).astype(BF16)
        for r in range(rep):
            ocmp_scr[r] = _dot(pcb_scr[r, :, 0:width], vcb)
        ovl_t = ovl_ref[:, 0:width]
        imp_scr[...] = (_nt_dot(ovl_t, split_scr[0, :, 0:width]) + _nt_dot(ovl_t, split_scr[1, :, 0:width])
                        + _nt_dot(ovl_t, split_scr[2, :, 0:width]))

    n_grp = n_cmp // LANES
    need = jnp.minimum(((qi + 1) * tq - CMP_LEN) // (CMP_STRIDE * LANES) + 1, n_grp)
    for k in range(1, n_grp + 1):
        pl.when(need == k)(functools.partial(compressed, k * LANES))
    o_cmp = [ocmp_scr[r] for r in range(rep)]
    imp_t = imp_scr[...]

    n_rb = tq // NSA_TK
    n_wt = WIN_LEN // NSA_TK + 1
    w_keys = n_wt * NSA_TK
    win_parts = [[] for _ in range(rep)]
    for a in range(n_rb):
        rows = slice(a * NSA_TK, (a + 1) * NSA_TK)
        blk_q = qi * n_rb + a
        j0 = jnp.maximum(blk_q - (n_wt - 1), 0)
        w_rows = pl.ds(pl.multiple_of(j0 * NSA_TK, NSA_TK), w_keys)
        kw_tile = kw_ref[0, w_rows, :]
        vw_aug = jnp.concatenate([vw_ref[0, w_rows, :], jnp.ones((w_keys, HEAD_DIM), BF16)], axis=1)
        dist = t_col[rows] - (j0 * NSA_TK + lax.broadcasted_iota(jnp.int32, (1, w_keys), 1))
        wmask = jnp.where(dist >= 0, dist, WIN_LEN) < WIN_LEN
        tiles = [jnp.clip(blk_q - j0 - i, 0, BIAS_FAR) for i in range(n_wt)]
        for r in range(rep):
            s = _nt_dot(q_heads[r][rows], kw_tile) + jnp.concatenate([bias_ref[t, r] for t in tiles], axis=-1)
            s = jnp.where(wmask, s, NEG_INF)
            p = jnp.exp2(s - jnp.max(s, axis=-1, keepdims=True)).astype(BF16)
            acc = _dot(p, vw_aug)
            win_parts[r].append(acc[:, :HEAD_DIM] / acc[:, HEAD_DIM:])
    o_win = [jnp.concatenate(parts, axis=0) for parts in win_parts]

    blk = lax.broadcasted_iota(jnp.int32, (LANES, 1), 0)
    blk_f = blk.astype(F32)
    t_row = qi * tq + lax.broadcasted_iota(jnp.int32, (1, tq), 1)
    cur = t_row >> (SLC_LEN.bit_length() - 1)
    valid = blk * SLC_LEN <= t_row
    forced = (blk == 0) | (blk == cur) | (blk == cur - 1)
    score = jnp.where(valid, jnp.where(forced, FORCE_SCORE, imp_t), -1.0)
    sel_t = jnp.zeros((LANES, tq), F32)
    for _ in range(SLC_TOPK):
        best = jnp.max(score, axis=0, keepdims=True)
        idx = jnp.min(jnp.where(score == best, blk_f, float(LANES)), axis=0, keepdims=True)
        hit = blk_f == idx
        sel_t = jnp.where(hit, 1.0, sel_t)
        score = jnp.where(hit, -3e38, score)
    unsel = (sel_t.T - 1.0).astype(BF16)
    lhs = [jnp.concatenate([q_heads[r], unsel], axis=1) for r in range(rep)]

    ones_v = jnp.ones((SEL_TK, HEAD_DIM), BF16)

    sub = SEL_TK // NSA_TK
    n_tiles = ((qi + 1) * n_rb - 1) // sub + 1

    def key_rows(jj):
        return pl.ds(pl.multiple_of(jj * SEL_TK, SEL_TK), SEL_TK)

    def logits_stage(jj, heads):
        rows = key_rows(jj)
        rhs = jnp.concatenate([ks_ref[0, rows, :], et_ref[rows, :]], axis=1)
        for r in heads:
            s_scr[r] = _nt_dot(lhs[r], rhs)

    def softmax_stage(jj, heads):
        for a in range(n_rb):
            rows = slice(a * NSA_TK, (a + 1) * NSA_TK)
            u = qi * n_rb + a - sub * jj
            tiles = [jnp.where(u < i, BIAS_FUTURE, jnp.minimum(u - i, BIAS_FAR)) for i in range(sub)]
            for r in heads:
                s = s_scr[r, rows, :] + jnp.concatenate([bias_ref[t, r] for t in tiles], axis=-1)
                m_old = m_scr[r, rows, :]
                m_new = jnp.maximum(m_old, jnp.max(s, axis=-1, keepdims=True))
                a_scr[r, rows, :] = jnp.exp2(m_old - m_new)
                p_scr[r, rows, :] = jnp.exp2(s - jnp.concatenate([m_new] * sub, axis=1)).astype(BF16)
                m_scr[r, rows, :] = m_new

    def pv_stage(jj, heads):
        v_aug = jnp.concatenate([vs_ref[0, key_rows(jj), :], ones_v], axis=1)
        for r in heads:
            alpha = a_scr[r]
            acc_scr[r] = jnp.concatenate([alpha, alpha], axis=1) * acc_scr[r] + _dot(p_scr[r], v_aug)

    all_heads = range(rep)
    m_scr[...] = jnp.full(m_scr.shape, NEG_INF, F32)
    acc_scr[...] = jnp.zeros_like(acc_scr)
    p_scr[...] = jnp.zeros_like(p_scr)
    a_scr[...] = jnp.ones_like(a_scr)
    logits_stage(0, all_heads)

    def sel_step(it, carry):
        pv_stage(jnp.maximum(it - 1, 0), all_heads)
        softmax_stage(it, all_heads)
        logits_stage(jnp.minimum(it + 1, n_tiles - 1), all_heads)
        return carry

    lax.fori_loop(0, n_tiles, sel_step, 0)
    pv_stage(n_tiles - 1, all_heads)

    gates = jax.nn.sigmoid(gl_ref[0] + gb_ref[0])
    o_slc = []
    for r in range(rep):
        acc = acc_scr[r]
        o_slc.append(acc[:, :HEAD_DIM] / acc[:, HEAD_DIM:])
    for r in range(rep):
        c = r * N_BRANCH
        out = (gates[:, c:c + 1] * o_cmp[r] + gates[:, c + 1:c + 2] * o_slc[r]
               + gates[:, c + 2:c + 3] * o_win[r])
        o_ref[0, :, r * HEAD_DIM:(r + 1) * HEAD_DIM] = out


def _nsa(p1, p2, kcv, gate_bias, bias, overlap_t, membership, batch, seq):
    tq = NSA_TQ
    n_cmp = seq // CMP_STRIDE
    kv_blk = (1, seq, LANES)
    grp_w = NSA_REP * HEAD_DIM
    once = pl.Buffered(1)
    return pl.pallas_call(
        _nsa_kernel,
        grid=(batch, NSA_KV_GROUPS, seq // tq),
        in_specs=[pl.BlockSpec((1, tq, grp_w), lambda b, g, i: (b, i, g)),
                  pl.BlockSpec(kv_blk, lambda b, g, i: (b, 0, P2_KS + g), pipeline_mode=once),
                  pl.BlockSpec(kv_blk, lambda b, g, i: (b, 0, P2_VS + g), pipeline_mode=once),
                  pl.BlockSpec(kv_blk, lambda b, g, i: (b, 0, P2_KW + g), pipeline_mode=once),
                  pl.BlockSpec(kv_blk, lambda b, g, i: (b, 0, P2_VW + g), pipeline_mode=once),
                  pl.BlockSpec((1, 1, 1, n_cmp, HEAD_DIM), lambda b, g, i: (b, 0, g, 0, 0)),
                  pl.BlockSpec((1, 1, 1, n_cmp, HEAD_DIM), lambda b, g, i: (b, 1, g, 0, 0)),
                  pl.BlockSpec((1, tq, LANES), lambda b, g, i: (b, i, P1_GL + g)),
                  pl.BlockSpec((1, 1, LANES), lambda b, g, i: (g, 0, 0)),
                  pl.BlockSpec((N_BIAS_TILES, NSA_REP, NSA_TK, NSA_TK), lambda b, g, i: (0, g, 0, 0),
                               pipeline_mode=once),
                  pl.BlockSpec((LANES, n_cmp), lambda b, g, i: (0, 0)),
                  pl.BlockSpec((seq, LANES), lambda b, g, i: (0, 0))],
        out_specs=pl.BlockSpec((1, tq, grp_w), lambda b, g, i: (b, i, g)),
        out_shape=jax.ShapeDtypeStruct((batch, seq, NSA_WIDTH), F32),
        scratch_shapes=[pltpu.VMEM((NSA_REP, tq, LANES), F32),
                        pltpu.VMEM((NSA_REP, tq, 2 * HEAD_DIM), F32),
                        pltpu.VMEM((NSA_REP, tq, SEL_TK), F32),
                        pltpu.VMEM((NSA_REP, tq, SEL_TK), BF16),
                        pltpu.VMEM((NSA_REP, tq, LANES), F32),
                        pltpu.VMEM((NSA_REP, tq, HEAD_DIM), F32),
                        pltpu.VMEM((LANES, tq), F32),
                        pltpu.VMEM((NSA_REP, tq, n_cmp), F32),
                        pltpu.VMEM((NSA_REP, tq, n_cmp), BF16),
                        pltpu.VMEM((3, tq, n_cmp), BF16)],
        compiler_params=_params(("arbitrary", "arbitrary", "arbitrary")),
        name="nsa_attn",
    )(p2, p2, p2, p2, p2, kcv, kcv, p1, gate_bias, bias, overlap_t, membership)


def _out_kernel(x_ref, oa_ref, on_ref, ga_ref, gn_ref, w_ref, o_ref):
    def norm(t, g):
        ms = jnp.mean(t * t, axis=-1, keepdims=True)
        return (t * lax.rsqrt(ms + EPS) * g).astype(BF16)

    ya = norm(oa_ref[...], ga_ref[...])
    yn = norm(on_ref[...], gn_ref[...])
    y = _dot(ya, w_ref[0:A_WIDTH, :]) + _dot(yn, w_ref[A_WIDTH:A_WIDTH + NSA_WIDTH, :])
    o_ref[...] = x_ref[...] + y


def _out_proj(x2, oa, on, gain, w, *, tm):
    n_tok, d = x2.shape
    return pl.pallas_call(
        _out_kernel,
        grid=(n_tok // tm,),
        in_specs=[pl.BlockSpec((tm, d), lambda i: (i, 0)),
                  pl.BlockSpec((tm, A_WIDTH), lambda i: (i, 0)),
                  pl.BlockSpec((tm, NSA_WIDTH), lambda i: (i, 0)),
                  pl.BlockSpec((1, A_WIDTH), lambda i: (0, 0)),
                  pl.BlockSpec((1, NSA_WIDTH), lambda i: (0, 0)),
                  pl.BlockSpec((d, d), lambda i: (0, 0))],
        out_specs=pl.BlockSpec((tm, d), lambda i: (i, 0)),
        out_shape=jax.ShapeDtypeStruct((n_tok, d), F32),
        compiler_params=_params(("arbitrary",)),
        name="out_proj",
    )(x2, oa, on, gain[:A_WIDTH].reshape(1, A_WIDTH), gain[A_WIDTH:].reshape(1, NSA_WIDTH), w)


def _pack_w_in(w_in, gate_bias, q_norm_a, k_norm_a, q_norm_nsa, k_norm_nsa):
    scale = HEAD_DIM ** -0.5
    sizes = (A_WIDTH,) * 3 + (NSA_WIDTH,) + (KV_WIDTH,) * 6 + (N_BRANCH * NSA_HEADS,)
    offs = np.concatenate([[0], np.cumsum(sizes)])
    qa, ka, va, qn, kc, vc, k_s, v_s, k_w, v_w, gl = [w_in[:, offs[i]:offs[i + 1]] for i in range(11)]
    per_grp = NSA_REP * N_BRANCH
    pad = jnp.zeros((w_in.shape[0], LANES - per_grp), w_in.dtype)
    gl_cols = [jnp.concatenate([gl[:, g * per_grp:(g + 1) * per_grp], pad], axis=1) for g in range(NSA_KV_GROUPS)]
    n_pad = P1_HEADS - P1_USED
    w_pad = jnp.zeros((w_in.shape[0], n_pad * LANES), w_in.dtype)
    w = jnp.concatenate([qa, ka, va, kc, vc] + gl_cols + [w_pad, qn, k_s, v_s, k_w, v_w], axis=1).astype(BF16)

    ones = jnp.ones((HEAD_DIM,), F32)
    tile = lambda v, n: jnp.tile(v, n)
    gain = jnp.concatenate([tile(q_norm_a * (scale * LOG2E), A_HEADS), tile(k_norm_a, A_HEADS),
                            tile(ones, A_HEADS + 6 + n_pad),
                            tile(q_norm_nsa * (scale * LOG2E), NSA_HEADS), tile(k_norm_nsa, 2), tile(ones, 2),
                            tile(k_norm_nsa, 2), tile(ones, 2)])
    flag = jnp.concatenate([jnp.ones((2 * A_WIDTH,), F32), jnp.zeros((A_WIDTH + (6 + n_pad) * HEAD_DIM,), F32),
                            jnp.ones((NSA_WIDTH + KV_WIDTH,), F32), jnp.zeros((KV_WIDTH,), F32),
                            jnp.ones((KV_WIDTH,), F32), jnp.zeros((KV_WIDTH,), F32)])
    gb = jnp.stack([jnp.concatenate([gate_bias[g * per_grp:(g + 1) * per_grp],
                                     jnp.zeros((LANES - per_grp,), F32)]) for g in range(NSA_KV_GROUPS)])
    return w, gain, flag, gb.reshape(NSA_KV_GROUPS, 1, LANES)


def _forward(x, ffn1_norm, ffn1_w1, ffn1_w3, ffn1_w2, mix_norm, w_in, gate_bias, q_norm_a, k_norm_a,
             q_norm_nsa, k_norm_nsa, cmp_pos, cmp_k_w1, cmp_k_w2, cmp_v_w1, cmp_v_w2, out_norm, w_out,
             ffn2_norm, ffn2_w1, ffn2_w3, ffn2_w2, rel_bias, *, tm_ffn, tf_ffn, tm_proj, tm_out):
    batch, seq, d = x.shape
    depth = ffn1_w1.shape[0]
    n_tok = batch * seq
    assert seq % DIL_SPAN == 0 and seq // SLC_LEN <= LANES

    bias_a = _expand_bias(rel_bias[:, :A_HEADS], jnp.asarray(_dilated_buckets()), scale=LOG2E)
    bias_a = bias_a.reshape(3, 2, A_HEADS, BAND, 2 * BAND)
    bias_n = _expand_bias(rel_bias[:, A_HEADS:], jnp.asarray(_nsa_buckets()),
                          scale=LOG2E, shift_far=True)
    overlap_t = jnp.asarray(_overlap_t_np(seq // CMP_STRIDE)).astype(BF16)
    membership = jnp.asarray(_block_membership_np(seq)).astype(BF16)

    x2 = x.reshape(n_tok, d)
    for l in range(depth):
        x2 = _ffn(x2, ffn1_norm[l], ffn1_w1[l].astype(BF16), ffn1_w3[l].astype(BF16),
                  ffn1_w2[l].astype(BF16), tm=tm_ffn, tf=tf_ffn)

        w_p, g_p, f_p, gb = _pack_w_in(w_in[l], gate_bias[l], q_norm_a[l], k_norm_a[l],
                                       q_norm_nsa[l], k_norm_nsa[l])
        p1, p2 = _proj(x2, mix_norm[l], w_p, g_p, f_p, P1_HEADS * LANES, tm=tm_proj, tn=1024)
        p1 = p1.reshape(batch, seq, P1_HEADS * LANES)
        p2 = p2.reshape(batch, seq, P2_HEADS * LANES)

        o_a = _dilated(p1, bias_a, batch, seq)

        cmp_w1 = jnp.stack([cmp_k_w1[l], cmp_v_w1[l]]).astype(BF16)
        cmp_w2 = jnp.stack([cmp_k_w2[l], cmp_v_w2[l]]).astype(BF16)
        cmp_gain = jnp.stack([k_norm_nsa[l], jnp.ones((HEAD_DIM,), F32)]).reshape(2, 1, HEAD_DIM)
        cmp_flag = jnp.stack([jnp.ones((HEAD_DIM,), F32), jnp.zeros((HEAD_DIM,), F32)]).reshape(2, 1, HEAD_DIM)
        kcv = _compress(p1, cmp_pos[l], cmp_w1, cmp_w2, cmp_gain, cmp_flag, batch, seq)

        o_n = _nsa(p1, p2, kcv, gb, bias_n, overlap_t, membership, batch, seq)

        x2 = _out_proj(x2, o_a.reshape(n_tok, A_WIDTH), o_n.reshape(n_tok, NSA_WIDTH), out_norm[l],
                       w_out[l].astype(BF16), tm=tm_out)

        x2 = _ffn(x2, ffn2_norm[l], ffn2_w1[l].astype(BF16), ffn2_w3[l].astype(BF16),
                  ffn2_w2[l].astype(BF16), tm=tm_ffn, tf=tf_ffn)
    return x2.reshape(batch, seq, d)


def kernel(x, ffn1_norm, ffn1_w1, ffn1_w3, ffn1_w2, mix_norm, w_in, gate_bias, q_norm_a, k_norm_a, q_norm_nsa, k_norm_nsa, cmp_pos, cmp_k_w1, cmp_k_w2, cmp_v_w1, cmp_v_w2, out_norm, w_out, ffn2_norm, ffn2_w1, ffn2_w3, ffn2_w2, rel_bias):
    return _forward(x, ffn1_norm, ffn1_w1, ffn1_w3, ffn1_w2, mix_norm, w_in, gate_bias, q_norm_a, k_norm_a,
                    q_norm_nsa, k_norm_nsa, cmp_pos, cmp_k_w1, cmp_k_w2, cmp_v_w1, cmp_v_w2, out_norm, w_out,
                    ffn2_norm, ffn2_w1, ffn2_w3, ffn2_w2, rel_bias,
                    tm_ffn=512, tf_ffn=512, tm_proj=1024, tm_out=512)
```

```python
import functools
import math

import numpy as np
import jax
import jax.numpy as jnp
from jax import lax
from jax.experimental import pallas as pl
from jax.experimental.pallas import tpu as pltpu

D_MODEL = 2048
HEAD_DIM = 128
N_HEADS = 16
A_HEADS = 8
NSA_HEADS = 8
NSA_KV_GROUPS = 2
NSA_REP = 4
A_WIDTH = A_HEADS * HEAD_DIM
NSA_WIDTH = NSA_HEADS * HEAD_DIM
KV_WIDTH = NSA_KV_GROUPS * HEAD_DIM
N_BRANCH = 3
D_FF = 5632
DILATED_PATTERNS = ((128, 1), (512, 4), (2048, 16))
BAND = 128
CMP_STRIDE = 16
CMP_LEN = 32
CMP_HIDDEN = 512
SLC_LEN = 64
SLC_TOPK = 16
WIN_LEN = 512
N_BUCKETS = 32
MAX_DISTANCE = 2048
FORCE_SCORE = 1e6
NEG_INF = -1e30
EPS = 1e-6
LOG2E = 1.4426950408889634

LANES = 128
DIL_SPAN = 2048
NSA_TQ = 512
NSA_TK = 128
SEL_TK = 512
CMP_ROWS = 32
BIAS_FAR = 13
BIAS_FUTURE = 14
N_BIAS_TILES = 15
MASK_BIG = 1e30
VMEM_LIMIT = 56 * 1024 * 1024

F32 = jnp.float32
BF16 = jnp.bfloat16

P1_QA, P1_KA, P1_VA, P1_KC, P1_VC, P1_GL = 0, 8, 16, 24, 26, 28
P1_USED = 30
P1_HEADS = 32
P2_QN, P2_KS, P2_VS, P2_KW, P2_VW = 0, 8, 10, 12, 14
P2_HEADS = 16


def _params(sem, flags=None):
    return pltpu.CompilerParams(dimension_semantics=sem, vmem_limit_bytes=VMEM_LIMIT, flags=flags)


def _nt_dot(a, b):
    return lax.dot_general(a, b, (((1,), (1,)), ((), ())), preferred_element_type=F32)


def _dot(a, b):
    return jnp.dot(a, b, preferred_element_type=F32)


def _rel_bucket_np(dist):
    n = np.maximum(dist, 0)
    max_exact = N_BUCKETS // 2
    nf = np.maximum(n, 1).astype(np.float32)
    log_b = max_exact + (np.log(nf / np.float32(max_exact)) / np.float32(math.log(MAX_DISTANCE / max_exact))
                         * np.float32(N_BUCKETS - max_exact)).astype(np.int32)
    return np.where(n < max_exact, n, np.minimum(log_b, N_BUCKETS - 1)).astype(np.int32)


def _dilated_buckets():
    i = np.arange(BAND)[:, None]
    j = np.arange(2 * BAND)[None, :] - BAND
    diff = i - j
    out = []
    for window, dilation in DILATED_PATTERNS:
        mask = (diff >= 0) & (diff <= window // dilation)
        bkt = _rel_bucket_np(diff * dilation)
        out.append(np.where(mask, bkt, -1))
        out.append(np.where(mask & (j >= 0), bkt, -1))
    return np.stack(out).astype(np.int32)


def _nsa_buckets():
    i = np.arange(NSA_TK)[:, None]
    j = np.arange(NSA_TK)[None, :]
    tiles = []
    for m in range(BIAS_FAR + 1):
        dist = NSA_TK * m + i - j
        tiles.append(np.where(dist >= 0, _rel_bucket_np(dist), -1))
    assert (tiles[BIAS_FAR] == N_BUCKETS - 1).all()
    tiles.append(np.full((NSA_TK, NSA_TK), -1))
    assert len(tiles) == N_BIAS_TILES and BIAS_FUTURE == N_BIAS_TILES - 1
    return np.stack(tiles).astype(np.int32)


def _overlap_t_np(n_cmp_pad):
    c_start = np.arange(n_cmp_pad)[None, :] * CMP_STRIDE
    s_start = np.arange(LANES)[:, None] * SLC_LEN
    return ((c_start < s_start + SLC_LEN) & (c_start + CMP_LEN > s_start)).astype(np.float32)


def _block_membership_np(seq):
    k_blk = np.arange(seq)[:, None] // SLC_LEN
    return np.where(k_blk == np.arange(LANES)[None, :], MASK_BIG, 0.0).astype(np.float32)


def _bias_kernel(tbl_ref, bkt_ref, o_ref, *, n_heads, scale, shift_far):
    bkt = bkt_ref[0]
    for h in range(n_heads):
        acc = jnp.full(bkt.shape, NEG_INF, F32)
        base = tbl_ref[N_BUCKETS - 1, h] if shift_far else 0.0
        for b in range(N_BUCKETS):
            acc = jnp.where(bkt == b, (tbl_ref[b, h] - base) * scale, acc)
        o_ref[0, h] = acc


def _expand_bias(table, buckets, scale=1.0, shift_far=False):
    n_t, rows, width = buckets.shape
    nh = table.shape[1]
    return pl.pallas_call(
        functools.partial(_bias_kernel, n_heads=nh, scale=scale, shift_far=shift_far),
        grid=(n_t,),
        in_specs=[pl.BlockSpec(memory_space=pltpu.SMEM),
                  pl.BlockSpec((1, rows, width), lambda t: (t, 0, 0))],
        out_specs=pl.BlockSpec((1, nh, rows, width), lambda t: (t, 0, 0, 0)),
        out_shape=jax.ShapeDtypeStruct((n_t, nh, rows, width), F32),
        compiler_params=_params(("arbitrary",)),
        name="bias_expand",
    )(table, buckets)


def _ffn_kernel(x_ref, g_ref, w1_ref, w3_ref, w2_ref, o_ref, n_scr):
    j = pl.program_id(1)

    @pl.when(j == 0)
    def _():
        x = x_ref[...]
        ms = jnp.mean(x * x, axis=-1, keepdims=True)
        n_scr[...] = (x * lax.rsqrt(ms + EPS) * g_ref[...]).astype(BF16)
        o_ref[...] = jnp.zeros_like(o_ref)

    n = n_scr[...]
    h1 = _dot(n, w1_ref[...])
    h3 = _dot(n, w3_ref[...])
    a = (h1 * jax.nn.sigmoid(h1) * h3).astype(BF16)
    o_ref[...] += _dot(a, w2_ref[...])

    @pl.when(j == pl.num_programs(1) - 1)
    def _():
        o_ref[...] = x_ref[...] + 0.5 * o_ref[...]


def _ffn(x2, gain, w1, w3, w2, *, tm, tf):
    n_tok, d = x2.shape
    f = w1.shape[1]
    return pl.pallas_call(
        _ffn_kernel,
        grid=(n_tok // tm, f // tf),
        in_specs=[pl.BlockSpec((tm, d), lambda i, j: (i, 0)),
                  pl.BlockSpec((1, d), lambda i, j: (0, 0)),
                  pl.BlockSpec((d, tf), lambda i, j: (0, j)),
                  pl.BlockSpec((d, tf), lambda i, j: (0, j)),
                  pl.BlockSpec((tf, d), lambda i, j: (j, 0))],
        out_specs=pl.BlockSpec((tm, d), lambda i, j: (i, 0)),
        out_shape=jax.ShapeDtypeStruct((n_tok, d), F32),
        scratch_shapes=[pltpu.VMEM((tm, d), BF16)],
        compiler_params=_params(("arbitrary", "arbitrary")),
        name="ffn",
    )(x2, gain.reshape(1, d), w1, w3, w2)


def _proj_kernel(x_ref, g_ref, w_ref, cg_ref, cf_ref, o1_ref, o2_ref, n_scr, acc_scr, *, n1):
    j = pl.program_id(1)
    last = pl.num_programs(1) - 1

    def matmul():
        acc_scr[...] = _dot(n_scr[...], w_ref[...])

    def epilogue(o_ref):
        for h in range(acc_scr.shape[1] // LANES):
            cols = slice(h * LANES, (h + 1) * LANES)
            a = acc_scr[:, cols]
            ms = jnp.mean(a * a, axis=-1, keepdims=True)
            inv = jnp.where(cf_ref[:, cols] > 0.0, lax.rsqrt(ms + EPS), 1.0)
            o_ref[:, cols] = (a * inv * cg_ref[:, cols]).astype(o_ref.dtype)

    @pl.when(j == 0)
    def _():
        x = x_ref[...]
        ms = jnp.mean(x * x, axis=-1, keepdims=True)
        n_scr[...] = (x * lax.rsqrt(ms + EPS) * g_ref[...]).astype(BF16)
        matmul()

    @pl.when(jnp.logical_and(j > 0, j <= n1))
    def _():
        epilogue(o1_ref)
        matmul()

    @pl.when(jnp.logical_and(j > n1, j < last))
    def _():
        epilogue(o2_ref)
        matmul()

    @pl.when(j == last)
    def _():
        epilogue(o2_ref)


def _proj(x2, gain, w, col_gain, col_flag, cols1, *, tm, tn):
    n_tok, d = x2.shape
    n_cols = w.shape[1]
    n1 = cols1 // tn
    n2 = (n_cols - cols1) // tn
    assert n1 * tn == cols1 and (n1 + n2) * tn == n_cols
    n_j = n1 + n2
    prev = lambda j: jnp.maximum(j - 1, 0)
    return pl.pallas_call(
        functools.partial(_proj_kernel, n1=n1),
        grid=(n_tok // tm, n_j + 1),
        in_specs=[pl.BlockSpec((tm, d), lambda i, j: (i, 0)),
                  pl.BlockSpec((1, d), lambda i, j: (0, 0)),
                  pl.BlockSpec((d, tn), lambda i, j: (0, jnp.minimum(j, n_j - 1))),
                  pl.BlockSpec((1, tn), lambda i, j: (0, prev(j))),
                  pl.BlockSpec((1, tn), lambda i, j: (0, prev(j)))],
        out_specs=[pl.BlockSpec((tm, tn), lambda i, j: (i, jnp.minimum(prev(j), n1 - 1))),
                   pl.BlockSpec((tm, tn), lambda i, j: (i, jnp.maximum(prev(j) - n1, 0)))],
        out_shape=[jax.ShapeDtypeStruct((n_tok, cols1), F32),
                   jax.ShapeDtypeStruct((n_tok, n_cols - cols1), BF16)],
        scratch_shapes=[pltpu.VMEM((tm, d), BF16), pltpu.VMEM((tm, tn), F32)],
        compiler_params=_params(("arbitrary", "arbitrary")),
        name="in_proj",
    )(x2, gain.reshape(1, d), w, col_gain.reshape(1, n_cols), col_flag.reshape(1, n_cols))


def _dilated_kernel(q_ref, kc_ref, kp_ref, vc_ref, vp_ref, bias_ref, o_ref,
                    k_scr, v_scr, out_scr, lse_scr):
    n = pl.program_id(2)
    k_scr[0:DIL_SPAN, :] = kp_ref[0]
    k_scr[DIL_SPAN:2 * DIL_SPAN, :] = kc_ref[0]
    v_scr[0:DIL_SPAN, :] = vp_ref[0]
    v_scr[DIL_SPAN:2 * DIL_SPAN, :] = vc_ref[0]

    for p, (_, dil) in enumerate(DILATED_PATTERNS):
        span = BAND * dil

        def unit(u, carry, p=p, dil=dil, span=span):
            s = u // dil
            r = u - s * dil
            base = s * span + r
            first = jnp.logical_and(n == 0, s == 0).astype(jnp.int32)
            q = q_ref[0, pl.ds(base, BAND, stride=dil), :].astype(BF16)
            kk = k_scr[pl.ds(DIL_SPAN + base - span, 2 * BAND, stride=dil), :].astype(BF16)
            vv = v_scr[pl.ds(DIL_SPAN + base - span, 2 * BAND, stride=dil), :].astype(BF16)
            lg = _nt_dot(q, kk) + bias_ref[p, first, 0]
            m = jnp.max(lg, axis=-1, keepdims=True)
            e = jnp.exp2(lg - m).astype(BF16)
            acc = _dot(e, jnp.concatenate([vv, jnp.ones((2 * BAND, LANES), BF16)], axis=1))
            ssum = acc[:, LANES:]
            out_scr[p, pl.ds(base, BAND, stride=dil), :] = acc[:, :LANES] / ssum
            lse_scr[p, pl.ds(base, BAND, stride=dil), :] = m + jnp.log2(ssum)
            return carry

        lax.fori_loop(0, DIL_SPAN // BAND, unit, 0, unroll=8)

    chunk = 256

    def combine(c, carry):
        rows = pl.ds(pl.multiple_of(c * chunk, chunk), chunk)
        l0, l1, l2 = lse_scr[0, rows, :], lse_scr[1, rows, :], lse_scr[2, rows, :]
        big = jnp.maximum(jnp.maximum(l0, l1), l2)
        e0, e1, e2 = jnp.exp2(l0 - big), jnp.exp2(l1 - big), jnp.exp2(l2 - big)
        num = e0 * out_scr[0, rows, :] + e1 * out_scr[1, rows, :] + e2 * out_scr[2, rows, :]
        o_ref[0, rows, :] = num / (e0 + e1 + e2)
        return carry

    lax.fori_loop(0, DIL_SPAN // chunk, combine, 0)


def _dilated(p1, bias, batch, seq):
    blk = (1, DIL_SPAN, LANES)
    prev = lambda nn: jnp.maximum(nn - 1, 0)
    return pl.pallas_call(
        _dilated_kernel,
        grid=(batch, A_HEADS, seq // DIL_SPAN),
        in_specs=[pl.BlockSpec(blk, lambda b, h, nn: (b, nn, P1_QA + h)),
                  pl.BlockSpec(blk, lambda b, h, nn: (b, nn, P1_KA + h)),
                  pl.BlockSpec(blk, lambda b, h, nn: (b, prev(nn), P1_KA + h)),
                  pl.BlockSpec(blk, lambda b, h, nn: (b, nn, P1_VA + h)),
                  pl.BlockSpec(blk, lambda b, h, nn: (b, prev(nn), P1_VA + h)),
                  pl.BlockSpec((3, 2, 1, BAND, 2 * BAND), lambda b, h, nn: (0, 0, h, 0, 0))],
        out_specs=pl.BlockSpec(blk, lambda b, h, nn: (b, nn, h)),
        out_shape=jax.ShapeDtypeStruct((batch, seq, A_WIDTH), F32),
        scratch_shapes=[pltpu.VMEM((2 * DIL_SPAN, LANES), F32),
                        pltpu.VMEM((2 * DIL_SPAN, LANES), F32),
                        pltpu.VMEM((3, DIL_SPAN, LANES), F32),
                        pltpu.VMEM((3, DIL_SPAN, LANES), F32)],
        compiler_params=_params(("arbitrary", "arbitrary", "arbitrary")),
        name="dilated_attn",
    )(p1, p1, p1, p1, p1, bias)


def _compress_kernel(t_ref, pos_ref, w1_ref, w2_ref, gain_ref, flag_ref, o_ref, lo_scr, hi_scr):
    n_blk = o_ref.shape[3]
    lo_scr[...] = jnp.zeros_like(lo_scr)
    hi_scr[...] = jnp.zeros_like(hi_scr)
    for i in range(CMP_STRIDE):
        rows = t_ref[0, pl.ds(i, n_blk, stride=CMP_STRIDE), :]
        a = (rows + pos_ref[i:i + 1, :]).astype(BF16)
        lo_scr[...] += _dot(a, w1_ref[0, i * HEAD_DIM:(i + 1) * HEAD_DIM, :])
        b = (rows + pos_ref[CMP_STRIDE + i:CMP_STRIDE + i + 1, :]).astype(BF16)
        hi_scr[...] += _dot(b, w1_ref[0, (CMP_STRIDE + i) * HEAD_DIM:(CMP_STRIDE + i + 1) * HEAD_DIM, :])
    hid = lo_scr[...] + pltpu.roll(hi_scr[...], n_blk - 1, 0)
    o = _dot(jax.nn.gelu(hid).astype(BF16), w2_ref[0])
    ms = jnp.mean(o * o, axis=-1, keepdims=True)
    inv = jnp.where(flag_ref[0] > 0.0, lax.rsqrt(ms + EPS), 1.0)
    o_ref[0, 0, 0] = o * inv * gain_ref[0]


def _compress(p1, pos, w1, w2, gain, flag, batch, seq):
    n_blk = seq // CMP_STRIDE
    return pl.pallas_call(
        _compress_kernel,
        grid=(batch, 2, NSA_KV_GROUPS),
        in_specs=[pl.BlockSpec((1, seq, LANES), lambda b, kv, g: (b, 0, P1_KC + 2 * kv + g)),
                  pl.BlockSpec((CMP_LEN, HEAD_DIM), lambda b, kv, g: (0, 0)),
                  pl.BlockSpec((1, CMP_LEN * HEAD_DIM, CMP_HIDDEN), lambda b, kv, g: (kv, 0, 0)),
                  pl.BlockSpec((1, CMP_HIDDEN, HEAD_DIM), lambda b, kv, g: (kv, 0, 0)),
                  pl.BlockSpec((1, 1, HEAD_DIM), lambda b, kv, g: (kv, 0, 0)),
                  pl.BlockSpec((1, 1, HEAD_DIM), lambda b, kv, g: (kv, 0, 0))],
        out_specs=pl.BlockSpec((1, 1, 1, n_blk, HEAD_DIM), lambda b, kv, g: (b, kv, g, 0, 0)),
        out_shape=jax.ShapeDtypeStruct((batch, 2, NSA_KV_GROUPS, n_blk, HEAD_DIM), F32),
        scratch_shapes=[pltpu.VMEM((n_blk, CMP_HIDDEN), F32), pltpu.VMEM((n_blk, CMP_HIDDEN), F32)],
        compiler_params=_params(("arbitrary", "arbitrary", "arbitrary")),
        name="nsa_compress",
    )(p1, pos, w1, w2, gain, flag)


def _nsa_kernel(q_ref, ks_ref, vs_ref, kw_ref, vw_ref, kc_ref, vc_ref, gl_ref, gb_ref,
                bias_ref, ovl_ref, et_ref, o_ref, m_scr, acc_scr, s_scr, p_scr, a_scr, ocmp_scr, imp_scr,
                lc_scr, pcb_scr, split_scr):
    qi = pl.program_id(2)
    tq, rep = NSA_TQ, NSA_REP
    n_cmp = kc_ref.shape[3]

    q2 = q_ref[0]
    q_heads = [q2[:, r * HEAD_DIM:(r + 1) * HEAD_DIM] for r in range(rep)]
    t_col = qi * tq + lax.broadcasted_iota(jnp.int32, (tq, 1), 0)

    def compressed(width):
        cmp_end = lax.broadcasted_iota(jnp.int32, (1, width), 1) * CMP_STRIDE + (CMP_LEN - 1)
        kcb = kc_ref[0, 0, 0, 0:width, :].astype(BF16)
        vcb = vc_ref[0, 0, 0, 0:width, :].astype(BF16)
        for r in range(rep):
            lc_scr[r, :, 0:width] = _nt_dot(q_heads[r], kcb)
        for c in range(tq // CMP_ROWS):
            rows = slice(c * CMP_ROWS, (c + 1) * CMP_ROWS)
            cmask = cmp_end <= t_col[rows]
            pc_sum = jnp.zeros((CMP_ROWS, width), F32)
            for r in range(rep):
                lc = jnp.where(cmask, lc_scr[r, rows, 0:width], NEG_INF)
                mc = jnp.max(lc, axis=-1, keepdims=True)
                pc = jnp.where(cmask, jnp.exp2(lc - mc), 0.0)
                pc = pc / jnp.maximum(jnp.sum(pc, axis=-1, keepdims=True), 1e-30)
                pcb_scr[r, rows, 0:width] = pc.astype(BF16)
                pc_sum = pc_sum + pc
            hi = pc_sum.astype(BF16)
            rem = pc_sum - hi.astype(F32)
            mid = rem.astype(BF16)
            split_scr[0, rows, 0:width] = hi
            split_scr[1, rows, 0:width] = mid
            split_scr[2, rows, 0:width] = (rem - mid.astype(F32)).astype(BF16)
        for r in range(rep):
            ocmp_scr[r] = _dot(pcb_scr[r, :, 0:width], vcb)
        ovl_t = ovl_ref[:, 0:width]
        imp_scr[...] = (_nt_dot(ovl_t, split_scr[0, :, 0:width]) + _nt_dot(ovl_t, split_scr[1, :, 0:width])
                        + _nt_dot(ovl_t, split_scr[2, :, 0:width]))

    n_grp = n_cmp // LANES
    need = jnp.minimum(((qi + 1) * tq - CMP_LEN) // (CMP_STRIDE * LANES) + 1, n_grp)
    for k in range(1, n_grp + 1):
        pl.when(need == k)(functools.partial(compressed, k * LANES))
    o_cmp = [ocmp_scr[r] for r in range(rep)]
    imp_t = imp_scr[...]

    n_rb = tq // NSA_TK
    n_wt = WIN_LEN // NSA_TK + 1
    w_keys = n_wt * NSA_TK
    win_parts = [[] for _ in range(rep)]
    for a in range(n_rb):
        rows = slice(a * NSA_TK, (a + 1) * NSA_TK)
        blk_q = qi * n_rb + a
        j0 = jnp.maximum(blk_q - (n_wt - 1), 0)
        w_rows = pl.ds(pl.multiple_of(j0 * NSA_TK, NSA_TK), w_keys)
        kw_tile = kw_ref[0, w_rows, :]
        vw_aug = jnp.concatenate([vw_ref[0, w_rows, :], jnp.ones((w_keys, HEAD_DIM), BF16)], axis=1)
        dist = t_col[rows] - (j0 * NSA_TK + lax.broadcasted_iota(jnp.int32, (1, w_keys), 1))
        wmask = jnp.where(dist >= 0, dist, WIN_LEN) < WIN_LEN
        tiles = [jnp.clip(blk_q - j0 - i, 0, BIAS_FAR) for i in range(n_wt)]
        for r in range(rep):
            s = _nt_dot(q_heads[r][rows], kw_tile) + jnp.concatenate([bias_ref[t, r] for t in tiles], axis=-1)
            s = jnp.where(wmask, s, NEG_INF)
            p = jnp.exp2(s - jnp.max(s, axis=-1, keepdims=True)).astype(BF16)
            acc = _dot(p, vw_aug)
            win_parts[r].append(acc[:, :HEAD_DIM] / acc[:, HEAD_DIM:])
    o_win = [jnp.concatenate(parts, axis=0) for parts in win_parts]

    blk = lax.broadcasted_iota(jnp.int32, (LANES, 1), 0)
    blk_f = blk.astype(F32)
    t_row = qi * tq + lax.broadcasted_iota(jnp.int32, (1, tq), 1)
    cur = t_row >> (SLC_LEN.bit_length() - 1)
    valid = blk * SLC_LEN <= t_row
    forced = (blk == 0) | (blk == cur) | (blk == cur - 1)
    score = jnp.where(valid, jnp.where(forced, FORCE_SCORE, imp_t), -1.0)
    sel_t = jnp.zeros((LANES, tq), F32)
    for _ in range(SLC_TOPK):
        best = jnp.max(score, axis=0, keepdims=True)
        idx = jnp.min(jnp.where(score == best, blk_f, float(LANES)), axis=0, keepdims=True)
        hit = blk_f == idx
        sel_t = jnp.where(hit, 1.0, sel_t)
        score = jnp.where(hit, -3e38, score)
    unsel = (sel_t.T - 1.0).astype(BF16)
    lhs = [jnp.concatenate([q_heads[r], unsel], axis=1) for r in range(rep)]

    ones_v = jnp.ones((SEL_TK, HEAD_DIM), BF16)

    sub = SEL_TK // NSA_TK
    n_tiles = ((qi + 1) * n_rb - 1) // sub + 1

    def key_rows(jj):
        return pl.ds(pl.multiple_of(jj * SEL_TK, SEL_TK), SEL_TK)

    def logits_stage(jj, heads):
        rows = key_rows(jj)
        rhs = jnp.concatenate([ks_ref[0, rows, :], et_ref[rows, :]], axis=1)
        for r in heads:
            s_scr[r] = _nt_dot(lhs[r], rhs)

    def softmax_stage(jj, heads):
        for a in range(n_rb):
            rows = slice(a * NSA_TK, (a + 1) * NSA_TK)
            u = qi * n_rb + a - sub * jj
            tiles = [jnp.where(u < i, BIAS_FUTURE, jnp.minimum(u - i, BIAS_FAR)) for i in range(sub)]
            for r in heads:
                s = s_scr[r, rows, :] + jnp.concatenate([bias_ref[t, r] for t in tiles], axis=-1)
                m_old = m_scr[r, rows, :]
                m_new = jnp.maximum(m_old, jnp.max(s, axis=-1, keepdims=True))
                a_scr[r, rows, :] = jnp.exp2(m_old - m_new)
                p_scr[r, rows, :] = jnp.exp2(s - jnp.concatenate([m_new] * sub, axis=1)).astype(BF16)
                m_scr[r, rows, :] = m_new

    def pv_stage(jj, heads):
        v_aug = jnp.concatenate([vs_ref[0, key_rows(jj), :], ones_v], axis=1)
        for r in heads:
            alpha = a_scr[r]
            acc_scr[r] = jnp.concatenate([alpha, alpha], axis=1) * acc_scr[r] + _dot(p_scr[r], v_aug)

    all_heads = range(rep)
    m_scr[...] = jnp.full(m_scr.shape, NEG_INF, F32)
    acc_scr[...] = jnp.zeros_like(acc_scr)
    p_scr[...] = jnp.zeros_like(p_scr)
    a_scr[...] = jnp.ones_like(a_scr)
    logits_stage(0, all_heads)

    def sel_step(it, carry):
        pv_stage(jnp.maximum(it - 1, 0), all_heads)
        softmax_stage(it, all_heads)
        logits_stage(jnp.minimum(it + 1, n_tiles - 1), all_heads)
        return carry

    lax.fori_loop(0, n_tiles, sel_step, 0)
    pv_stage(n_tiles - 1, all_heads)

    gates = jax.nn.sigmoid(gl_ref[0] + gb_ref[0])
    o_slc = []
    for r in range(rep):
        acc = acc_scr[r]
        o_slc.append(acc[:, :HEAD_DIM] / acc[:, HEAD_DIM:])
    for r in range(rep):
        c = r * N_BRANCH
        out = (gates[:, c:c + 1] * o_cmp[r] + gates[:, c + 1:c + 2] * o_slc[r]
               + gates[:, c + 2:c + 3] * o_win[r])
        o_ref[0, :, r * HEAD_DIM:(r + 1) * HEAD_DIM] = out


def _nsa(p1, p2, kcv, gate_bias, bias, overlap_t, membership, batch, seq):
    tq = NSA_TQ
    n_cmp = seq // CMP_STRIDE
    kv_blk = (1, seq, LANES)
    grp_w = NSA_REP * HEAD_DIM
    once = pl.Buffered(1)
    return pl.pallas_call(
        _nsa_kernel,
        grid=(batch, NSA_KV_GROUPS, seq // tq),
        in_specs=[pl.BlockSpec((1, tq, grp_w), lambda b, g, i: (b, i, g)),
                  pl.BlockSpec(kv_blk, lambda b, g, i: (b, 0, P2_KS + g), pipeline_mode=once),
                  pl.BlockSpec(kv_blk, lambda b, g, i: (b, 0, P2_VS + g), pipeline_mode=once),
                  pl.BlockSpec(kv_blk, lambda b, g, i: (b, 0, P2_KW + g), pipeline_mode=once),
                  pl.BlockSpec(kv_blk, lambda b, g, i: (b, 0, P2_VW + g), pipeline_mode=once),
                  pl.BlockSpec((1, 1, 1, n_cmp, HEAD_DIM), lambda b, g, i: (b, 0, g, 0, 0)),
                  pl.BlockSpec((1, 1, 1, n_cmp, HEAD_DIM), lambda b, g, i: (b, 1, g, 0, 0)),
                  pl.BlockSpec((1, tq, LANES), lambda b, g, i: (b, i, P1_GL + g)),
                  pl.BlockSpec((1, 1, LANES), lambda b, g, i: (g, 0, 0)),
                  pl.BlockSpec((N_BIAS_TILES, NSA_REP, NSA_TK, NSA_TK), lambda b, g, i: (0, g, 0, 0),
                               pipeline_mode=once),
                  pl.BlockSpec((LANES, n_cmp), lambda b, g, i: (0, 0)),
                  pl.BlockSpec((seq, LANES), lambda b, g, i: (0, 0))],
        out_specs=pl.BlockSpec((1, tq, grp_w), lambda b, g, i: (b, i, g)),
        out_shape=jax.ShapeDtypeStruct((batch, seq, NSA_WIDTH), F32),
        scratch_shapes=[pltpu.VMEM((NSA_REP, tq, LANES), F32),
                        pltpu.VMEM((NSA_REP, tq, 2 * HEAD_DIM), F32),
                        pltpu.VMEM((NSA_REP, tq, SEL_TK), F32),
                        pltpu.VMEM((NSA_REP, tq, SEL_TK), BF16),
                        pltpu.VMEM((NSA_REP, tq, LANES), F32),
                        pltpu.VMEM((NSA_REP, tq, HEAD_DIM), F32),
                        pltpu.VMEM((LANES, tq), F32),
                        pltpu.VMEM((NSA_REP, tq, n_cmp), F32),
                        pltpu.VMEM((NSA_REP, tq, n_cmp), BF16),
                        pltpu.VMEM((3, tq, n_cmp), BF16)],
        compiler_params=_params(("arbitrary", "arbitrary", "arbitrary")),
        name="nsa_attn",
    )(p2, p2, p2, p2, p2, kcv, kcv, p1, gate_bias, bias, overlap_t, membership)


def _out_kernel(x_ref, oa_ref, on_ref, ga_ref, gn_ref, w_ref, o_ref):
    def norm(t, g):
        ms = jnp.mean(t * t, axis=-1, keepdims=True)
        return (t * lax.rsqrt(ms + EPS) * g).astype(BF16)

    ya = norm(oa_ref[...], ga_ref[...])
    yn = norm(on_ref[...], gn_ref[...])
    y = _dot(ya, w_ref[0:A_WIDTH, :]) + _dot(yn, w_ref[A_WIDTH:A_WIDTH + NSA_WIDTH, :])
    o_ref[...] = x_ref[...] + y


def _out_proj(x2, oa, on, gain, w, *, tm):
    n_tok, d = x2.shape
    return pl.pallas_call(
        _out_kernel,
        grid=(n_tok // tm,),
        in_specs=[pl.BlockSpec((tm, d), lambda i: (i, 0)),
                  pl.BlockSpec((tm, A_WIDTH), lambda i: (i, 0)),
                  pl.BlockSpec((tm, NSA_WIDTH), lambda i: (i, 0)),
                  pl.BlockSpec((1, A_WIDTH), lambda i: (0, 0)),
                  pl.BlockSpec((1, NSA_WIDTH), lambda i: (0, 0)),
                  pl.BlockSpec((d, d), lambda i: (0, 0))],
        out_specs=pl.BlockSpec((tm, d), lambda i: (i, 0)),
        out_shape=jax.ShapeDtypeStruct((n_tok, d), F32),
        compiler_params=_params(("arbitrary",)),
        name="out_proj",
    )(x2, oa, on, gain[:A_WIDTH].reshape(1, A_WIDTH), gain[A_WIDTH:].reshape(1, NSA_WIDTH), w)


def _pack_w_in(w_in, gate_bias, q_norm_a, k_norm_a, q_norm_nsa, k_norm_nsa):
    scale = HEAD_DIM ** -0.5
    sizes = (A_WIDTH,) * 3 + (NSA_WIDTH,) + (KV_WIDTH,) * 6 + (N_BRANCH * NSA_HEADS,)
    offs = np.concatenate([[0], np.cumsum(sizes)])
    qa, ka, va, qn, kc, vc, k_s, v_s, k_w, v_w, gl = [w_in[:, offs[i]:offs[i + 1]] for i in range(11)]
    per_grp = NSA_REP * N_BRANCH
    pad = jnp.zeros((w_in.shape[0], LANES - per_grp), w_in.dtype)
    gl_cols = [jnp.concatenate([gl[:, g * per_grp:(g + 1) * per_grp], pad], axis=1) for g in range(NSA_KV_GROUPS)]
    n_pad = P1_HEADS - P1_USED
    w_pad = jnp.zeros((w_in.shape[0], n_pad * LANES), w_in.dtype)
    w = jnp.concatenate([qa, ka, va, kc, vc] + gl_cols + [w_pad, qn, k_s, v_s, k_w, v_w], axis=1).astype(BF16)

    ones = jnp.ones((HEAD_DIM,), F32)
    tile = lambda v, n: jnp.tile(v, n)
    gain = jnp.concatenate([tile(q_norm_a * (scale * LOG2E), A_HEADS), tile(k_norm_a, A_HEADS),
                            tile(ones, A_HEADS + 6 + n_pad),
                            tile(q_norm_nsa * (scale * LOG2E), NSA_HEADS), tile(k_norm_nsa, 2), tile(ones, 2),
                            tile(k_norm_nsa, 2), tile(ones, 2)])
    flag = jnp.concatenate([jnp.ones((2 * A_WIDTH,), F32), jnp.zeros((A_WIDTH + (6 + n_pad) * HEAD_DIM,), F32),
                            jnp.ones((NSA_WIDTH + KV_WIDTH,), F32), jnp.zeros((KV_WIDTH,), F32),
                            jnp.ones((KV_WIDTH,), F32), jnp.zeros((KV_WIDTH,), F32)])
    gb = jnp.stack([jnp.concatenate([gate_bias[g * per_grp:(g + 1) * per_grp],
                                     jnp.zeros((LANES - per_grp,), F32)]) for g in range(NSA_KV_GROUPS)])
    return w, gain, flag, gb.reshape(NSA_KV_GROUPS, 1, LANES)


def _forward(x, ffn1_norm, ffn1_w1, ffn1_w3, ffn1_w2, mix_norm, w_in, gate_bias, q_norm_a, k_norm_a,
             q_norm_nsa, k_norm_nsa, cmp_pos, cmp_k_w1, cmp_k_w2, cmp_v_w1, cmp_v_w2, out_norm, w_out,
             ffn2_norm, ffn2_w1, ffn2_w3, ffn2_w2, rel_bias, *, tm_ffn, tf_ffn, tm_proj, tm_out):
    batch, seq, d = x.shape
    depth = ffn1_w1.shape[0]
    n_tok = batch * seq
    assert seq % DIL_SPAN == 0 and seq // SLC_LEN <= LANES
    assert seq % NSA_TQ == 0 and seq % SEL_TK == 0 and (seq // CMP_STRIDE) % LANES == 0
    assert n_tok % max(tm_ffn, tm_proj, tm_out) == 0 and d == D_MODEL

    bias_a = _expand_bias(rel_bias[:, :A_HEADS], jnp.asarray(_dilated_buckets()), scale=LOG2E)
    bias_a = bias_a.reshape(3, 2, A_HEADS, BAND, 2 * BAND)
    bias_n = _expand_bias(rel_bias[:, A_HEADS:], jnp.asarray(_nsa_buckets()),
                          scale=LOG2E, shift_far=True)
    overlap_t = jnp.asarray(_overlap_t_np(seq // CMP_STRIDE)).astype(BF16)
    membership = jnp.asarray(_block_membership_np(seq)).astype(BF16)

    x2 = x.reshape(n_tok, d)
    for l in range(depth):
        x2 = _ffn(x2, ffn1_norm[l], ffn1_w1[l].astype(BF16), ffn1_w3[l].astype(BF16),
                  ffn1_w2[l].astype(BF16), tm=tm_ffn, tf=tf_ffn)

        w_p, g_p, f_p, gb = _pack_w_in(w_in[l], gate_bias[l], q_norm_a[l], k_norm_a[l],
                                       q_norm_nsa[l], k_norm_nsa[l])
        p1, p2 = _proj(x2, mix_norm[l], w_p, g_p, f_p, P1_HEADS * LANES, tm=tm_proj, tn=1024)
        p1 = p1.reshape(batch, seq, P1_HEADS * LANES)
        p2 = p2.reshape(batch, seq, P2_HEADS * LANES)

        o_a = _dilated(p1, bias_a, batch, seq)

        cmp_w1 = jnp.stack([cmp_k_w1[l], cmp_v_w1[l]]).astype(BF16)
        cmp_w2 = jnp.stack([cmp_k_w2[l], cmp_v_w2[l]]).astype(BF16)
        cmp_gain = jnp.stack([k_norm_nsa[l], jnp.ones((HEAD_DIM,), F32)]).reshape(2, 1, HEAD_DIM)
        cmp_flag = jnp.stack([jnp.ones((HEAD_DIM,), F32), jnp.zeros((HEAD_DIM,), F32)]).reshape(2, 1, HEAD_DIM)
        kcv = _compress(p1, cmp_pos[l], cmp_w1, cmp_w2, cmp_gain, cmp_flag, batch, seq)

        o_n = _nsa(p1, p2, kcv, gb, bias_n, overlap_t, membership, batch, seq)

        x2 = _out_proj(x2, o_a.reshape(n_tok, A_WIDTH), o_n.reshape(n_tok, NSA_WIDTH), out_norm[l],
                       w_out[l].astype(BF16), tm=tm_out)

        x2 = _ffn(x2, ffn2_norm[l], ffn2_w1[l].astype(BF16), ffn2_w3[l].astype(BF16),
                  ffn2_w2[l].astype(BF16), tm=tm_ffn, tf=tf_ffn)
    return x2.reshape(batch, seq, d)


def kernel(x, ffn1_norm, ffn1_w1, ffn1_w3, ffn1_w2, mix_norm, w_in, gate_bias, q_norm_a, k_norm_a, q_norm_nsa, k_norm_nsa, cmp_pos, cmp_k_w1, cmp_k_w2, cmp_v_w1, cmp_v_w2, out_norm, w_out, ffn2_norm, ffn2_w1, ffn2_w3, ffn2_w2, rel_bias):
    return _forward(x, ffn1_norm, ffn1_w1, ffn1_w3, ffn1_w2, mix_norm, w_in, gate_bias, q_norm_a, k_norm_a,
                    q_norm_nsa, k_norm_nsa, cmp_pos, cmp_k_w1, cmp_k_w2, cmp_v_w1, cmp_v_w2, out_norm, w_out,
                    ffn2_norm, ffn2_w1, ffn2_w3, ffn2_w2, rel_bias,
                    tm_ffn=1024, tf_ffn=256, tm_proj=1024, tm_out=512)
```

```python
import functools
import math

import numpy as np
import jax
import jax.numpy as jnp
from jax import lax
from jax.experimental import pallas as pl
from jax.experimental.pallas import tpu as pltpu

D_MODEL = 2048
HEAD_DIM = 128
N_HEADS = 16
A_HEADS = 8
NSA_HEADS = 8
NSA_KV_GROUPS = 2
NSA_REP = 4
A_WIDTH = A_HEADS * HEAD_DIM
NSA_WIDTH = NSA_HEADS * HEAD_DIM
KV_WIDTH = NSA_KV_GROUPS * HEAD_DIM
N_BRANCH = 3
D_FF = 5632
DILATED_PATTERNS = ((128, 1), (512, 4), (2048, 16))
BAND = 128
CMP_STRIDE = 16
CMP_LEN = 32
CMP_HIDDEN = 512
SLC_LEN = 64
SLC_TOPK = 16
WIN_LEN = 512
N_BUCKETS = 32
MAX_DISTANCE = 2048
FORCE_SCORE = 1e6
NEG_INF = -1e30
EPS = 1e-6
LOG2E = 1.4426950408889634

LANES = 128
DIL_SPAN = 2048
NSA_TQ = 512
NSA_TK = 128
SEL_TK = 512
CMP_ROWS = 32
BIAS_FAR = 13
BIAS_FUTURE = 14
N_BIAS_TILES = 15
MASK_BIG = 1e30
VMEM_LIMIT = 56 * 1024 * 1024

F32 = jnp.float32
BF16 = jnp.bfloat16

P1_QA, P1_KA, P1_VA, P1_KC, P1_VC, P1_GL = 0, 8, 16, 24, 26, 28
P1_USED = 30
P1_HEADS = 32
P2_QN, P2_KS, P2_VS, P2_KW, P2_VW = 0, 8, 10, 12, 14
P2_HEADS = 16


def _params(sem, flags=None):
    return pltpu.CompilerParams(dimension_semantics=sem, vmem_limit_bytes=VMEM_LIMIT, flags=flags)


def _nt_dot(a, b):
    return lax.dot_general(a, b, (((1,), (1,)), ((), ())), preferred_element_type=F32)


def _dot(a, b):
    return jnp.dot(a, b, preferred_element_type=F32)


def _rel_bucket_np(dist):
    n = np.maximum(dist, 0)
    max_exact = N_BUCKETS // 2
    nf = np.maximum(n, 1).astype(np.float32)
    log_b = max_exact + (np.log(nf / np.float32(max_exact)) / np.float32(math.log(MAX_DISTANCE / max_exact))
                         * np.float32(N_BUCKETS - max_exact)).astype(np.int32)
    return np.where(n < max_exact, n, np.minimum(log_b, N_BUCKETS - 1)).astype(np.int32)


def _dilated_buckets():
    i = np.arange(BAND)[:, None]
    j = np.arange(2 * BAND)[None, :] - BAND
    diff = i - j
    out = []
    for window, dilation in DILATED_PATTERNS:
        mask = (diff >= 0) & (diff <= window // dilation)
        bkt = _rel_bucket_np(diff * dilation)
        out.append(np.where(mask, bkt, -1))
        out.append(np.where(mask & (j >= 0), bkt, -1))
    return np.stack(out).astype(np.int32)


def _nsa_buckets():
    i = np.arange(NSA_TK)[:, None]
    j = np.arange(NSA_TK)[None, :]
    tiles = []
    for m in range(BIAS_FAR + 1):
        dist = NSA_TK * m + i - j
        tiles.append(np.where(dist >= 0, _rel_bucket_np(dist), -1))
    assert (tiles[BIAS_FAR] == N_BUCKETS - 1).all()
    tiles.append(np.full((NSA_TK, NSA_TK), -1))
    assert len(tiles) == N_BIAS_TILES and BIAS_FUTURE == N_BIAS_TILES - 1
    return np.stack(tiles).astype(np.int32)


def _overlap_t_np(n_cmp_pad):
    c_start = np.arange(n_cmp_pad)[None, :] * CMP_STRIDE
    s_start = np.arange(LANES)[:, None] * SLC_LEN
    return ((c_start < s_start + SLC_LEN) & (c_start + CMP_LEN > s_start)).astype(np.float32)


def _block_membership_np(seq):
    k_blk = np.arange(seq)[:, None] // SLC_LEN
    return np.where(k_blk == np.arange(LANES)[None, :], MASK_BIG, 0.0).astype(np.float32)


def _bias_kernel(tbl_ref, bkt_ref, o_ref, *, n_heads, scale, shift_far):
    bkt = bkt_ref[0]
    for h in range(n_heads):
        acc = jnp.full(bkt.shape, NEG_INF, F32)
        base = tbl_ref[N_BUCKETS - 1, h] if shift_far else 0.0
        for b in range(N_BUCKETS):
            acc = jnp.where(bkt == b, (tbl_ref[b, h] - base) * scale, acc)
        o_ref[0, h] = acc


def _expand_bias(table, buckets, scale=1.0, shift_far=False):
    n_t, rows, width = buckets.shape
    nh = table.shape[1]
    return pl.pallas_call(
        functools.partial(_bias_kernel, n_heads=nh, scale=scale, shift_far=shift_far),
        grid=(n_t,),
        in_specs=[pl.BlockSpec(memory_space=pltpu.SMEM),
                  pl.BlockSpec((1, rows, width), lambda t: (t, 0, 0))],
        out_specs=pl.BlockSpec((1, nh, rows, width), lambda t: (t, 0, 0, 0)),
        out_shape=jax.ShapeDtypeStruct((n_t, nh, rows, width), F32),
        compiler_params=_params(("arbitrary",)),
        name="bias_expand",
    )(table, buckets)


def _ffn_kernel(x_ref, g_ref, w1_ref, w3_ref, w2_ref, o_ref, n_scr):
    j = pl.program_id(1)

    @pl.when(j == 0)
    def _():
        x = x_ref[...]
        ms = jnp.mean(x * x, axis=-1, keepdims=True)
        n_scr[...] = (x * lax.rsqrt(ms + EPS) * g_ref[...]).astype(BF16)
        o_ref[...] = jnp.zeros_like(o_ref)

    n = n_scr[...]
    h1 = _dot(n, w1_ref[...])
    h3 = _dot(n, w3_ref[...])
    a = (h1 * jax.nn.sigmoid(h1) * h3).astype(BF16)
    o_ref[...] += _dot(a, w2_ref[...])

    @pl.when(j == pl.num_programs(1) - 1)
    def _():
        o_ref[...] = x_ref[...] + 0.5 * o_ref[...]


def _ffn(x2, gain, w1, w3, w2, *, tm, tf):
    n_tok, d = x2.shape
    f = w1.shape[1]
    return pl.pallas_call(
        _ffn_kernel,
        grid=(n_tok // tm, f // tf),
        in_specs=[pl.BlockSpec((tm, d), lambda i, j: (i, 0)),
                  pl.BlockSpec((1, d), lambda i, j: (0, 0)),
                  pl.BlockSpec((d, tf), lambda i, j: (0, j)),
                  pl.BlockSpec((d, tf), lambda i, j: (0, j)),
                  pl.BlockSpec((tf, d), lambda i, j: (j, 0))],
        out_specs=pl.BlockSpec((tm, d), lambda i, j: (i, 0)),
        out_shape=jax.ShapeDtypeStruct((n_tok, d), F32),
        scratch_shapes=[pltpu.VMEM((tm, d), BF16)],
        compiler_params=_params(("arbitrary", "arbitrary")),
        name="ffn",
    )(x2, gain.reshape(1, d), w1, w3, w2)


def _proj_kernel(x_ref, g_ref, w_ref, cg_ref, cf_ref, o1_ref, o2_ref, n_scr, acc_scr, *, n1):
    j = pl.program_id(1)
    last = pl.num_programs(1) - 1

    def matmul():
        acc_scr[...] = _dot(n_scr[...], w_ref[...])

    def epilogue(o_ref):
        for h in range(acc_scr.shape[1] // LANES):
            cols = slice(h * LANES, (h + 1) * LANES)
            a = acc_scr[:, cols]
            ms = jnp.mean(a * a, axis=-1, keepdims=True)
            inv = jnp.where(cf_ref[:, cols] > 0.0, lax.rsqrt(ms + EPS), 1.0)
            o_ref[:, cols] = (a * inv * cg_ref[:, cols]).astype(o_ref.dtype)

    @pl.when(j == 0)
    def _():
        x = x_ref[...]
        ms = jnp.mean(x * x, axis=-1, keepdims=True)
        n_scr[...] = (x * lax.rsqrt(ms + EPS) * g_ref[...]).astype(BF16)
        matmul()

    @pl.when(jnp.logical_and(j > 0, j <= n1))
    def _():
        epilogue(o1_ref)
        matmul()

    @pl.when(jnp.logical_and(j > n1, j < last))
    def _():
        epilogue(o2_ref)
        matmul()

    @pl.when(j == last)
    def _():
        epilogue(o2_ref)


def _proj(x2, gain, w, col_gain, col_flag, cols1, *, tm, tn):
    n_tok, d = x2.shape
    n_cols = w.shape[1]
    n1 = cols1 // tn
    n2 = (n_cols - cols1) // tn
    assert n1 * tn == cols1 and (n1 + n2) * tn == n_cols
    n_j = n1 + n2
    prev = lambda j: jnp.maximum(j - 1, 0)
    return pl.pallas_call(
        functools.partial(_proj_kernel, n1=n1),
        grid=(n_tok // tm, n_j + 1),
        in_specs=[pl.BlockSpec((tm, d), lambda i, j: (i, 0)),
                  pl.BlockSpec((1, d), lambda i, j: (0, 0)),
                  pl.BlockSpec((d, tn), lambda i, j: (0, jnp.minimum(j, n_j - 1))),
                  pl.BlockSpec((1, tn), lambda i, j: (0, prev(j))),
                  pl.BlockSpec((1, tn), lambda i, j: (0, prev(j)))],
        out_specs=[pl.BlockSpec((tm, tn), lambda i, j: (i, jnp.minimum(prev(j), n1 - 1))),
                   pl.BlockSpec((tm, tn), lambda i, j: (i, jnp.maximum(prev(j) - n1, 0)))],
        out_shape=[jax.ShapeDtypeStruct((n_tok, cols1), F32),
                   jax.ShapeDtypeStruct((n_tok, n_cols - cols1), BF16)],
        scratch_shapes=[pltpu.VMEM((tm, d), BF16), pltpu.VMEM((tm, tn), F32)],
        compiler_params=_params(("arbitrary", "arbitrary")),
        name="in_proj",
    )(x2, gain.reshape(1, d), w, col_gain.reshape(1, n_cols), col_flag.reshape(1, n_cols))


def _dilated_kernel(q_ref, kc_ref, kp_ref, vc_ref, vp_ref, bias_ref, o_ref,
                    k_scr, v_scr, out_scr, lse_scr):
    n = pl.program_id(2)
    k_scr[0:DIL_SPAN, :] = kp_ref[0]
    k_scr[DIL_SPAN:2 * DIL_SPAN, :] = kc_ref[0]
    v_scr[0:DIL_SPAN, :] = vp_ref[0]
    v_scr[DIL_SPAN:2 * DIL_SPAN, :] = vc_ref[0]

    for p, (_, dil) in enumerate(DILATED_PATTERNS):
        span = BAND * dil

        def unit(u, carry, p=p, dil=dil, span=span):
            s = u // dil
            r = u - s * dil
            base = s * span + r
            first = jnp.logical_and(n == 0, s == 0).astype(jnp.int32)
            q = q_ref[0, pl.ds(base, BAND, stride=dil), :].astype(BF16)
            kk = k_scr[pl.ds(DIL_SPAN + base - span, 2 * BAND, stride=dil), :].astype(BF16)
            vv = v_scr[pl.ds(DIL_SPAN + base - span, 2 * BAND, stride=dil), :].astype(BF16)
            lg = _nt_dot(q, kk) + bias_ref[p, first, 0]
            m = jnp.max(lg, axis=-1, keepdims=True)
            e = jnp.exp2(lg - m).astype(BF16)
            acc = _dot(e, jnp.concatenate([vv, jnp.ones((2 * BAND, LANES), BF16)], axis=1))
            ssum = acc[:, LANES:]
            out_scr[p, pl.ds(base, BAND, stride=dil), :] = acc[:, :LANES] / ssum
            lse_scr[p, pl.ds(base, BAND, stride=dil), :] = m + jnp.log2(ssum)
            return carry

        lax.fori_loop(0, DIL_SPAN // BAND, unit, 0, unroll=8)

    chunk = 256

    def combine(c, carry):
        rows = pl.ds(pl.multiple_of(c * chunk, chunk), chunk)
        l0, l1, l2 = lse_scr[0, rows, :], lse_scr[1, rows, :], lse_scr[2, rows, :]
        big = jnp.maximum(jnp.maximum(l0, l1), l2)
        e0, e1, e2 = jnp.exp2(l0 - big), jnp.exp2(l1 - big), jnp.exp2(l2 - big)
        num = e0 * out_scr[0, rows, :] + e1 * out_scr[1, rows, :] + e2 * out_scr[2, rows, :]
        o_ref[0, rows, :] = num / (e0 + e1 + e2)
        return carry

    lax.fori_loop(0, DIL_SPAN // chunk, combine, 0)


def _dilated(p1, bias, batch, seq):
    blk = (1, DIL_SPAN, LANES)
    prev = lambda nn: jnp.maximum(nn - 1, 0)
    return pl.pallas_call(
        _dilated_kernel,
        grid=(batch, A_HEADS, seq // DIL_SPAN),
        in_specs=[pl.BlockSpec(blk, lambda b, h, nn: (b, nn, P1_QA + h)),
                  pl.BlockSpec(blk, lambda b, h, nn: (b, nn, P1_KA + h)),
                  pl.BlockSpec(blk, lambda b, h, nn: (b, prev(nn), P1_KA + h)),
                  pl.BlockSpec(blk, lambda b, h, nn: (b, nn, P1_VA + h)),
                  pl.BlockSpec(blk, lambda b, h, nn: (b, prev(nn), P1_VA + h)),
                  pl.BlockSpec((3, 2, 1, BAND, 2 * BAND), lambda b, h, nn: (0, 0, h, 0, 0))],
        out_specs=pl.BlockSpec(blk, lambda b, h, nn: (b, nn, h)),
        out_shape=jax.ShapeDtypeStruct((batch, seq, A_WIDTH), F32),
        scratch_shapes=[pltpu.VMEM((2 * DIL_SPAN, LANES), F32),
                        pltpu.VMEM((2 * DIL_SPAN, LANES), F32),
                        pltpu.VMEM((3, DIL_SPAN, LANES), F32),
                        pltpu.VMEM((3, DIL_SPAN, LANES), F32)],
        compiler_params=_params(("arbitrary", "arbitrary", "arbitrary")),
        name="dilated_attn",
    )(p1, p1, p1, p1, p1, bias)


def _compress_kernel(t_ref, pos_ref, w1_ref, w2_ref, gain_ref, flag_ref, o_ref, lo_scr, hi_scr):
    n_blk = o_ref.shape[3]
    lo_scr[...] = jnp.zeros_like(lo_scr)
    hi_scr[...] = jnp.zeros_like(hi_scr)
    for i in range(CMP_STRIDE):
        rows = t_ref[0, pl.ds(i, n_blk, stride=CMP_STRIDE), :]
        a = (rows + pos_ref[i:i + 1, :]).astype(BF16)
        lo_scr[...] += _dot(a, w1_ref[0, i * HEAD_DIM:(i + 1) * HEAD_DIM, :])
        b = (rows + pos_ref[CMP_STRIDE + i:CMP_STRIDE + i + 1, :]).astype(BF16)
        hi_scr[...] += _dot(b, w1_ref[0, (CMP_STRIDE + i) * HEAD_DIM:(CMP_STRIDE + i + 1) * HEAD_DIM, :])
    hid = lo_scr[...] + pltpu.roll(hi_scr[...], n_blk - 1, 0)
    o = _dot(jax.nn.gelu(hid).astype(BF16), w2_ref[0])
    ms = jnp.mean(o * o, axis=-1, keepdims=True)
    inv = jnp.where(flag_ref[0] > 0.0, lax.rsqrt(ms + EPS), 1.0)
    o_ref[0, 0, 0] = o * inv * gain_ref[0]


def _compress(p1, pos, w1, w2, gain, flag, batch, seq):
    n_blk = seq // CMP_STRIDE
    return pl.pallas_call(
        _compress_kernel,
        grid=(batch, 2, NSA_KV_GROUPS),
        in_specs=[pl.BlockSpec((1, seq, LANES), lambda b, kv, g: (b, 0, P1_KC + 2 * kv + g)),
                  pl.BlockSpec((CMP_LEN, HEAD_DIM), lambda b, kv, g: (0, 0)),
                  pl.BlockSpec((1, CMP_LEN * HEAD_DIM, CMP_HIDDEN), lambda b, kv, g: (kv, 0, 0)),
                  pl.BlockSpec((1, CMP_HIDDEN, HEAD_DIM), lambda b, kv, g: (kv, 0, 0)),
                  pl.BlockSpec((1, 1, HEAD_DIM), lambda b, kv, g: (kv, 0, 0)),
                  pl.BlockSpec((1, 1, HEAD_DIM), lambda b, kv, g: (kv, 0, 0))],
        out_specs=pl.BlockSpec((1, 1, 1, n_blk, HEAD_DIM), lambda b, kv, g: (b, kv, g, 0, 0)),
        out_shape=jax.ShapeDtypeStruct((batch, 2, NSA_KV_GROUPS, n_blk, HEAD_DIM), F32),
        scratch_shapes=[pltpu.VMEM((n_blk, CMP_HIDDEN), F32), pltpu.VMEM((n_blk, CMP_HIDDEN), F32)],
        compiler_params=_params(("arbitrary", "arbitrary", "arbitrary")),
        name="nsa_compress",
    )(p1, pos, w1, w2, gain, flag)


def _nsa_kernel(q_ref, ks_ref, vs_ref, kw_ref, vw_ref, kc_ref, vc_ref, gl_ref, gb_ref,
                bias_ref, ovl_ref, et_ref, o_ref, m_scr, acc_scr, s_scr, p_scr, a_scr, ocmp_scr, imp_scr,
                lc_scr, pcb_scr, split_scr):
    qi = pl.program_id(2)
    tq, rep = NSA_TQ, NSA_REP
    n_cmp = kc_ref.shape[3]

    q2 = q_ref[0]
    q_heads = [q2[:, r * HEAD_DIM:(r + 1) * HEAD_DIM] for r in range(rep)]
    t_col = qi * tq + lax.broadcasted_iota(jnp.int32, (tq, 1), 0)

    def compressed(width):
        cmp_end = lax.broadcasted_iota(jnp.int32, (1, width), 1) * CMP_STRIDE + (CMP_LEN - 1)
        kcb = kc_ref[0, 0, 0, 0:width, :].astype(BF16)
        vcb = vc_ref[0, 0, 0, 0:width, :].astype(BF16)
        for r in range(rep):
            lc_scr[r, :, 0:width] = _nt_dot(q_heads[r], kcb)
        for c in range(tq // CMP_ROWS):
            rows = slice(c * CMP_ROWS, (c + 1) * CMP_ROWS)
            cmask = cmp_end <= t_col[rows]
            pc_sum = jnp.zeros((CMP_ROWS, width), F32)
            for r in range(rep):
                lc = jnp.where(cmask, lc_scr[r, rows, 0:width], NEG_INF)
                mc = jnp.max(lc, axis=-1, keepdims=True)
                pc = jnp.where(cmask, jnp.exp2(lc - mc), 0.0)
                pc = pc / jnp.maximum(jnp.sum(pc, axis=-1, keepdims=True), 1e-30)
                pcb_scr[r, rows, 0:width] = pc.astype(BF16)
                pc_sum = pc_sum + pc
            hi = pc_sum.astype(BF16)
            rem = pc_sum - hi.astype(F32)
            mid = rem.astype(BF16)
            split_scr[0, rows, 0:width] = hi
            split_scr[1, rows, 0:width] = mid
            split_scr[2, rows, 0:width] = (rem - mid.astype(F32)).astype(BF16)
        for r in range(rep):
            ocmp_scr[r] = _dot(pcb_scr[r, :, 0:width], vcb)
        ovl_t = ovl_ref[:, 0:width]
        imp_scr[...] = (_nt_dot(ovl_t, split_scr[0, :, 0:width]) + _nt_dot(ovl_t, split_scr[1, :, 0:width])
                        + _nt_dot(ovl_t, split_scr[2, :, 0:width]))

    n_grp = n_cmp // LANES
    need = jnp.minimum(((qi + 1) * tq - CMP_LEN) // (CMP_STRIDE * LANES) + 1, n_grp)
    for k in range(1, n_grp + 1):
        pl.when(need == k)(functools.partial(compressed, k * LANES))
    o_cmp = [ocmp_scr[r] for r in range(rep)]
    imp_t = imp_scr[...]

    n_rb = tq // NSA_TK
    n_wt = WIN_LEN // NSA_TK + 1
    w_keys = n_wt * NSA_TK
    win_parts = [[] for _ in range(rep)]
    for a in range(n_rb):
        rows = slice(a * NSA_TK, (a + 1) * NSA_TK)
        blk_q = qi * n_rb + a
        j0 = jnp.maximum(blk_q - (n_wt - 1), 0)
        w_rows = pl.ds(pl.multiple_of(j0 * NSA_TK, NSA_TK), w_keys)
        kw_tile = kw_ref[0, w_rows, :]
        vw_aug = jnp.concatenate([vw_ref[0, w_rows, :], jnp.ones((w_keys, HEAD_DIM), BF16)], axis=1)
        dist = t_col[rows] - (j0 * NSA_TK + lax.broadcasted_iota(jnp.int32, (1, w_keys), 1))
        wmask = jnp.where(dist >= 0, dist, WIN_LEN) < WIN_LEN
        tiles = [jnp.clip(blk_q - j0 - i, 0, BIAS_FAR) for i in range(n_wt)]
        for r in range(rep):
            s = _nt_dot(q_heads[r][rows], kw_tile) + jnp.concatenate([bias_ref[t, r] for t in tiles], axis=-1)
            s = jnp.where(wmask, s, NEG_INF)
            p = jnp.exp2(s - jnp.max(s, axis=-1, keepdims=True)).astype(BF16)
            acc = _dot(p, vw_aug)
            win_parts[r].append(acc[:, :HEAD_DIM] / acc[:, HEAD_DIM:])
    o_win = [jnp.concatenate(parts, axis=0) for parts in win_parts]

    blk = lax.broadcasted_iota(jnp.int32, (LANES, 1), 0)
    blk_f = blk.astype(F32)
    t_row = qi * tq + lax.broadcasted_iota(jnp.int32, (1, tq), 1)
    cur = t_row >> (SLC_LEN.bit_length() - 1)
    valid = blk * SLC_LEN <= t_row
    forced = (blk == 0) | (blk == cur) | (blk == cur - 1)
    score = jnp.where(valid, jnp.where(forced, FORCE_SCORE, imp_t), -1.0)
    sel_t = jnp.zeros((LANES, tq), F32)
    for _ in range(SLC_TOPK):
        best = jnp.max(score, axis=0, keepdims=True)
        idx = jnp.min(jnp.where(score == best, blk_f, float(LANES)), axis=0, keepdims=True)
        hit = blk_f == idx
        sel_t = jnp.where(hit, 1.0, sel_t)
        score = jnp.where(hit, -3e38, score)
    unsel = (sel_t.T - 1.0).astype(BF16)
    lhs = [jnp.concatenate([q_heads[r], unsel], axis=1) for r in range(rep)]

    ones_v = jnp.ones((SEL_TK, HEAD_DIM), BF16)

    sub = SEL_TK // NSA_TK
    n_tiles = ((qi + 1) * n_rb - 1) // sub + 1

    def key_rows(jj):
        return pl.ds(pl.multiple_of(jj * SEL_TK, SEL_TK), SEL_TK)

    def logits_stage(jj, heads):
        rows = key_rows(jj)
        rhs = jnp.concatenate([ks_ref[0, rows, :], et_ref[rows, :]], axis=1)
        for r in heads:
            s_scr[r] = _nt_dot(lhs[r], rhs)

    def softmax_stage(jj, heads):
        for a in range(n_rb):
            rows = slice(a * NSA_TK, (a + 1) * NSA_TK)
            u = qi * n_rb + a - sub * jj
            tiles = [jnp.where(u < i, BIAS_FUTURE, jnp.minimum(u - i, BIAS_FAR)) for i in range(sub)]
            for r in heads:
                s = s_scr[r, rows, :] + jnp.concatenate([bias_ref[t, r] for t in tiles], axis=-1)
                m_old = m_scr[r, rows, :]
                m_new = jnp.maximum(m_old, jnp.max(s, axis=-1, keepdims=True))
                a_scr[r, rows, :] = jnp.exp2(m_old - m_new)
                p_scr[r, rows, :] = jnp.exp2(s - jnp.concatenate([m_new] * sub, axis=1)).astype(BF16)
                m_scr[r, rows, :] = m_new

    def pv_stage(jj, heads):
        v_aug = jnp.concatenate([vs_ref[0, key_rows(jj), :], ones_v], axis=1)
        for r in heads:
            alpha = a_scr[r]
            acc_scr[r] = jnp.concatenate([alpha, alpha], axis=1) * acc_scr[r] + _dot(p_scr[r], v_aug)

    all_heads = range(rep)
    m_scr[...] = jnp.full(m_scr.shape, NEG_INF, F32)
    acc_scr[...] = jnp.zeros_like(acc_scr)
    p_scr[...] = jnp.zeros_like(p_scr)
    a_scr[...] = jnp.ones_like(a_scr)
    logits_stage(0, all_heads)

    def sel_step(it, carry):
        pv_stage(jnp.maximum(it - 1, 0), all_heads)
        softmax_stage(it, all_heads)
        logits_stage(jnp.minimum(it + 1, n_tiles - 1), all_heads)
        return carry

    lax.fori_loop(0, n_tiles, sel_step, 0)
    pv_stage(n_tiles - 1, all_heads)

    gates = jax.nn.sigmoid(gl_ref[0] + gb_ref[0])
    o_slc = []
    for r in range(rep):
        acc = acc_scr[r]
        o_slc.append(acc[:, :HEAD_DIM] / acc[:, HEAD_DIM:])
    for r in range(rep):
        c = r * N_BRANCH
        out = (gates[:, c:c + 1] * o_cmp[r] + gates[:, c + 1:c + 2] * o_slc[r]
               + gates[:, c + 2:c + 3] * o_win[r])
        o_ref[0, :, r * HEAD_DIM:(r + 1) * HEAD_DIM] = out


def _nsa(p1, p2, kcv, gate_bias, bias, overlap_t, membership, batch, seq):
    tq = NSA_TQ
    n_cmp = seq // CMP_STRIDE
    kv_blk = (1, seq, LANES)
    grp_w = NSA_REP * HEAD_DIM
    once = pl.Buffered(1)
    return pl.pallas_call(
        _nsa_kernel,
        grid=(batch, NSA_KV_GROUPS, seq // tq),
        in_specs=[pl.BlockSpec((1, tq, grp_w), lambda b, g, i: (b, i, g)),
                  pl.BlockSpec(kv_blk, lambda b, g, i: (b, 0, P2_KS + g), pipeline_mode=once),
                  pl.BlockSpec(kv_blk, lambda b, g, i: (b, 0, P2_VS + g), pipeline_mode=once),
                  pl.BlockSpec(kv_blk, lambda b, g, i: (b, 0, P2_KW + g), pipeline_mode=once),
                  pl.BlockSpec(kv_blk, lambda b, g, i: (b, 0, P2_VW + g), pipeline_mode=once),
                  pl.BlockSpec((1, 1, 1, n_cmp, HEAD_DIM), lambda b, g, i: (b, 0, g, 0, 0)),
                  pl.BlockSpec((1, 1, 1, n_cmp, HEAD_DIM), lambda b, g, i: (b, 1, g, 0, 0)),
                  pl.BlockSpec((1, tq, LANES), lambda b, g, i: (b, i, P1_GL + g)),
                  pl.BlockSpec((1, 1, LANES), lambda b, g, i: (g, 0, 0)),
                  pl.BlockSpec((N_BIAS_TILES, NSA_REP, NSA_TK, NSA_TK), lambda b, g, i: (0, g, 0, 0),
                               pipeline_mode=once),
                  pl.BlockSpec((LANES, n_cmp), lambda b, g, i: (0, 0)),
                  pl.BlockSpec((seq, LANES), lambda b, g, i: (0, 0))],
        out_specs=pl.BlockSpec((1, tq, grp_w), lambda b, g, i: (b, i, g)),
        out_shape=jax.ShapeDtypeStruct((batch, seq, NSA_WIDTH), F32),
        scratch_shapes=[pltpu.VMEM((NSA_REP, tq, LANES), F32),
                        pltpu.VMEM((NSA_REP, tq, 2 * HEAD_DIM), F32),
                        pltpu.VMEM((NSA_REP, tq, SEL_TK), F32),
                        pltpu.VMEM((NSA_REP, tq, SEL_TK), BF16),
                        pltpu.VMEM((NSA_REP, tq, LANES), F32),
                        pltpu.VMEM((NSA_REP, tq, HEAD_DIM), F32),
                        pltpu.VMEM((LANES, tq), F32),
                        pltpu.VMEM((NSA_REP, tq, n_cmp), F32),
                        pltpu.VMEM((NSA_REP, tq, n_cmp), BF16),
                        pltpu.VMEM((3, tq, n_cmp), BF16)],
        compiler_params=_params(("arbitrary", "arbitrary", "arbitrary")),
        name="nsa_attn",
    )(p2, p2, p2, p2, p2, kcv, kcv, p1, gate_bias, bias, overlap_t, membership)


def _out_kernel(x_ref, oa_ref, on_ref, ga_ref, gn_ref, w_ref, o_ref):
    def norm(t, g):
        ms = jnp.mean(t * t, axis=-1, keepdims=True)
        return (t * lax.rsqrt(ms + EPS) * g).astype(BF16)

    ya = norm(oa_ref[...], ga_ref[...])
    yn = norm(on_ref[...], gn_ref[...])
    y = _dot(ya, w_ref[0:A_WIDTH, :]) + _dot(yn, w_ref[A_WIDTH:A_WIDTH + NSA_WIDTH, :])
    o_ref[...] = x_ref[...] + y


def _out_proj(x2, oa, on, gain, w, *, tm):
    n_tok, d = x2.shape
    return pl.pallas_call(
        _out_kernel,
        grid=(n_tok // tm,),
        in_specs=[pl.BlockSpec((tm, d), lambda i: (i, 0)),
                  pl.BlockSpec((tm, A_WIDTH), lambda i: (i, 0)),
                  pl.BlockSpec((tm, NSA_WIDTH), lambda i: (i, 0)),
                  pl.BlockSpec((1, A_WIDTH), lambda i: (0, 0)),
                  pl.BlockSpec((1, NSA_WIDTH), lambda i: (0, 0)),
                  pl.BlockSpec((d, d), lambda i: (0, 0))],
        out_specs=pl.BlockSpec((tm, d), lambda i: (i, 0)),
        out_shape=jax.ShapeDtypeStruct((n_tok, d), F32),
        compiler_params=_params(("arbitrary",)),
        name="out_proj",
    )(x2, oa, on, gain[:A_WIDTH].reshape(1, A_WIDTH), gain[A_WIDTH:].reshape(1, NSA_WIDTH), w)


def _pack_w_in(w_in, gate_bias, q_norm_a, k_norm_a, q_norm_nsa, k_norm_nsa):
    scale = HEAD_DIM ** -0.5
    sizes = (A_WIDTH,) * 3 + (NSA_WIDTH,) + (KV_WIDTH,) * 6 + (N_BRANCH * NSA_HEADS,)
    offs = np.concatenate([[0], np.cumsum(sizes)])
    qa, ka, va, qn, kc, vc, k_s, v_s, k_w, v_w, gl = [w_in[:, offs[i]:offs[i + 1]] for i in range(11)]
    per_grp = NSA_REP * N_BRANCH
    pad = jnp.zeros((w_in.shape[0], LANES - per_grp), w_in.dtype)
    gl_cols = [jnp.concatenate([gl[:, g * per_grp:(g + 1) * per_grp], pad], axis=1) for g in range(NSA_KV_GROUPS)]
    n_pad = P1_HEADS - P1_USED
    w_pad = jnp.zeros((w_in.shape[0], n_pad * LANES), w_in.dtype)
    w = jnp.concatenate([qa, ka, va, kc, vc] + gl_cols + [w_pad, qn, k_s, v_s, k_w, v_w], axis=1).astype(BF16)

    ones = jnp.ones((HEAD_DIM,), F32)
    tile = lambda v, n: jnp.tile(v, n)
    gain = jnp.concatenate([tile(q_norm_a * (scale * LOG2E), A_HEADS), tile(k_norm_a, A_HEADS),
                            tile(ones, A_HEADS + 6 + n_pad),
                            tile(q_norm_nsa * (scale * LOG2E), NSA_HEADS), tile(k_norm_nsa, 2), tile(ones, 2),
                            tile(k_norm_nsa, 2), tile(ones, 2)])
    flag = jnp.concatenate([jnp.ones((2 * A_WIDTH,), F32), jnp.zeros((A_WIDTH + (6 + n_pad) * HEAD_DIM,), F32),
                            jnp.ones((NSA_WIDTH + KV_WIDTH,), F32), jnp.zeros((KV_WIDTH,), F32),
                            jnp.ones((KV_WIDTH,), F32), jnp.zeros((KV_WIDTH,), F32)])
    gb = jnp.stack([jnp.concatenate([gate_bias[g * per_grp:(g + 1) * per_grp],
                                     jnp.zeros((LANES - per_grp,), F32)]) for g in range(NSA_KV_GROUPS)])
    return w, gain, flag, gb.reshape(NSA_KV_GROUPS, 1, LANES)


def _forward(x, ffn1_norm, ffn1_w1, ffn1_w3, ffn1_w2, mix_norm, w_in, gate_bias, q_norm_a, k_norm_a,
             q_norm_nsa, k_norm_nsa, cmp_pos, cmp_k_w1, cmp_k_w2, cmp_v_w1, cmp_v_w2, out_norm, w_out,
             ffn2_norm, ffn2_w1, ffn2_w3, ffn2_w2, rel_bias, *, tm_ffn, tf_ffn, tm_proj, tm_out):
    batch, seq, d = x.shape
    depth = ffn1_w1.shape[0]
    n_tok = batch * seq
    assert seq % DIL_SPAN == 0 and seq // SLC_LEN <= LANES
    assert seq % NSA_TQ == 0 and seq % SEL_TK == 0 and (seq // CMP_STRIDE) % LANES == 0
    assert n_tok % max(tm_ffn, tm_proj, tm_out) == 0 and d == D_MODEL

    bias_a = _expand_bias(rel_bias[:, :A_HEADS], jnp.asarray(_dilated_buckets()), scale=LOG2E)
    bias_a = bias_a.reshape(3, 2, A_HEADS, BAND, 2 * BAND)
    bias_n = _expand_bias(rel_bias[:, A_HEADS:], jnp.asarray(_nsa_buckets()),
                          scale=LOG2E, shift_far=True)
    overlap_t = jnp.asarray(_overlap_t_np(seq // CMP_STRIDE)).astype(BF16)
    membership = jnp.asarray(_block_membership_np(seq)).astype(BF16)

    x2 = x.reshape(n_tok, d)
    for l in range(depth):
        x2 = _ffn(x2, ffn1_norm[l], ffn1_w1[l].astype(BF16), ffn1_w3[l].astype(BF16),
                  ffn1_w2[l].astype(BF16), tm=tm_ffn, tf=tf_ffn)

        w_p, g_p, f_p, gb = _pack_w_in(w_in[l], gate_bias[l], q_norm_a[l], k_norm_a[l],
                                       q_norm_nsa[l], k_norm_nsa[l])
        p1, p2 = _proj(x2, mix_norm[l], w_p, g_p, f_p, P1_HEADS * LANES, tm=tm_proj, tn=1024)
        p1 = p1.reshape(batch, seq, P1_HEADS * LANES)
        p2 = p2.reshape(batch, seq, P2_HEADS * LANES)

        o_a = _dilated(p1, bias_a, batch, seq)

        cmp_w1 = jnp.stack([cmp_k_w1[l], cmp_v_w1[l]]).astype(BF16)
        cmp_w2 = jnp.stack([cmp_k_w2[l], cmp_v_w2[l]]).astype(BF16)
        cmp_gain = jnp.stack([k_norm_nsa[l], jnp.ones((HEAD_DIM,), F32)]).reshape(2, 1, HEAD_DIM)
        cmp_flag = jnp.stack([jnp.ones((HEAD_DIM,), F32), jnp.zeros((HEAD_DIM,), F32)]).reshape(2, 1, HEAD_DIM)
        kcv = _compress(p1, cmp_pos[l], cmp_w1, cmp_w2, cmp_gain, cmp_flag, batch, seq)

        o_n = _nsa(p1, p2, kcv, gb, bias_n, overlap_t, membership, batch, seq)

        x2 = _out_proj(x2, o_a.reshape(n_tok, A_WIDTH), o_n.reshape(n_tok, NSA_WIDTH), out_norm[l],
                       w_out[l].astype(BF16), tm=tm_out)

        x2 = _ffn(x2, ffn2_norm[l], ffn2_w1[l].astype(BF16), ffn2_w3[l].astype(BF16),
                  ffn2_w2[l].astype(BF16), tm=tm_ffn, tf=tf_ffn)
    return x2.reshape(batch, seq, d)


def kernel(x, ffn1_norm, ffn1_w1, ffn1_w3, ffn1_w2, mix_norm, w_in, gate_bias, q_norm_a, k_norm_a, q_norm_nsa, k_norm_nsa, cmp_pos, cmp_k_w1, cmp_k_w2, cmp_v_w1, cmp_v_w2, out_norm, w_out, ffn2_norm, ffn2_w1, ffn2_w3, ffn2_w2, rel_bias):
    return _forward(x, ffn1_norm, ffn1_w1, ffn1_w3, ffn1_w2, mix_norm, w_in, gate_bias, q_norm_a, k_norm_a,
                    q_norm_nsa, k_norm_nsa, cmp_pos, cmp_k_w1, cmp_k_w2, cmp_v_w1, cmp_v_w2, out_norm, w_out,
                    ffn2_norm, ffn2_w1, ffn2_w3, ffn2_w2, rel_bias,
                    tm_ffn=1024, tf_ffn=512, tm_proj=1024, tm_out=512)
```

```python
import functools
import math

import numpy as np
import jax
import jax.numpy as jnp
from jax import lax
from jax.experimental import pallas as pl
from jax.experimental.pallas import tpu as pltpu

D_MODEL = 2048
HEAD_DIM = 128
N_HEADS = 16
A_HEADS = 8
NSA_HEADS = 8
NSA_KV_GROUPS = 2
NSA_REP = 4
A_WIDTH = A_HEADS * HEAD_DIM
NSA_WIDTH = NSA_HEADS * HEAD_DIM
KV_WIDTH = NSA_KV_GROUPS * HEAD_DIM
N_BRANCH = 3
D_FF = 5632
DILATED_PATTERNS = ((128, 1), (512, 4), (2048, 16))
BAND = 128
CMP_STRIDE = 16
CMP_LEN = 32
CMP_HIDDEN = 512
SLC_LEN = 64
SLC_TOPK = 16
WIN_LEN = 512
N_BUCKETS = 32
MAX_DISTANCE = 2048
FORCE_SCORE = 1e6
NEG_INF = -1e30
EPS = 1e-6
LOG2E = 1.4426950408889634

LANES = 128
DIL_SPAN = 2048
NSA_TQ = 512
NSA_TK = 128
SEL_TK = 512
CMP_ROWS = 32
BIAS_FAR = 13
BIAS_FUTURE = 14
N_BIAS_TILES = 15
MASK_BIG = 1e30
VMEM_LIMIT = 56 * 1024 * 1024

F32 = jnp.float32
BF16 = jnp.bfloat16

P1_QA, P1_KA, P1_VA, P1_KC, P1_VC, P1_GL = 0, 8, 16, 24, 26, 28
P1_USED = 30
P1_HEADS = 32
P2_QN, P2_KS, P2_VS, P2_KW, P2_VW = 0, 8, 10, 12, 14
P2_HEADS = 16


def _params(sem, flags=None):
    return pltpu.CompilerParams(dimension_semantics=sem, vmem_limit_bytes=VMEM_LIMIT, flags=flags)


def _nt_dot(a, b):
    return lax.dot_general(a, b, (((1,), (1,)), ((), ())), preferred_element_type=F32)


def _dot(a, b):
    return jnp.dot(a, b, preferred_element_type=F32)


def _rel_bucket_np(dist):
    n = np.maximum(dist, 0)
    max_exact = N_BUCKETS // 2
    nf = np.maximum(n, 1).astype(np.float32)
    log_b = max_exact + (np.log(nf / np.float32(max_exact)) / np.float32(math.log(MAX_DISTANCE / max_exact))
                         * np.float32(N_BUCKETS - max_exact)).astype(np.int32)
    return np.where(n < max_exact, n, np.minimum(log_b, N_BUCKETS - 1)).astype(np.int32)


def _dilated_buckets():
    i = np.arange(BAND)[:, None]
    j = np.arange(2 * BAND)[None, :] - BAND
    diff = i - j
    out = []
    for window, dilation in DILATED_PATTERNS:
        mask = (diff >= 0) & (diff <= window // dilation)
        bkt = _rel_bucket_np(diff * dilation)
        out.append(np.where(mask, bkt, -1))
        out.append(np.where(mask & (j >= 0), bkt, -1))
    return np.stack(out).astype(np.int32)


def _nsa_buckets():
    i = np.arange(NSA_TK)[:, None]
    j = np.arange(NSA_TK)[None, :]
    tiles = []
    for m in range(BIAS_FAR + 1):
        dist = NSA_TK * m + i - j
        tiles.append(np.where(dist >= 0, _rel_bucket_np(dist), -1))
    assert (tiles[BIAS_FAR] == N_BUCKETS - 1).all()
    tiles.append(np.full((NSA_TK, NSA_TK), -1))
    assert len(tiles) == N_BIAS_TILES and BIAS_FUTURE == N_BIAS_TILES - 1
    return np.stack(tiles).astype(np.int32)


def _overlap_t_np(n_cmp_pad):
    c_start = np.arange(n_cmp_pad)[None, :] * CMP_STRIDE
    s_start = np.arange(LANES)[:, None] * SLC_LEN
    return ((c_start < s_start + SLC_LEN) & (c_start + CMP_LEN > s_start)).astype(np.float32)


def _block_membership_np(seq):
    k_blk = np.arange(seq)[:, None] // SLC_LEN
    return np.where(k_blk == np.arange(LANES)[None, :], MASK_BIG, 0.0).astype(np.float32)


def _bias_kernel(tbl_ref, bkt_ref, o_ref, *, n_heads, scale, shift_far):
    bkt = bkt_ref[0]
    for h in range(n_heads):
        acc = jnp.full(bkt.shape, NEG_INF, F32)
        base = tbl_ref[N_BUCKETS - 1, h] if shift_far else 0.0
        for b in range(N_BUCKETS):
            acc = jnp.where(bkt == b, (tbl_ref[b, h] - base) * scale, acc)
        o_ref[0, h] = acc


def _expand_bias(table, buckets, scale=1.0, shift_far=False):
    n_t, rows, width = buckets.shape
    nh = table.shape[1]
    return pl.pallas_call(
        functools.partial(_bias_kernel, n_heads=nh, scale=scale, shift_far=shift_far),
        grid=(n_t,),
        in_specs=[pl.BlockSpec(memory_space=pltpu.SMEM),
                  pl.BlockSpec((1, rows, width), lambda t: (t, 0, 0))],
        out_specs=pl.BlockSpec((1, nh, rows, width), lambda t: (t, 0, 0, 0)),
        out_shape=jax.ShapeDtypeStruct((n_t, nh, rows, width), F32),
        compiler_params=_params(("arbitrary",)),
        name="bias_expand",
    )(table, buckets)


def _ffn_kernel(x_ref, g_ref, w1_ref, w3_ref, w2_ref, o_ref, n_scr):
    j = pl.program_id(1)

    @pl.when(j == 0)
    def _():
        x = x_ref[...]
        ms = jnp.mean(x * x, axis=-1, keepdims=True)
        n_scr[...] = (x * lax.rsqrt(ms + EPS) * g_ref[...]).astype(BF16)
        o_ref[...] = jnp.zeros_like(o_ref)

    n = n_scr[...]
    h1 = _dot(n, w1_ref[...])
    h3 = _dot(n, w3_ref[...])
    a = (h1 * jax.nn.sigmoid(h1) * h3).astype(BF16)
    o_ref[...] += _dot(a, w2_ref[...])

    @pl.when(j == pl.num_programs(1) - 1)
    def _():
        o_ref[...] = x_ref[...] + 0.5 * o_ref[...]


def _ffn(x2, gain, w1, w3, w2, *, tm, tf):
    n_tok, d = x2.shape
    f = w1.shape[1]
    return pl.pallas_call(
        _ffn_kernel,
        grid=(n_tok // tm, f // tf),
        in_specs=[pl.BlockSpec((tm, d), lambda i, j: (i, 0)),
                  pl.BlockSpec((1, d), lambda i, j: (0, 0)),
                  pl.BlockSpec((d, tf), lambda i, j: (0, j)),
                  pl.BlockSpec((d, tf), lambda i, j: (0, j)),
                  pl.BlockSpec((tf, d), lambda i, j: (j, 0))],
        out_specs=pl.BlockSpec((tm, d), lambda i, j: (i, 0)),
        out_shape=jax.ShapeDtypeStruct((n_tok, d), F32),
        scratch_shapes=[pltpu.VMEM((tm, d), BF16)],
        compiler_params=_params(("arbitrary", "arbitrary")),
        name="ffn",
    )(x2, gain.reshape(1, d), w1, w3, w2)


def _proj_kernel(x_ref, g_ref, w_ref, cg_ref, cf_ref, o1_ref, o2_ref, n_scr, acc_scr, *, n1):
    j = pl.program_id(1)
    last = pl.num_programs(1) - 1

    def matmul():
        acc_scr[...] = _dot(n_scr[...], w_ref[...])

    def epilogue(o_ref):
        for h in range(acc_scr.shape[1] // LANES):
            cols = slice(h * LANES, (h + 1) * LANES)
            a = acc_scr[:, cols]
            ms = jnp.mean(a * a, axis=-1, keepdims=True)
            inv = jnp.where(cf_ref[:, cols] > 0.0, lax.rsqrt(ms + EPS), 1.0)
            o_ref[:, cols] = (a * inv * cg_ref[:, cols]).astype(o_ref.dtype)

    @pl.when(j == 0)
    def _():
        x = x_ref[...]
        ms = jnp.mean(x * x, axis=-1, keepdims=True)
        n_scr[...] = (x * lax.rsqrt(ms + EPS) * g_ref[...]).astype(BF16)
        matmul()

    @pl.when(jnp.logical_and(j > 0, j <= n1))
    def _():
        epilogue(o1_ref)
        matmul()

    @pl.when(jnp.logical_and(j > n1, j < last))
    def _():
        epilogue(o2_ref)
        matmul()

    @pl.when(j == last)
    def _():
        epilogue(o2_ref)


def _proj(x2, gain, w, col_gain, col_flag, cols1, *, tm, tn):
    n_tok, d = x2.shape
    n_cols = w.shape[1]
    n1 = cols1 // tn
    n2 = (n_cols - cols1) // tn
    assert n1 * tn == cols1 and (n1 + n2) * tn == n_cols
    n_j = n1 + n2
    prev = lambda j: jnp.maximum(j - 1, 0)
    return pl.pallas_call(
        functools.partial(_proj_kernel, n1=n1),
        grid=(n_tok // tm, n_j + 1),
        in_specs=[pl.BlockSpec((tm, d), lambda i, j: (i, 0)),
                  pl.BlockSpec((1, d), lambda i, j: (0, 0)),
                  pl.BlockSpec((d, tn), lambda i, j: (0, jnp.minimum(j, n_j - 1))),
                  pl.BlockSpec((1, tn), lambda i, j: (0, prev(j))),
                  pl.BlockSpec((1, tn), lambda i, j: (0, prev(j)))],
        out_specs=[pl.BlockSpec((tm, tn), lambda i, j: (i, jnp.minimum(prev(j), n1 - 1))),
                   pl.BlockSpec((tm, tn), lambda i, j: (i, jnp.maximum(prev(j) - n1, 0)))],
        out_shape=[jax.ShapeDtypeStruct((n_tok, cols1), F32),
                   jax.ShapeDtypeStruct((n_tok, n_cols - cols1), BF16)],
        scratch_shapes=[pltpu.VMEM((tm, d), BF16), pltpu.VMEM((tm, tn), F32)],
        compiler_params=_params(("arbitrary", "arbitrary")),
        name="in_proj",
    )(x2, gain.reshape(1, d), w, col_gain.reshape(1, n_cols), col_flag.reshape(1, n_cols))


def _dilated_kernel(q_ref, kc_ref, kp_ref, vc_ref, vp_ref, bias_ref, o_ref,
                    k_scr, v_scr, out_scr, lse_scr):
    n = pl.program_id(2)
    k_scr[0:DIL_SPAN, :] = kp_ref[0]
    k_scr[DIL_SPAN:2 * DIL_SPAN, :] = kc_ref[0]
    v_scr[0:DIL_SPAN, :] = vp_ref[0]
    v_scr[DIL_SPAN:2 * DIL_SPAN, :] = vc_ref[0]

    for p, (_, dil) in enumerate(DILATED_PATTERNS):
        span = BAND * dil

        def unit(u, carry, p=p, dil=dil, span=span):
            s = u // dil
            r = u - s * dil
            base = s * span + r
            first = jnp.logical_and(n == 0, s == 0).astype(jnp.int32)
            q = q_ref[0, pl.ds(base, BAND, stride=dil), :].astype(BF16)
            kk = k_scr[pl.ds(DIL_SPAN + base - span, 2 * BAND, stride=dil), :].astype(BF16)
            vv = v_scr[pl.ds(DIL_SPAN + base - span, 2 * BAND, stride=dil), :].astype(BF16)
            lg = _nt_dot(q, kk) + bias_ref[p, first, 0]
            m = jnp.max(lg, axis=-1, keepdims=True)
            e = jnp.exp2(lg - m).astype(BF16)
            acc = _dot(e, jnp.concatenate([vv, jnp.ones((2 * BAND, LANES), BF16)], axis=1))
            ssum = acc[:, LANES:]
            out_scr[p, pl.ds(base, BAND, stride=dil), :] = acc[:, :LANES] / ssum
            lse_scr[p, pl.ds(base, BAND, stride=dil), :] = m + jnp.log2(ssum)
            return carry

        lax.fori_loop(0, DIL_SPAN // BAND, unit, 0, unroll=8)

    chunk = 256

    def combine(c, carry):
        rows = pl.ds(pl.multiple_of(c * chunk, chunk), chunk)
        l0, l1, l2 = lse_scr[0, rows, :], lse_scr[1, rows, :], lse_scr[2, rows, :]
        big = jnp.maximum(jnp.maximum(l0, l1), l2)
        e0, e1, e2 = jnp.exp2(l0 - big), jnp.exp2(l1 - big), jnp.exp2(l2 - big)
        num = e0 * out_scr[0, rows, :] + e1 * out_scr[1, rows, :] + e2 * out_scr[2, rows, :]
        o_ref[0, rows, :] = num / (e0 + e1 + e2)
        return carry

    lax.fori_loop(0, DIL_SPAN // chunk, combine, 0)


def _dilated(p1, bias, batch, seq):
    blk = (1, DIL_SPAN, LANES)
    prev = lambda nn: jnp.maximum(nn - 1, 0)
    return pl.pallas_call(
        _dilated_kernel,
        grid=(batch, A_HEADS, seq // DIL_SPAN),
        in_specs=[pl.BlockSpec(blk, lambda b, h, nn: (b, nn, P1_QA + h)),
                  pl.BlockSpec(blk, lambda b, h, nn: (b, nn, P1_KA + h)),
                  pl.BlockSpec(blk, lambda b, h, nn: (b, prev(nn), P1_KA + h)),
                  pl.BlockSpec(blk, lambda b, h, nn: (b, nn, P1_VA + h)),
                  pl.BlockSpec(blk, lambda b, h, nn: (b, prev(nn), P1_VA + h)),
                  pl.BlockSpec((3, 2, 1, BAND, 2 * BAND), lambda b, h, nn: (0, 0, h, 0, 0))],
        out_specs=pl.BlockSpec(blk, lambda b, h, nn: (b, nn, h)),
        out_shape=jax.ShapeDtypeStruct((batch, seq, A_WIDTH), F32),
        scratch_shapes=[pltpu.VMEM((2 * DIL_SPAN, LANES), F32),
                        pltpu.VMEM((2 * DIL_SPAN, LANES), F32),
                        pltpu.VMEM((3, DIL_SPAN, LANES), F32),
                        pltpu.VMEM((3, DIL_SPAN, LANES), F32)],
        compiler_params=_params(("arbitrary", "arbitrary", "arbitrary")),
        name="dilated_attn",
    )(p1, p1, p1, p1, p1, bias)


def _compress_kernel(t_ref, pos_ref, w1_ref, w2_ref, gain_ref, flag_ref, o_ref, lo_scr, hi_scr):
    n_blk = o_ref.shape[3]
    lo_scr[...] = jnp.zeros_like(lo_scr)
    hi_scr[...] = jnp.zeros_like(hi_scr)
    for i in range(CMP_STRIDE):
        rows = t_ref[0, pl.ds(i, n_blk, stride=CMP_STRIDE), :]
        a = (rows + pos_ref[i:i + 1, :]).astype(BF16)
        lo_scr[...] += _dot(a, w1_ref[0, i * HEAD_DIM:(i + 1) * HEAD_DIM, :])
        b = (rows + pos_ref[CMP_STRIDE + i:CMP_STRIDE + i + 1, :]).astype(BF16)
        hi_scr[...] += _dot(b, w1_ref[0, (CMP_STRIDE + i) * HEAD_DIM:(CMP_STRIDE + i + 1) * HEAD_DIM, :])
    hid = lo_scr[...] + pltpu.roll(hi_scr[...], n_blk - 1, 0)
    o = _dot(jax.nn.gelu(hid).astype(BF16), w2_ref[0])
    ms = jnp.mean(o * o, axis=-1, keepdims=True)
    inv = jnp.where(flag_ref[0] > 0.0, lax.rsqrt(ms + EPS), 1.0)
    o_ref[0, 0, 0] = o * inv * gain_ref[0]


def _compress(p1, pos, w1, w2, gain, flag, batch, seq):
    n_blk = seq // CMP_STRIDE
    return pl.pallas_call(
        _compress_kernel,
        grid=(batch, 2, NSA_KV_GROUPS),
        in_specs=[pl.BlockSpec((1, seq, LANES), lambda b, kv, g: (b, 0, P1_KC + 2 * kv + g)),
                  pl.BlockSpec((CMP_LEN, HEAD_DIM), lambda b, kv, g: (0, 0)),
                  pl.BlockSpec((1, CMP_LEN * HEAD_DIM, CMP_HIDDEN), lambda b, kv, g: (kv, 0, 0)),
                  pl.BlockSpec((1, CMP_HIDDEN, HEAD_DIM), lambda b, kv, g: (kv, 0, 0)),
                  pl.BlockSpec((1, 1, HEAD_DIM), lambda b, kv, g: (kv, 0, 0)),
                  pl.BlockSpec((1, 1, HEAD_DIM), lambda b, kv, g: (kv, 0, 0))],
        out_specs=pl.BlockSpec((1, 1, 1, n_blk, HEAD_DIM), lambda b, kv, g: (b, kv, g, 0, 0)),
        out_shape=jax.ShapeDtypeStruct((batch, 2, NSA_KV_GROUPS, n_blk, HEAD_DIM), F32),
        scratch_shapes=[pltpu.VMEM((n_blk, CMP_HIDDEN), F32), pltpu.VMEM((n_blk, CMP_HIDDEN), F32)],
        compiler_params=_params(("arbitrary", "arbitrary", "arbitrary")),
        name="nsa_compress",
    )(p1, pos, w1, w2, gain, flag)


def _nsa_kernel(q_ref, ks_ref, vs_ref, kw_ref, vw_ref, kc_ref, vc_ref, gl_ref, gb_ref,
                bias_ref, ovl_ref, et_ref, o_ref, m_scr, acc_scr, s_scr, p_scr, ocmp_scr, imp_scr,
                lc_scr, pcb_scr, split_scr):
    qi = pl.program_id(2)
    tq, rep = NSA_TQ, NSA_REP
    n_cmp = kc_ref.shape[3]

    q2 = q_ref[0]
    q_heads = [q2[:, r * HEAD_DIM:(r + 1) * HEAD_DIM] for r in range(rep)]
    t_col = qi * tq + lax.broadcasted_iota(jnp.int32, (tq, 1), 0)

    def compressed(width):
        cmp_end = lax.broadcasted_iota(jnp.int32, (1, width), 1) * CMP_STRIDE + (CMP_LEN - 1)
        kcb = kc_ref[0, 0, 0, 0:width, :].astype(BF16)
        vcb = vc_ref[0, 0, 0, 0:width, :].astype(BF16)
        for r in range(rep):
            lc_scr[r, :, 0:width] = _nt_dot(q_heads[r], kcb)
        for c in range(tq // CMP_ROWS):
            rows = slice(c * CMP_ROWS, (c + 1) * CMP_ROWS)
            cmask = cmp_end <= t_col[rows]
            pc_sum = jnp.zeros((CMP_ROWS, width), F32)
            for r in range(rep):
                lc = jnp.where(cmask, lc_scr[r, rows, 0:width], NEG_INF)
                mc = jnp.max(lc, axis=-1, keepdims=True)
                pc = jnp.where(cmask, jnp.exp2(lc - mc), 0.0)
                pc = pc / jnp.maximum(jnp.sum(pc, axis=-1, keepdims=True), 1e-30)
                pcb_scr[r, rows, 0:width] = pc.astype(BF16)
                pc_sum = pc_sum + pc
            hi = pc_sum.astype(BF16)
            rem = pc_sum - hi.astype(F32)
            mid = rem.astype(BF16)
            split_scr[0, rows, 0:width] = hi
            split_scr[1, rows, 0:width] = mid
            split_scr[2, rows, 0:width] = (rem - mid.astype(F32)).astype(BF16)
        for r in range(rep):
            ocmp_scr[r] = _dot(pcb_scr[r, :, 0:width], vcb)
        ovl_t = ovl_ref[:, 0:width]
        imp_scr[...] = (_nt_dot(ovl_t, split_scr[0, :, 0:width]) + _nt_dot(ovl_t, split_scr[1, :, 0:width])
                        + _nt_dot(ovl_t, split_scr[2, :, 0:width]))

    n_grp = n_cmp // LANES
    need = jnp.minimum(((qi + 1) * tq - CMP_LEN) // (CMP_STRIDE * LANES) + 1, n_grp)
    for k in range(1, n_grp + 1):
        pl.when(need == k)(functools.partial(compressed, k * LANES))
    o_cmp = [ocmp_scr[r] for r in range(rep)]
    imp_t = imp_scr[...]

    n_rb = tq // NSA_TK
    n_wt = WIN_LEN // NSA_TK + 1
    w_keys = n_wt * NSA_TK
    win_parts = [[] for _ in range(rep)]
    for a in range(n_rb):
        rows = slice(a * NSA_TK, (a + 1) * NSA_TK)
        blk_q = qi * n_rb + a
        j0 = jnp.maximum(blk_q - (n_wt - 1), 0)
        w_rows = pl.ds(pl.multiple_of(j0 * NSA_TK, NSA_TK), w_keys)
        kw_tile = kw_ref[0, w_rows, :]
        vw_aug = jnp.concatenate([vw_ref[0, w_rows, :], jnp.ones((w_keys, HEAD_DIM), BF16)], axis=1)
        dist = t_col[rows] - (j0 * NSA_TK + lax.broadcasted_iota(jnp.int32, (1, w_keys), 1))
        wmask = jnp.where(dist >= 0, dist, WIN_LEN) < WIN_LEN
        tiles = [jnp.clip(blk_q - j0 - i, 0, BIAS_FAR) for i in range(n_wt)]
        for r in range(rep):
            s = _nt_dot(q_heads[r][rows], kw_tile) + jnp.concatenate([bias_ref[t, r] for t in tiles], axis=-1)
            s = jnp.where(wmask, s, NEG_INF)
            p = jnp.exp2(s - jnp.max(s, axis=-1, keepdims=True)).astype(BF16)
            acc = _dot(p, vw_aug)
            win_parts[r].append(acc[:, :HEAD_DIM] / acc[:, HEAD_DIM:])
    o_win = [jnp.concatenate(parts, axis=0) for parts in win_parts]

    blk = lax.broadcasted_iota(jnp.int32, (LANES, 1), 0)
    blk_f = blk.astype(F32)
    t_row = qi * tq + lax.broadcasted_iota(jnp.int32, (1, tq), 1)
    cur = t_row >> (SLC_LEN.bit_length() - 1)
    valid = blk * SLC_LEN <= t_row
    forced = (blk == 0) | (blk == cur) | (blk == cur - 1)
    score = jnp.where(valid, jnp.where(forced, FORCE_SCORE, imp_t), -1.0)
    sel_t = jnp.zeros((LANES, tq), F32)
    for _ in range(SLC_TOPK):
        best = jnp.max(score, axis=0, keepdims=True)
        idx = jnp.min(jnp.where(score == best, blk_f, float(LANES)), axis=0, keepdims=True)
        hit = blk_f == idx
        sel_t = jnp.where(hit, 1.0, sel_t)
        score = jnp.where(hit, -3e38, score)
    unsel = (sel_t.T - 1.0).astype(BF16)
    lhs = [jnp.concatenate([q_heads[r], unsel], axis=1) for r in range(rep)]

    ones_v = jnp.ones((SEL_TK, HEAD_DIM), BF16)

    sub = SEL_TK // NSA_TK
    n_tiles = ((qi + 1) * n_rb - 1) // sub + 1

    def key_rows(jj):
        return pl.ds(pl.multiple_of(jj * SEL_TK, SEL_TK), SEL_TK)

    def logits_stage(jj, heads):
        rows = key_rows(jj)
        rhs = jnp.concatenate([ks_ref[0, rows, :], et_ref[rows, :]], axis=1)
        for r in heads:
            s_scr[r] = _nt_dot(lhs[r], rhs)

    def softmax_stage(jj, heads):
        for a in range(n_rb):
            rows = slice(a * NSA_TK, (a + 1) * NSA_TK)
            u = qi * n_rb + a - sub * jj
            tiles = [jnp.where(u < i, BIAS_FUTURE, jnp.minimum(u - i, BIAS_FAR)) for i in range(sub)]
            for r in heads:
                s = s_scr[r, rows, :] + jnp.concatenate([bias_ref[t, r] for t in tiles], axis=-1)
                m_old = m_scr[1 - (jj & 1), r, rows, :]
                m_new = jnp.maximum(m_old, jnp.max(s, axis=-1, keepdims=True))
                p_scr[r, rows, :] = jnp.exp2(s - jnp.concatenate([m_new] * sub, axis=1)).astype(BF16)
                m_scr[jj & 1, r, rows, :] = m_new

    def pv_stage(it, heads):
        v_aug = jnp.concatenate([vs_ref[0, key_rows(jnp.maximum(it - 1, 0)), :], ones_v], axis=1)
        for r in heads:
            alpha = jnp.exp2(m_scr[it & 1, r] - m_scr[1 - (it & 1), r])
            acc_scr[r] = jnp.concatenate([alpha, alpha], axis=1) * acc_scr[r] + _dot(p_scr[r], v_aug)

    all_heads = range(rep)
    m_scr[...] = jnp.full(m_scr.shape, NEG_INF, F32)
    acc_scr[...] = jnp.zeros_like(acc_scr)
    p_scr[...] = jnp.zeros_like(p_scr)
    logits_stage(0, all_heads)

    def sel_step(it, carry):
        pv_stage(it, all_heads)
        softmax_stage(it, all_heads)
        logits_stage(jnp.minimum(it + 1, n_tiles - 1), all_heads)
        return carry

    lax.fori_loop(0, n_tiles, sel_step, 0)
    pv_stage(n_tiles, all_heads)

    gates = jax.nn.sigmoid(gl_ref[0] + gb_ref[0])
    o_slc = []
    for r in range(rep):
        acc = acc_scr[r]
        o_slc.append(acc[:, :HEAD_DIM] / acc[:, HEAD_DIM:])
    for r in range(rep):
        c = r * N_BRANCH
        out = (gates[:, c:c + 1] * o_cmp[r] + gates[:, c + 1:c + 2] * o_slc[r]
               + gates[:, c + 2:c + 3] * o_win[r])
        o_ref[0, :, r * HEAD_DIM:(r + 1) * HEAD_DIM] = out


def _nsa(p1, p2, kcv, gate_bias, bias, overlap_t, membership, batch, seq):
    tq = NSA_TQ
    n_cmp = seq // CMP_STRIDE
    kv_blk = (1, seq, LANES)
    grp_w = NSA_REP * HEAD_DIM
    once = pl.Buffered(1)
    return pl.pallas_call(
        _nsa_kernel,
        grid=(batch, NSA_KV_GROUPS, seq // tq),
        in_specs=[pl.BlockSpec((1, tq, grp_w), lambda b, g, i: (b, i, g)),
                  pl.BlockSpec(kv_blk, lambda b, g, i: (b, 0, P2_KS + g), pipeline_mode=once),
                  pl.BlockSpec(kv_blk, lambda b, g, i: (b, 0, P2_VS + g), pipeline_mode=once),
                  pl.BlockSpec(kv_blk, lambda b, g, i: (b, 0, P2_KW + g), pipeline_mode=once),
                  pl.BlockSpec(kv_blk, lambda b, g, i: (b, 0, P2_VW + g), pipeline_mode=once),
                  pl.BlockSpec((1, 1, 1, n_cmp, HEAD_DIM), lambda b, g, i: (b, 0, g, 0, 0)),
                  pl.BlockSpec((1, 1, 1, n_cmp, HEAD_DIM), lambda b, g, i: (b, 1, g, 0, 0)),
                  pl.BlockSpec((1, tq, LANES), lambda b, g, i: (b, i, P1_GL + g)),
                  pl.BlockSpec((1, 1, LANES), lambda b, g, i: (g, 0, 0)),
                  pl.BlockSpec((N_BIAS_TILES, NSA_REP, NSA_TK, NSA_TK), lambda b, g, i: (0, g, 0, 0),
                               pipeline_mode=once),
                  pl.BlockSpec((LANES, n_cmp), lambda b, g, i: (0, 0)),
                  pl.BlockSpec((seq, LANES), lambda b, g, i: (0, 0))],
        out_specs=pl.BlockSpec((1, tq, grp_w), lambda b, g, i: (b, i, g)),
        out_shape=jax.ShapeDtypeStruct((batch, seq, NSA_WIDTH), F32),
        scratch_shapes=[pltpu.VMEM((2, NSA_REP, tq, LANES), F32),
                        pltpu.VMEM((NSA_REP, tq, 2 * HEAD_DIM), F32),
                        pltpu.VMEM((NSA_REP, tq, SEL_TK), F32),
                        pltpu.VMEM((NSA_REP, tq, SEL_TK), BF16),
                        pltpu.VMEM((NSA_REP, tq, HEAD_DIM), F32),
                        pltpu.VMEM((LANES, tq), F32),
                        pltpu.VMEM((NSA_REP, tq, n_cmp), F32),
                        pltpu.VMEM((NSA_REP, tq, n_cmp), BF16),
                        pltpu.VMEM((3, tq, n_cmp), BF16)],
        compiler_params=_params(("arbitrary", "arbitrary", "arbitrary")),
        name="nsa_attn",
    )(p2, p2, p2, p2, p2, kcv, kcv, p1, gate_bias, bias, overlap_t, membership)


def _out_kernel(x_ref, oa_ref, on_ref, ga_ref, gn_ref, w_ref, o_ref):
    def norm(t, g):
        ms = jnp.mean(t * t, axis=-1, keepdims=True)
        return (t * lax.rsqrt(ms + EPS) * g).astype(BF16)

    ya = norm(oa_ref[...], ga_ref[...])
    yn = norm(on_ref[...], gn_ref[...])
    y = _dot(ya, w_ref[0:A_WIDTH, :]) + _dot(yn, w_ref[A_WIDTH:A_WIDTH + NSA_WIDTH, :])
    o_ref[...] = x_ref[...] + y


def _out_proj(x2, oa, on, gain, w, *, tm):
    n_tok, d = x2.shape
    return pl.pallas_call(
        _out_kernel,
        grid=(n_tok // tm,),
        in_specs=[pl.BlockSpec((tm, d), lambda i: (i, 0)),
                  pl.BlockSpec((tm, A_WIDTH), lambda i: (i, 0)),
                  pl.BlockSpec((tm, NSA_WIDTH), lambda i: (i, 0)),
                  pl.BlockSpec((1, A_WIDTH), lambda i: (0, 0)),
                  pl.BlockSpec((1, NSA_WIDTH), lambda i: (0, 0)),
                  pl.BlockSpec((d, d), lambda i: (0, 0))],
        out_specs=pl.BlockSpec((tm, d), lambda i: (i, 0)),
        out_shape=jax.ShapeDtypeStruct((n_tok, d), F32),
        compiler_params=_params(("arbitrary",)),
        name="out_proj",
    )(x2, oa, on, gain[:A_WIDTH].reshape(1, A_WIDTH), gain[A_WIDTH:].reshape(1, NSA_WIDTH), w)


def _pack_w_in(w_in, gate_bias, q_norm_a, k_norm_a, q_norm_nsa, k_norm_nsa):
    scale = HEAD_DIM ** -0.5
    sizes = (A_WIDTH,) * 3 + (NSA_WIDTH,) + (KV_WIDTH,) * 6 + (N_BRANCH * NSA_HEADS,)
    offs = np.concatenate([[0], np.cumsum(sizes)])
    qa, ka, va, qn, kc, vc, k_s, v_s, k_w, v_w, gl = [w_in[:, offs[i]:offs[i + 1]] for i in range(11)]
    per_grp = NSA_REP * N_BRANCH
    pad = jnp.zeros((w_in.shape[0], LANES - per_grp), w_in.dtype)
    gl_cols = [jnp.concatenate([gl[:, g * per_grp:(g + 1) * per_grp], pad], axis=1) for g in range(NSA_KV_GROUPS)]
    n_pad = P1_HEADS - P1_USED
    w_pad = jnp.zeros((w_in.shape[0], n_pad * LANES), w_in.dtype)
    w = jnp.concatenate([qa, ka, va, kc, vc] + gl_cols + [w_pad, qn, k_s, v_s, k_w, v_w], axis=1).astype(BF16)

    ones = jnp.ones((HEAD_DIM,), F32)
    tile = lambda v, n: jnp.tile(v, n)
    gain = jnp.concatenate([tile(q_norm_a * (scale * LOG2E), A_HEADS), tile(k_norm_a, A_HEADS),
                            tile(ones, A_HEADS + 6 + n_pad),
                            tile(q_norm_nsa * (scale * LOG2E), NSA_HEADS), tile(k_norm_nsa, 2), tile(ones, 2),
                            tile(k_norm_nsa, 2), tile(ones, 2)])
    flag = jnp.concatenate([jnp.ones((2 * A_WIDTH,), F32), jnp.zeros((A_WIDTH + (6 + n_pad) * HEAD_DIM,), F32),
                            jnp.ones((NSA_WIDTH + KV_WIDTH,), F32), jnp.zeros((KV_WIDTH,), F32),
                            jnp.ones((KV_WIDTH,), F32), jnp.zeros((KV_WIDTH,), F32)])
    gb = jnp.stack([jnp.concatenate([gate_bias[g * per_grp:(g + 1) * per_grp],
                                     jnp.zeros((LANES - per_grp,), F32)]) for g in range(NSA_KV_GROUPS)])
    return w, gain, flag, gb.reshape(NSA_KV_GROUPS, 1, LANES)


def _forward(x, ffn1_norm, ffn1_w1, ffn1_w3, ffn1_w2, mix_norm, w_in, gate_bias, q_norm_a, k_norm_a,
             q_norm_nsa, k_norm_nsa, cmp_pos, cmp_k_w1, cmp_k_w2, cmp_v_w1, cmp_v_w2, out_norm, w_out,
             ffn2_norm, ffn2_w1, ffn2_w3, ffn2_w2, rel_bias, *, tm_ffn, tf_ffn, tm_proj, tm_out):
    batch, seq, d = x.shape
    depth = ffn1_w1.shape[0]
    n_tok = batch * seq
    assert seq % DIL_SPAN == 0 and seq // SLC_LEN <= LANES
    assert seq % NSA_TQ == 0 and seq % SEL_TK == 0 and (seq // CMP_STRIDE) % LANES == 0
    assert n_tok % max(tm_ffn, tm_proj, tm_out) == 0 and d == D_MODEL

    bias_a = _expand_bias(rel_bias[:, :A_HEADS], jnp.asarray(_dilated_buckets()), scale=LOG2E)
    bias_a = bias_a.reshape(3, 2, A_HEADS, BAND, 2 * BAND)
    bias_n = _expand_bias(rel_bias[:, A_HEADS:], jnp.asarray(_nsa_buckets()),
                          scale=LOG2E, shift_far=True)
    overlap_t = jnp.asarray(_overlap_t_np(seq // CMP_STRIDE)).astype(BF16)
    membership = jnp.asarray(_block_membership_np(seq)).astype(BF16)

    x2 = x.reshape(n_tok, d)
    for l in range(depth):
        x2 = _ffn(x2, ffn1_norm[l], ffn1_w1[l].astype(BF16), ffn1_w3[l].astype(BF16),
                  ffn1_w2[l].astype(BF16), tm=tm_ffn, tf=tf_ffn)

        w_p, g_p, f_p, gb = _pack_w_in(w_in[l], gate_bias[l], q_norm_a[l], k_norm_a[l],
                                       q_norm_nsa[l], k_norm_nsa[l])
        p1, p2 = _proj(x2, mix_norm[l], w_p, g_p, f_p, P1_HEADS * LANES, tm=tm_proj, tn=1024)
        p1 = p1.reshape(batch, seq, P1_HEADS * LANES)
        p2 = p2.reshape(batch, seq, P2_HEADS * LANES)

        o_a = _dilated(p1, bias_a, batch, seq)

        cmp_w1 = jnp.stack([cmp_k_w1[l], cmp_v_w1[l]]).astype(BF16)
        cmp_w2 = jnp.stack([cmp_k_w2[l], cmp_v_w2[l]]).astype(BF16)
        cmp_gain = jnp.stack([k_norm_nsa[l], jnp.ones((HEAD_DIM,), F32)]).reshape(2, 1, HEAD_DIM)
        cmp_flag = jnp.stack([jnp.ones((HEAD_DIM,), F32), jnp.zeros((HEAD_DIM,), F32)]).reshape(2, 1, HEAD_DIM)
        kcv = _compress(p1, cmp_pos[l], cmp_w1, cmp_w2, cmp_gain, cmp_flag, batch, seq)

        o_n = _nsa(p1, p2, kcv, gb, bias_n, overlap_t, membership, batch, seq)

        x2 = _out_proj(x2, o_a.reshape(n_tok, A_WIDTH), o_n.reshape(n_tok, NSA_WIDTH), out_norm[l],
                       w_out[l].astype(BF16), tm=tm_out)

        x2 = _ffn(x2, ffn2_norm[l], ffn2_w1[l].astype(BF16), ffn2_w3[l].astype(BF16),
                  ffn2_w2[l].astype(BF16), tm=tm_ffn, tf=tf_ffn)
    return x2.reshape(batch, seq, d)


def kernel(x, ffn1_norm, ffn1_w1, ffn1_w3, ffn1_w2, mix_norm, w_in, gate_bias, q_norm_a, k_norm_a, q_norm_nsa, k_norm_nsa, cmp_pos, cmp_k_w1, cmp_k_w2, cmp_v_w1, cmp_v_w2, out_norm, w_out, ffn2_norm, ffn2_w1, ffn2_w3, ffn2_w2, rel_bias):
    return _forward(x, ffn1_norm, ffn1_w1, ffn1_w3, ffn1_w2, mix_norm, w_in, gate_bias, q_norm_a, k_norm_a,
                    q_norm_nsa, k_norm_nsa, cmp_pos, cmp_k_w1, cmp_k_w2, cmp_v_w1, cmp_v_w2, out_norm, w_out,
                    ffn2_norm, ffn2_w1, ffn2_w3, ffn2_w2, rel_bias,
                    tm_ffn=1024, tf_ffn=512, tm_proj=1024, tm_out=512)
```

```python
import functools
import math

import numpy as np
import jax
import jax.numpy as jnp
from jax import lax
from jax.experimental import pallas as pl
from jax.experimental.pallas import tpu as pltpu

D_MODEL = 2048
HEAD_DIM = 128
N_HEADS = 16
A_HEADS = 8
NSA_HEADS = 8
NSA_KV_GROUPS = 2
NSA_REP = 4
A_WIDTH = A_HEADS * HEAD_DIM
NSA_WIDTH = NSA_HEADS * HEAD_DIM
KV_WIDTH = NSA_KV_GROUPS * HEAD_DIM
N_BRANCH = 3
D_FF = 5632
DILATED_PATTERNS = ((128, 1), (512, 4), (2048, 16))
BAND = 128
CMP_STRIDE = 16
CMP_LEN = 32
CMP_HIDDEN = 512
SLC_LEN = 64
SLC_TOPK = 16
WIN_LEN = 512
N_BUCKETS = 32
MAX_DISTANCE = 2048
FORCE_SCORE = 1e6
NEG_INF = -1e30
EPS = 1e-6
LOG2E = 1.4426950408889634

LANES = 128
DIL_SPAN = 2048
NSA_TQ = 512
NSA_TK = 128
SEL_TK = 512
CMP_ROWS = 32
BIAS_FAR = 13
BIAS_FUTURE = 14
N_BIAS_TILES = 15
MASK_BIG = 1e30
VMEM_LIMIT = 56 * 1024 * 1024

F32 = jnp.float32
BF16 = jnp.bfloat16

P1_QA, P1_KA, P1_VA, P1_KC, P1_VC, P1_GL = 0, 8, 16, 24, 26, 28
P1_USED = 30
P1_HEADS = 32
P2_QN, P2_KS, P2_VS, P2_KW, P2_VW = 0, 8, 10, 12, 14
P2_HEADS = 16


def _params(sem, flags=None):
    return pltpu.CompilerParams(dimension_semantics=sem, vmem_limit_bytes=VMEM_LIMIT, flags=flags)


def _nt_dot(a, b):
    return lax.dot_general(a, b, (((1,), (1,)), ((), ())), preferred_element_type=F32)


def _dot(a, b):
    return jnp.dot(a, b, preferred_element_type=F32)


def _rel_bucket_np(dist):
    n = np.maximum(dist, 0)
    max_exact = N_BUCKETS // 2
    nf = np.maximum(n, 1).astype(np.float32)
    log_b = max_exact + (np.log(nf / np.float32(max_exact)) / np.float32(math.log(MAX_DISTANCE / max_exact))
                         * np.float32(N_BUCKETS - max_exact)).astype(np.int32)
    return np.where(n < max_exact, n, np.minimum(log_b, N_BUCKETS - 1)).astype(np.int32)


def _dilated_buckets():
    i = np.arange(BAND)[:, None]
    j = np.arange(2 * BAND)[None, :] - BAND
    diff = i - j
    out = []
    for window, dilation in DILATED_PATTERNS:
        mask = (diff >= 0) & (diff <= window // dilation)
        bkt = _rel_bucket_np(diff * dilation)
        out.append(np.where(mask, bkt, -1))
        out.append(np.where(mask & (j >= 0), bkt, -1))
    return np.stack(out).astype(np.int32)


def _nsa_buckets():
    i = np.arange(NSA_TK)[:, None]
    j = np.arange(NSA_TK)[None, :]
    tiles = []
    for m in range(BIAS_FAR + 1):
        dist = NSA_TK * m + i - j
        tiles.append(np.where(dist >= 0, _rel_bucket_np(dist), -1))
    assert (tiles[BIAS_FAR] == N_BUCKETS - 1).all()
    tiles.append(np.full((NSA_TK, NSA_TK), -1))
    assert len(tiles) == N_BIAS_TILES and BIAS_FUTURE == N_BIAS_TILES - 1
    return np.stack(tiles).astype(np.int32)


def _overlap_t_np(n_cmp_pad):
    c_start = np.arange(n_cmp_pad)[None, :] * CMP_STRIDE
    s_start = np.arange(LANES)[:, None] * SLC_LEN
    return ((c_start < s_start + SLC_LEN) & (c_start + CMP_LEN > s_start)).astype(np.float32)


def _block_membership_np(seq):
    k_blk = np.arange(seq)[:, None] // SLC_LEN
    return np.where(k_blk == np.arange(LANES)[None, :], MASK_BIG, 0.0).astype(np.float32)


def _bias_kernel(tbl_ref, bkt_ref, o_ref, *, n_heads, scale, shift_far):
    bkt = bkt_ref[0]
    for h in range(n_heads):
        acc = jnp.full(bkt.shape, NEG_INF, F32)
        base = tbl_ref[N_BUCKETS - 1, h] if shift_far else 0.0
        for b in range(N_BUCKETS):
            acc = jnp.where(bkt == b, (tbl_ref[b, h] - base) * scale, acc)
        o_ref[0, h] = acc


def _expand_bias(table, buckets, scale=1.0, shift_far=False):
    n_t, rows, width = buckets.shape
    nh = table.shape[1]
    return pl.pallas_call(
        functools.partial(_bias_kernel, n_heads=nh, scale=scale, shift_far=shift_far),
        grid=(n_t,),
        in_specs=[pl.BlockSpec(memory_space=pltpu.SMEM),
                  pl.BlockSpec((1, rows, width), lambda t: (t, 0, 0))],
        out_specs=pl.BlockSpec((1, nh, rows, width), lambda t: (t, 0, 0, 0)),
        out_shape=jax.ShapeDtypeStruct((n_t, nh, rows, width), F32),
        compiler_params=_params(("arbitrary",)),
        name="bias_expand",
    )(table, buckets)


def _ffn_kernel(x_ref, g_ref, w1_ref, w3_ref, w2_ref, o_ref, n_scr):
    j = pl.program_id(1)

    @pl.when(j == 0)
    def _():
        x = x_ref[...]
        ms = jnp.mean(x * x, axis=-1, keepdims=True)
        n_scr[...] = (x * lax.rsqrt(ms + EPS) * g_ref[...]).astype(BF16)
        o_ref[...] = jnp.zeros_like(o_ref)

    n = n_scr[...]
    h1 = _dot(n, w1_ref[...])
    h3 = _dot(n, w3_ref[...])
    a = (h1 * jax.nn.sigmoid(h1) * h3).astype(BF16)
    o_ref[...] += _dot(a, w2_ref[...])

    @pl.when(j == pl.num_programs(1) - 1)
    def _():
        o_ref[...] = x_ref[...] + 0.5 * o_ref[...]


def _ffn(x2, gain, w1, w3, w2, *, tm, tf):
    n_tok, d = x2.shape
    f = w1.shape[1]
    return pl.pallas_call(
        _ffn_kernel,
        grid=(n_tok // tm, f // tf),
        in_specs=[pl.BlockSpec((tm, d), lambda i, j: (i, 0)),
                  pl.BlockSpec((1, d), lambda i, j: (0, 0)),
                  pl.BlockSpec((d, tf), lambda i, j: (0, j)),
                  pl.BlockSpec((d, tf), lambda i, j: (0, j)),
                  pl.BlockSpec((tf, d), lambda i, j: (j, 0))],
        out_specs=pl.BlockSpec((tm, d), lambda i, j: (i, 0)),
        out_shape=jax.ShapeDtypeStruct((n_tok, d), F32),
        scratch_shapes=[pltpu.VMEM((tm, d), BF16)],
        compiler_params=_params(("arbitrary", "arbitrary")),
        name="ffn",
    )(x2, gain.reshape(1, d), w1, w3, w2)


def _proj_kernel(x_ref, g_ref, w_ref, cg_ref, cf_ref, o1_ref, o2_ref, n_scr, acc_scr, *, n1):
    j = pl.program_id(1)
    last = pl.num_programs(1) - 1

    def matmul():
        acc_scr[...] = _dot(n_scr[...], w_ref[...])

    def epilogue(o_ref):
        for h in range(acc_scr.shape[1] // LANES):
            cols = slice(h * LANES, (h + 1) * LANES)
            a = acc_scr[:, cols]
            ms = jnp.mean(a * a, axis=-1, keepdims=True)
            inv = jnp.where(cf_ref[:, cols] > 0.0, lax.rsqrt(ms + EPS), 1.0)
            o_ref[:, cols] = (a * inv * cg_ref[:, cols]).astype(o_ref.dtype)

    @pl.when(j == 0)
    def _():
        x = x_ref[...]
        ms = jnp.mean(x * x, axis=-1, keepdims=True)
        n_scr[...] = (x * lax.rsqrt(ms + EPS) * g_ref[...]).astype(BF16)
        matmul()

    @pl.when(jnp.logical_and(j > 0, j <= n1))
    def _():
        epilogue(o1_ref)
        matmul()

    @pl.when(jnp.logical_and(j > n1, j < last))
    def _():
        epilogue(o2_ref)
        matmul()

    @pl.when(j == last)
    def _():
        epilogue(o2_ref)


def _proj(x2, gain, w, col_gain, col_flag, cols1, *, tm, tn):
    n_tok, d = x2.shape
    n_cols = w.shape[1]
    n1 = cols1 // tn
    n2 = (n_cols - cols1) // tn
    assert n1 * tn == cols1 and (n1 + n2) * tn == n_cols
    n_j = n1 + n2
    prev = lambda j: jnp.maximum(j - 1, 0)
    return pl.pallas_call(
        functools.partial(_proj_kernel, n1=n1),
        grid=(n_tok // tm, n_j + 1),
        in_specs=[pl.BlockSpec((tm, d), lambda i, j: (i, 0)),
                  pl.BlockSpec((1, d), lambda i, j: (0, 0)),
                  pl.BlockSpec((d, tn), lambda i, j: (0, jnp.minimum(j, n_j - 1))),
                  pl.BlockSpec((1, tn), lambda i, j: (0, prev(j))),
                  pl.BlockSpec((1, tn), lambda i, j: (0, prev(j)))],
        out_specs=[pl.BlockSpec((tm, tn), lambda i, j: (i, jnp.minimum(prev(j), n1 - 1))),
                   pl.BlockSpec((tm, tn), lambda i, j: (i, jnp.maximum(prev(j) - n1, 0)))],
        out_shape=[jax.ShapeDtypeStruct((n_tok, cols1), F32),
                   jax.ShapeDtypeStruct((n_tok, n_cols - cols1), BF16)],
        scratch_shapes=[pltpu.VMEM((tm, d), BF16), pltpu.VMEM((tm, tn), F32)],
        compiler_params=_params(("arbitrary", "arbitrary")),
        name="in_proj",
    )(x2, gain.reshape(1, d), w, col_gain.reshape(1, n_cols), col_flag.reshape(1, n_cols))


def _dilated_kernel(q_ref, kc_ref, kp_ref, vc_ref, vp_ref, bias_ref, o_ref,
                    k_scr, v_scr, out_scr, lse_scr):
    n = pl.program_id(2)
    k_scr[0:DIL_SPAN, :] = kp_ref[0]
    k_scr[DIL_SPAN:2 * DIL_SPAN, :] = kc_ref[0]
    v_scr[0:DIL_SPAN, :] = vp_ref[0]
    v_scr[DIL_SPAN:2 * DIL_SPAN, :] = vc_ref[0]

    for p, (_, dil) in enumerate(DILATED_PATTERNS):
        span = BAND * dil

        def unit(u, carry, p=p, dil=dil, span=span):
            s = u // dil
            r = u - s * dil
            base = s * span + r
            first = jnp.logical_and(n == 0, s == 0).astype(jnp.int32)
            q = q_ref[0, pl.ds(base, BAND, stride=dil), :].astype(BF16)
            kk = k_scr[pl.ds(DIL_SPAN + base - span, 2 * BAND, stride=dil), :].astype(BF16)
            vv = v_scr[pl.ds(DIL_SPAN + base - span, 2 * BAND, stride=dil), :].astype(BF16)
            lg = _nt_dot(q, kk) + bias_ref[p, first, 0]
            m = jnp.max(lg, axis=-1, keepdims=True)
            e = jnp.exp2(lg - m).astype(BF16)
            acc = _dot(e, jnp.concatenate([vv, jnp.ones((2 * BAND, LANES), BF16)], axis=1))
            ssum = acc[:, LANES:]
            out_scr[p, pl.ds(base, BAND, stride=dil), :] = acc[:, :LANES] / ssum
            lse_scr[p, pl.ds(base, BAND, stride=dil), :] = m + jnp.log2(ssum)
            return carry

        lax.fori_loop(0, DIL_SPAN // BAND, unit, 0, unroll=8)

    chunk = 256

    def combine(c, carry):
        rows = pl.ds(pl.multiple_of(c * chunk, chunk), chunk)
        l0, l1, l2 = lse_scr[0, rows, :], lse_scr[1, rows, :], lse_scr[2, rows, :]
        big = jnp.maximum(jnp.maximum(l0, l1), l2)
        e0, e1, e2 = jnp.exp2(l0 - big), jnp.exp2(l1 - big), jnp.exp2(l2 - big)
        num = e0 * out_scr[0, rows, :] + e1 * out_scr[1, rows, :] + e2 * out_scr[2, rows, :]
        o_ref[0, rows, :] = num / (e0 + e1 + e2)
        return carry

    lax.fori_loop(0, DIL_SPAN // chunk, combine, 0)


def _dilated(p1, bias, batch, seq):
    blk = (1, DIL_SPAN, LANES)
    prev = lambda nn: jnp.maximum(nn - 1, 0)
    return pl.pallas_call(
        _dilated_kernel,
        grid=(batch, A_HEADS, seq // DIL_SPAN),
        in_specs=[pl.BlockSpec(blk, lambda b, h, nn: (b, nn, P1_QA + h)),
                  pl.BlockSpec(blk, lambda b, h, nn: (b, nn, P1_KA + h)),
                  pl.BlockSpec(blk, lambda b, h, nn: (b, prev(nn), P1_KA + h)),
                  pl.BlockSpec(blk, lambda b, h, nn: (b, nn, P1_VA + h)),
                  pl.BlockSpec(blk, lambda b, h, nn: (b, prev(nn), P1_VA + h)),
                  pl.BlockSpec((3, 2, 1, BAND, 2 * BAND), lambda b, h, nn: (0, 0, h, 0, 0))],
        out_specs=pl.BlockSpec(blk, lambda b, h, nn: (b, nn, h)),
        out_shape=jax.ShapeDtypeStruct((batch, seq, A_WIDTH), F32),
        scratch_shapes=[pltpu.VMEM((2 * DIL_SPAN, LANES), F32),
                        pltpu.VMEM((2 * DIL_SPAN, LANES), F32),
                        pltpu.VMEM((3, DIL_SPAN, LANES), F32),
                        pltpu.VMEM((3, DIL_SPAN, LANES), F32)],
        compiler_params=_params(("arbitrary", "arbitrary", "arbitrary")),
        name="dilated_attn",
    )(p1, p1, p1, p1, p1, bias)


def _compress_kernel(t_ref, pos_ref, w1_ref, w2_ref, gain_ref, flag_ref, o_ref, lo_scr, hi_scr):
    n_blk = o_ref.shape[3]
    lo_scr[...] = jnp.zeros_like(lo_scr)
    hi_scr[...] = jnp.zeros_like(hi_scr)
    for i in range(CMP_STRIDE):
        rows = t_ref[0, pl.ds(i, n_blk, stride=CMP_STRIDE), :]
        a = (rows + pos_ref[i:i + 1, :]).astype(BF16)
        lo_scr[...] += _dot(a, w1_ref[0, i * HEAD_DIM:(i + 1) * HEAD_DIM, :])
        b = (rows + pos_ref[CMP_STRIDE + i:CMP_STRIDE + i + 1, :]).astype(BF16)
        hi_scr[...] += _dot(b, w1_ref[0, (CMP_STRIDE + i) * HEAD_DIM:(CMP_STRIDE + i + 1) * HEAD_DIM, :])
    hid = lo_scr[...] + pltpu.roll(hi_scr[...], n_blk - 1, 0)
    o = _dot(jax.nn.gelu(hid).astype(BF16), w2_ref[0])
    ms = jnp.mean(o * o, axis=-1, keepdims=True)
    inv = jnp.where(flag_ref[0] > 0.0, lax.rsqrt(ms + EPS), 1.0)
    o_ref[0, 0, 0] = o * inv * gain_ref[0]


def _compress(p1, pos, w1, w2, gain, flag, batch, seq):
    n_blk = seq // CMP_STRIDE
    return pl.pallas_call(
        _compress_kernel,
        grid=(batch, 2, NSA_KV_GROUPS),
        in_specs=[pl.BlockSpec((1, seq, LANES), lambda b, kv, g: (b, 0, P1_KC + 2 * kv + g)),
                  pl.BlockSpec((CMP_LEN, HEAD_DIM), lambda b, kv, g: (0, 0)),
                  pl.BlockSpec((1, CMP_LEN * HEAD_DIM, CMP_HIDDEN), lambda b, kv, g: (kv, 0, 0)),
                  pl.BlockSpec((1, CMP_HIDDEN, HEAD_DIM), lambda b, kv, g: (kv, 0, 0)),
                  pl.BlockSpec((1, 1, HEAD_DIM), lambda b, kv, g: (kv, 0, 0)),
                  pl.BlockSpec((1, 1, HEAD_DIM), lambda b, kv, g: (kv, 0, 0))],
        out_specs=pl.BlockSpec((1, 1, 1, n_blk, HEAD_DIM), lambda b, kv, g: (b, kv, g, 0, 0)),
        out_shape=jax.ShapeDtypeStruct((batch, 2, NSA_KV_GROUPS, n_blk, HEAD_DIM), F32),
        scratch_shapes=[pltpu.VMEM((n_blk, CMP_HIDDEN), F32), pltpu.VMEM((n_blk, CMP_HIDDEN), F32)],
        compiler_params=_params(("arbitrary", "arbitrary", "arbitrary")),
        name="nsa_compress",
    )(p1, pos, w1, w2, gain, flag)


def _nsa_kernel(q_ref, ks_ref, vs_ref, kw_ref, vw_ref, kc_ref, vc_ref, gl_ref, gb_ref,
                bias_ref, ovl_ref, et_ref, o_ref, m_scr, acc_scr, s_scr, p_scr, ocmp_scr, imp_scr,
                lc_scr, pcb_scr, split_scr):
    qi = pl.program_id(2)
    tq, rep = NSA_TQ, NSA_REP
    n_cmp = kc_ref.shape[3]

    q2 = q_ref[0]
    q_heads = [q2[:, r * HEAD_DIM:(r + 1) * HEAD_DIM] for r in range(rep)]
    t_col = qi * tq + lax.broadcasted_iota(jnp.int32, (tq, 1), 0)

    def compressed(width):
        cmp_end = lax.broadcasted_iota(jnp.int32, (1, width), 1) * CMP_STRIDE + (CMP_LEN - 1)
        kcb = kc_ref[0, 0, 0, 0:width, :].astype(BF16)
        vcb = vc_ref[0, 0, 0, 0:width, :].astype(BF16)
        for r in range(rep):
            lc_scr[r, :, 0:width] = _nt_dot(q_heads[r], kcb)
        for c in range(tq // CMP_ROWS):
            rows = slice(c * CMP_ROWS, (c + 1) * CMP_ROWS)
            cmask = cmp_end <= t_col[rows]
            pc_sum = jnp.zeros((CMP_ROWS, width), F32)
            for r in range(rep):
                lc = jnp.where(cmask, lc_scr[r, rows, 0:width], NEG_INF)
                mc = jnp.max(lc, axis=-1, keepdims=True)
                pc = jnp.where(cmask, jnp.exp2(lc - mc), 0.0)
                pc = pc / jnp.maximum(jnp.sum(pc, axis=-1, keepdims=True), 1e-30)
                pcb_scr[r, rows, 0:width] = pc.astype(BF16)
                pc_sum = pc_sum + pc
            hi = pc_sum.astype(BF16)
            rem = pc_sum - hi.astype(F32)
            mid = rem.astype(BF16)
            split_scr[0, rows, 0:width] = hi
            split_scr[1, rows, 0:width] = mid
            split_scr[2, rows, 0:width] = (rem - mid.astype(F32)).astype(BF16)
        for r in range(rep):
            ocmp_scr[r] = _dot(pcb_scr[r, :, 0:width], vcb)
        ovl_t = ovl_ref[:, 0:width]
        imp_scr[...] = (_nt_dot(ovl_t, split_scr[0, :, 0:width]) + _nt_dot(ovl_t, split_scr[1, :, 0:width])
                        + _nt_dot(ovl_t, split_scr[2, :, 0:width]))

    n_grp = n_cmp // LANES
    need = jnp.minimum(((qi + 1) * tq - CMP_LEN) // (CMP_STRIDE * LANES) + 1, n_grp)
    for k in range(1, n_grp + 1):
        pl.when(need == k)(functools.partial(compressed, k * LANES))
    o_cmp = [ocmp_scr[r] for r in range(rep)]
    imp_t = imp_scr[...]

    n_rb = tq // NSA_TK
    n_wt = WIN_LEN // NSA_TK + 1
    w_keys = n_wt * NSA_TK
    win_parts = [[] for _ in range(rep)]
    for a in range(n_rb):
        rows = slice(a * NSA_TK, (a + 1) * NSA_TK)
        blk_q = qi * n_rb + a
        j0 = jnp.maximum(blk_q - (n_wt - 1), 0)
        w_rows = pl.ds(pl.multiple_of(j0 * NSA_TK, NSA_TK), w_keys)
        kw_tile = kw_ref[0, w_rows, :]
        vw_aug = jnp.concatenate([vw_ref[0, w_rows, :], jnp.ones((w_keys, HEAD_DIM), BF16)], axis=1)
        dist = t_col[rows] - (j0 * NSA_TK + lax.broadcasted_iota(jnp.int32, (1, w_keys), 1))
        wmask = jnp.where(dist >= 0, dist, WIN_LEN) < WIN_LEN
        tiles = [jnp.clip(blk_q - j0 - i, 0, BIAS_FAR) for i in range(n_wt)]
        for r in range(rep):
            s = _nt_dot(q_heads[r][rows], kw_tile) + jnp.concatenate([bias_ref[t, r] for t in tiles], axis=-1)
            s = jnp.where(wmask, s, NEG_INF)
            p = jnp.exp2(s - jnp.max(s, axis=-1, keepdims=True)).astype(BF16)
            acc = _dot(p, vw_aug)
            win_parts[r].append(acc[:, :HEAD_DIM] / acc[:, HEAD_DIM:])
    o_win = [jnp.concatenate(parts, axis=0) for parts in win_parts]

    blk = lax.broadcasted_iota(jnp.int32, (LANES, 1), 0)
    blk_f = blk.astype(F32)
    t_row = qi * tq + lax.broadcasted_iota(jnp.int32, (1, tq), 1)
    cur = t_row >> (SLC_LEN.bit_length() - 1)
    valid = blk * SLC_LEN <= t_row
    forced = (blk == 0) | (blk == cur) | (blk == cur - 1)
    score = jnp.where(valid, jnp.where(forced, FORCE_SCORE, imp_t), -1.0)
    sel_t = jnp.zeros((LANES, tq), F32)
    for _ in range(SLC_TOPK):
        best = jnp.max(score, axis=0, keepdims=True)
        idx = jnp.min(jnp.where(score == best, blk_f, float(LANES)), axis=0, keepdims=True)
        hit = blk_f == idx
        sel_t = jnp.where(hit, 1.0, sel_t)
        score = jnp.where(hit, -3e38, score)
    unsel = (sel_t.T - 1.0).astype(BF16)
    lhs = [jnp.concatenate([q_heads[r], unsel], axis=1) for r in range(rep)]

    ones_v = jnp.ones((SEL_TK, HEAD_DIM), BF16)

    sub = SEL_TK // NSA_TK
    n_tiles = ((qi + 1) * n_rb - 1) // sub + 1

    def key_rows(jj):
        return pl.ds(pl.multiple_of(jj * SEL_TK, SEL_TK), SEL_TK)

    def logits_stage(jj, heads):
        rows = key_rows(jj)
        rhs = jnp.concatenate([ks_ref[0, rows, :], et_ref[rows, :]], axis=1)
        for r in heads:
            s_scr[r] = _nt_dot(lhs[r], rhs)

    def softmax_stage(jj, heads, far):
        for a in range(n_rb):
            rows = slice(a * NSA_TK, (a + 1) * NSA_TK)
            u = qi * n_rb + a - sub * jj
            tiles = [jnp.where(u < i, BIAS_FUTURE, jnp.minimum(u - i, BIAS_FAR)) for i in range(sub)]
            for r in heads:
                s = s_scr[r, rows, :]
                if not far:
                    s = s + jnp.concatenate([bias_ref[t, r] for t in tiles], axis=-1)
                m_old = m_scr[1 - (jj & 1), r, rows, :]
                m_new = jnp.maximum(m_old, jnp.max(s, axis=-1, keepdims=True))
                p_scr[r, rows, :] = jnp.exp2(s - jnp.concatenate([m_new] * sub, axis=1)).astype(BF16)
                m_scr[jj & 1, r, rows, :] = m_new

    def pv_stage(it, heads):
        v_aug = jnp.concatenate([vs_ref[0, key_rows(jnp.maximum(it - 1, 0)), :], ones_v], axis=1)
        for r in heads:
            alpha = jnp.exp2(m_scr[it & 1, r] - m_scr[1 - (it & 1), r])
            acc_scr[r] = jnp.concatenate([alpha, alpha], axis=1) * acc_scr[r] + _dot(p_scr[r], v_aug)

    all_heads = range(rep)
    m_scr[...] = jnp.full(m_scr.shape, NEG_INF, F32)
    acc_scr[...] = jnp.zeros_like(acc_scr)
    p_scr[...] = jnp.zeros_like(p_scr)
    logits_stage(0, all_heads)

    def sel_step(far, it, carry):
        pv_stage(it, all_heads)
        softmax_stage(it, all_heads, far)
        logits_stage(jnp.minimum(it + 1, n_tiles - 1), all_heads)
        return carry

    n_far = jnp.clip((qi * n_rb - BIAS_FAR + 1) // sub, 0, n_tiles)
    lax.fori_loop(0, n_far, functools.partial(sel_step, True), 0)
    lax.fori_loop(n_far, n_tiles, functools.partial(sel_step, False), 0)
    pv_stage(n_tiles, all_heads)

    gates = jax.nn.sigmoid(gl_ref[0] + gb_ref[0])
    o_slc = []
    for r in range(rep):
        acc = acc_scr[r]
        o_slc.append(acc[:, :HEAD_DIM] / acc[:, HEAD_DIM:])
    for r in range(rep):
        c = r * N_BRANCH
        out = (gates[:, c:c + 1] * o_cmp[r] + gates[:, c + 1:c + 2] * o_slc[r]
               + gates[:, c + 2:c + 3] * o_win[r])
        o_ref[0, :, r * HEAD_DIM:(r + 1) * HEAD_DIM] = out


def _nsa(p1, p2, kcv, gate_bias, bias, overlap_t, membership, batch, seq):
    tq = NSA_TQ
    n_cmp = seq // CMP_STRIDE
    kv_blk = (1, seq, LANES)
    grp_w = NSA_REP * HEAD_DIM
    once = pl.Buffered(1)
    return pl.pallas_call(
        _nsa_kernel,
        grid=(batch, NSA_KV_GROUPS, seq // tq),
        in_specs=[pl.BlockSpec((1, tq, grp_w), lambda b, g, i: (b, i, g)),
                  pl.BlockSpec(kv_blk, lambda b, g, i: (b, 0, P2_KS + g), pipeline_mode=once),
                  pl.BlockSpec(kv_blk, lambda b, g, i: (b, 0, P2_VS + g), pipeline_mode=once),
                  pl.BlockSpec(kv_blk, lambda b, g, i: (b, 0, P2_KW + g), pipeline_mode=once),
                  pl.BlockSpec(kv_blk, lambda b, g, i: (b, 0, P2_VW + g), pipeline_mode=once),
                  pl.BlockSpec((1, 1, 1, n_cmp, HEAD_DIM), lambda b, g, i: (b, 0, g, 0, 0)),
                  pl.BlockSpec((1, 1, 1, n_cmp, HEAD_DIM), lambda b, g, i: (b, 1, g, 0, 0)),
                  pl.BlockSpec((1, tq, LANES), lambda b, g, i: (b, i, P1_GL + g)),
                  pl.BlockSpec((1, 1, LANES), lambda b, g, i: (g, 0, 0)),
                  pl.BlockSpec((N_BIAS_TILES, NSA_REP, NSA_TK, NSA_TK), lambda b, g, i: (0, g, 0, 0),
                               pipeline_mode=once),
                  pl.BlockSpec((LANES, n_cmp), lambda b, g, i: (0, 0)),
                  pl.BlockSpec((seq, LANES), lambda b, g, i: (0, 0))],
        out_specs=pl.BlockSpec((1, tq, grp_w), lambda b, g, i: (b, i, g)),
        out_shape=jax.ShapeDtypeStruct((batch, seq, NSA_WIDTH), F32),
        scratch_shapes=[pltpu.VMEM((2, NSA_REP, tq, LANES), F32),
                        pltpu.VMEM((NSA_REP, tq, 2 * HEAD_DIM), F32),
                        pltpu.VMEM((NSA_REP, tq, SEL_TK), F32),
                        pltpu.VMEM((NSA_REP, tq, SEL_TK), BF16),
                        pltpu.VMEM((NSA_REP, tq, HEAD_DIM), F32),
                        pltpu.VMEM((LANES, tq), F32),
                        pltpu.VMEM((NSA_REP, tq, n_cmp), F32),
                        pltpu.VMEM((NSA_REP, tq, n_cmp), BF16),
                        pltpu.VMEM((3, tq, n_cmp), BF16)],
        compiler_params=_params(("arbitrary", "arbitrary", "arbitrary")),
        name="nsa_attn",
    )(p2, p2, p2, p2, p2, kcv, kcv, p1, gate_bias, bias, overlap_t, membership)


def _out_kernel(x_ref, oa_ref, on_ref, ga_ref, gn_ref, w_ref, o_ref):
    def norm(t, g):
        ms = jnp.mean(t * t, axis=-1, keepdims=True)
        return (t * lax.rsqrt(ms + EPS) * g).astype(BF16)

    ya = norm(oa_ref[...], ga_ref[...])
    yn = norm(on_ref[...], gn_ref[...])
    y = _dot(ya, w_ref[0:A_WIDTH, :]) + _dot(yn, w_ref[A_WIDTH:A_WIDTH + NSA_WIDTH, :])
    o_ref[...] = x_ref[...] + y


def _out_proj(x2, oa, on, gain, w, *, tm):
    n_tok, d = x2.shape
    return pl.pallas_call(
        _out_kernel,
        grid=(n_tok // tm,),
        in_specs=[pl.BlockSpec((tm, d), lambda i: (i, 0)),
                  pl.BlockSpec((tm, A_WIDTH), lambda i: (i, 0)),
                  pl.BlockSpec((tm, NSA_WIDTH), lambda i: (i, 0)),
                  pl.BlockSpec((1, A_WIDTH), lambda i: (0, 0)),
                  pl.BlockSpec((1, NSA_WIDTH), lambda i: (0, 0)),
                  pl.BlockSpec((d, d), lambda i: (0, 0))],
        out_specs=pl.BlockSpec((tm, d), lambda i: (i, 0)),
        out_shape=jax.ShapeDtypeStruct((n_tok, d), F32),
        compiler_params=_params(("arbitrary",)),
        name="out_proj",
    )(x2, oa, on, gain[:A_WIDTH].reshape(1, A_WIDTH), gain[A_WIDTH:].reshape(1, NSA_WIDTH), w)


def _pack_w_in(w_in, gate_bias, q_norm_a, k_norm_a, q_norm_nsa, k_norm_nsa):
    scale = HEAD_DIM ** -0.5
    sizes = (A_WIDTH,) * 3 + (NSA_WIDTH,) + (KV_WIDTH,) * 6 + (N_BRANCH * NSA_HEADS,)
    offs = np.concatenate([[0], np.cumsum(sizes)])
    qa, ka, va, qn, kc, vc, k_s, v_s, k_w, v_w, gl = [w_in[:, offs[i]:offs[i + 1]] for i in range(11)]
    per_grp = NSA_REP * N_BRANCH
    pad = jnp.zeros((w_in.shape[0], LANES - per_grp), w_in.dtype)
    gl_cols = [jnp.concatenate([gl[:, g * per_grp:(g + 1) * per_grp], pad], axis=1) for g in range(NSA_KV_GROUPS)]
    n_pad = P1_HEADS - P1_USED
    w_pad = jnp.zeros((w_in.shape[0], n_pad * LANES), w_in.dtype)
    w = jnp.concatenate([qa, ka, va, kc, vc] + gl_cols + [w_pad, qn, k_s, v_s, k_w, v_w], axis=1).astype(BF16)

    ones = jnp.ones((HEAD_DIM,), F32)
    tile = lambda v, n: jnp.tile(v, n)
    gain = jnp.concatenate([tile(q_norm_a * (scale * LOG2E), A_HEADS), tile(k_norm_a, A_HEADS),
                            tile(ones, A_HEADS + 6 + n_pad),
                            tile(q_norm_nsa * (scale * LOG2E), NSA_HEADS), tile(k_norm_nsa, 2), tile(ones, 2),
                            tile(k_norm_nsa, 2), tile(ones, 2)])
    flag = jnp.concatenate([jnp.ones((2 * A_WIDTH,), F32), jnp.zeros((A_WIDTH + (6 + n_pad) * HEAD_DIM,), F32),
                            jnp.ones((NSA_WIDTH + KV_WIDTH,), F32), jnp.zeros((KV_WIDTH,), F32),
                            jnp.ones((KV_WIDTH,), F32), jnp.zeros((KV_WIDTH,), F32)])
    gb = jnp.stack([jnp.concatenate([gate_bias[g * per_grp:(g + 1) * per_grp],
                                     jnp.zeros((LANES - per_grp,), F32)]) for g in range(NSA_KV_GROUPS)])
    return w, gain, flag, gb.reshape(NSA_KV_GROUPS, 1, LANES)


def _forward(x, ffn1_norm, ffn1_w1, ffn1_w3, ffn1_w2, mix_norm, w_in, gate_bias, q_norm_a, k_norm_a,
             q_norm_nsa, k_norm_nsa, cmp_pos, cmp_k_w1, cmp_k_w2, cmp_v_w1, cmp_v_w2, out_norm, w_out,
             ffn2_norm, ffn2_w1, ffn2_w3, ffn2_w2, rel_bias, *, tm_ffn, tf_ffn, tm_proj, tm_out):
    batch, seq, d = x.shape
    depth = ffn1_w1.shape[0]
    n_tok = batch * seq
    assert seq % DIL_SPAN == 0 and seq // SLC_LEN <= LANES
    assert seq % NSA_TQ == 0 and seq % SEL_TK == 0 and (seq // CMP_STRIDE) % LANES == 0
    assert n_tok % max(tm_ffn, tm_proj, tm_out) == 0 and d == D_MODEL

    bias_a = _expand_bias(rel_bias[:, :A_HEADS], jnp.asarray(_dilated_buckets()), scale=LOG2E)
    bias_a = bias_a.reshape(3, 2, A_HEADS, BAND, 2 * BAND)
    bias_n = _expand_bias(rel_bias[:, A_HEADS:], jnp.asarray(_nsa_buckets()),
                          scale=LOG2E, shift_far=True)
    overlap_t = jnp.asarray(_overlap_t_np(seq // CMP_STRIDE)).astype(BF16)
    membership = jnp.asarray(_block_membership_np(seq)).astype(BF16)

    x2 = x.reshape(n_tok, d)
    for l in range(depth):
        x2 = _ffn(x2, ffn1_norm[l], ffn1_w1[l].astype(BF16), ffn1_w3[l].astype(BF16),
                  ffn1_w2[l].astype(BF16), tm=tm_ffn, tf=tf_ffn)

        w_p, g_p, f_p, gb = _pack_w_in(w_in[l], gate_bias[l], q_norm_a[l], k_norm_a[l],
                                       q_norm_nsa[l], k_norm_nsa[l])
        p1, p2 = _proj(x2, mix_norm[l], w_p, g_p, f_p, P1_HEADS * LANES, tm=tm_proj, tn=1024)
        p1 = p1.reshape(batch, seq, P1_HEADS * LANES)
        p2 = p2.reshape(batch, seq, P2_HEADS * LANES)

        o_a = _dilated(p1, bias_a, batch, seq)

        cmp_w1 = jnp.stack([cmp_k_w1[l], cmp_v_w1[l]]).astype(BF16)
        cmp_w2 = jnp.stack([cmp_k_w2[l], cmp_v_w2[l]]).astype(BF16)
        cmp_gain = jnp.stack([k_norm_nsa[l], jnp.ones((HEAD_DIM,), F32)]).reshape(2, 1, HEAD_DIM)
        cmp_flag = jnp.stack([jnp.ones((HEAD_DIM,), F32), jnp.zeros((HEAD_DIM,), F32)]).reshape(2, 1, HEAD_DIM)
        kcv = _compress(p1, cmp_pos[l], cmp_w1, cmp_w2, cmp_gain, cmp_flag, batch, seq)

        o_n = _nsa(p1, p2, kcv, gb, bias_n, overlap_t, membership, batch, seq)

        x2 = _out_proj(x2, o_a.reshape(n_tok, A_WIDTH), o_n.reshape(n_tok, NSA_WIDTH), out_norm[l],
                       w_out[l].astype(BF16), tm=tm_out)

        x2 = _ffn(x2, ffn2_norm[l], ffn2_w1[l].astype(BF16), ffn2_w3[l].astype(BF16),
                  ffn2_w2[l].astype(BF16), tm=tm_ffn, tf=tf_ffn)
    return x2.reshape(batch, seq, d)


def kernel(x, ffn1_norm, ffn1_w1, ffn1_w3, ffn1_w2, mix_norm, w_in, gate_bias, q_norm_a, k_norm_a, q_norm_nsa, k_norm_nsa, cmp_pos, cmp_k_w1, cmp_k_w2, cmp_v_w1, cmp_v_w2, out_norm, w_out, ffn2_norm, ffn2_w1, ffn2_w3, ffn2_w2, rel_bias):
    return _forward(x, ffn1_norm, ffn1_w1, ffn1_w3, ffn1_w2, mix_norm, w_in, gate_bias, q_norm_a, k_norm_a,
                    q_norm_nsa, k_norm_nsa, cmp_pos, cmp_k_w1, cmp_k_w2, cmp_v_w1, cmp_v_w2, out_norm, w_out,
                    ffn2_norm, ffn2_w1, ffn2_w3, ffn2_w2, rel_bias,
                    tm_ffn=1024, tf_ffn=512, tm_proj=1024, tm_out=512)
```

```python
import functools
import math

import numpy as np
import jax
import jax.numpy as jnp
from jax import lax
from jax.experimental import pallas as pl
from jax.experimental.pallas import tpu as pltpu

D_MODEL = 2048
HEAD_DIM = 128
N_HEADS = 16
A_HEADS = 8
NSA_HEADS = 8
NSA_KV_GROUPS = 2
NSA_REP = 4
A_WIDTH = A_HEADS * HEAD_DIM
NSA_WIDTH = NSA_HEADS * HEAD_DIM
KV_WIDTH = NSA_KV_GROUPS * HEAD_DIM
N_BRANCH = 3
D_FF = 5632
DILATED_PATTERNS = ((128, 1), (512, 4), (2048, 16))
BAND = 128
CMP_STRIDE = 16
CMP_LEN = 32
CMP_HIDDEN = 512
SLC_LEN = 64
SLC_TOPK = 16
WIN_LEN = 512
N_BUCKETS = 32
MAX_DISTANCE = 2048
FORCE_SCORE = 1e6
NEG_INF = -1e30
EPS = 1e-6
LOG2E = 1.4426950408889634

LANES = 128
DIL_SPAN = 2048
NSA_TQ = 512
NSA_TK = 128
SEL_TK = 512
CMP_ROWS = 32
BIAS_FAR = 13
BIAS_FUTURE = 14
N_BIAS_TILES = 15
MASK_BIG = 1e30
VMEM_LIMIT = 56 * 1024 * 1024

F32 = jnp.float32
BF16 = jnp.bfloat16

P1_QA, P1_KA, P1_VA, P1_KC, P1_VC, P1_GL = 0, 8, 16, 24, 26, 28
P1_USED = 30
P1_HEADS = 32
P2_QN, P2_KS, P2_VS, P2_KW, P2_VW = 0, 8, 10, 12, 14
P2_HEADS = 16


def _params(sem, flags=None):
    return pltpu.CompilerParams(dimension_semantics=sem, vmem_limit_bytes=VMEM_LIMIT, flags=flags)


def _nt_dot(a, b):
    return lax.dot_general(a, b, (((1,), (1,)), ((), ())), preferred_element_type=F32)


def _dot(a, b):
    return jnp.dot(a, b, preferred_element_type=F32)


def _rel_bucket_np(dist):
    n = np.maximum(dist, 0)
    max_exact = N_BUCKETS // 2
    nf = np.maximum(n, 1).astype(np.float32)
    log_b = max_exact + (np.log(nf / np.float32(max_exact)) / np.float32(math.log(MAX_DISTANCE / max_exact))
                         * np.float32(N_BUCKETS - max_exact)).astype(np.int32)
    return np.where(n < max_exact, n, np.minimum(log_b, N_BUCKETS - 1)).astype(np.int32)


def _dilated_buckets():
    i = np.arange(BAND)[:, None]
    j = np.arange(2 * BAND)[None, :] - BAND
    diff = i - j
    out = []
    for window, dilation in DILATED_PATTERNS:
        mask = (diff >= 0) & (diff <= window // dilation)
        bkt = _rel_bucket_np(diff * dilation)
        out.append(np.where(mask, bkt, -1))
        out.append(np.where(mask & (j >= 0), bkt, -1))
    return np.stack(out).astype(np.int32)


def _nsa_buckets():
    i = np.arange(NSA_TK)[:, None]
    j = np.arange(NSA_TK)[None, :]
    tiles = []
    for m in range(BIAS_FAR + 1):
        dist = NSA_TK * m + i - j
        tiles.append(np.where(dist >= 0, _rel_bucket_np(dist), -1))
    assert (tiles[BIAS_FAR] == N_BUCKETS - 1).all()
    tiles.append(np.full((NSA_TK, NSA_TK), -1))
    assert len(tiles) == N_BIAS_TILES and BIAS_FUTURE == N_BIAS_TILES - 1
    return np.stack(tiles).astype(np.int32)


def _overlap_t_np(n_cmp_pad):
    c_start = np.arange(n_cmp_pad)[None, :] * CMP_STRIDE
    s_start = np.arange(LANES)[:, None] * SLC_LEN
    return ((c_start < s_start + SLC_LEN) & (c_start + CMP_LEN > s_start)).astype(np.float32)


def _block_membership_np(seq):
    k_blk = np.arange(seq)[:, None] // SLC_LEN
    return np.where(k_blk == np.arange(LANES)[None, :], MASK_BIG, 0.0).astype(np.float32)


def _bias_kernel(tbl_ref, bkt_ref, o_ref, *, n_heads, scale, shift_far):
    bkt = bkt_ref[0]
    for h in range(n_heads):
        acc = jnp.full(bkt.shape, NEG_INF, F32)
        base = tbl_ref[N_BUCKETS - 1, h] if shift_far else 0.0
        for b in range(N_BUCKETS):
            acc = jnp.where(bkt == b, (tbl_ref[b, h] - base) * scale, acc)
        o_ref[0, h] = acc


def _expand_bias(table, buckets, scale=1.0, shift_far=False):
    n_t, rows, width = buckets.shape
    nh = table.shape[1]
    return pl.pallas_call(
        functools.partial(_bias_kernel, n_heads=nh, scale=scale, shift_far=shift_far),
        grid=(n_t,),
        in_specs=[pl.BlockSpec(memory_space=pltpu.SMEM),
                  pl.BlockSpec((1, rows, width), lambda t: (t, 0, 0))],
        out_specs=pl.BlockSpec((1, nh, rows, width), lambda t: (t, 0, 0, 0)),
        out_shape=jax.ShapeDtypeStruct((n_t, nh, rows, width), F32),
        compiler_params=_params(("arbitrary",)),
        name="bias_expand",
    )(table, buckets)


def _ffn_kernel(x_ref, g_ref, w1_ref, w3_ref, w2_ref, o_ref, n_scr):
    j = pl.program_id(1)

    @pl.when(j == 0)
    def _():
        x = x_ref[...]
        ms = jnp.mean(x * x, axis=-1, keepdims=True)
        n_scr[...] = (x * lax.rsqrt(ms + EPS) * g_ref[...]).astype(BF16)
        o_ref[...] = jnp.zeros_like(o_ref)

    n = n_scr[...]
    h1 = _dot(n, w1_ref[...])
    h3 = _dot(n, w3_ref[...])
    a = (h1 * jax.nn.sigmoid(h1) * h3).astype(BF16)
    o_ref[...] += _dot(a, w2_ref[...])

    @pl.when(j == pl.num_programs(1) - 1)
    def _():
        o_ref[...] = x_ref[...] + 0.5 * o_ref[...]


def _ffn(x2, gain, w1, w3, w2, *, tm, tf):
    n_tok, d = x2.shape
    f = w1.shape[1]
    return pl.pallas_call(
        _ffn_kernel,
        grid=(n_tok // tm, f // tf),
        in_specs=[pl.BlockSpec((tm, d), lambda i, j: (i, 0)),
                  pl.BlockSpec((1, d), lambda i, j: (0, 0)),
                  pl.BlockSpec((d, tf), lambda i, j: (0, j)),
                  pl.BlockSpec((d, tf), lambda i, j: (0, j)),
                  pl.BlockSpec((tf, d), lambda i, j: (j, 0))],
        out_specs=pl.BlockSpec((tm, d), lambda i, j: (i, 0)),
        out_shape=jax.ShapeDtypeStruct((n_tok, d), F32),
        scratch_shapes=[pltpu.VMEM((tm, d), BF16)],
        compiler_params=_params(("arbitrary", "arbitrary")),
        name="ffn",
    )(x2, gain.reshape(1, d), w1, w3, w2)


def _proj_kernel(x_ref, g_ref, w_ref, cg_ref, cf_ref, o1_ref, o2_ref, n_scr, acc_scr, *, n1):
    j = pl.program_id(1)
    last = pl.num_programs(1) - 1

    def matmul():
        acc_scr[...] = _dot(n_scr[...], w_ref[...])

    def epilogue(o_ref):
        for h in range(acc_scr.shape[1] // LANES):
            cols = slice(h * LANES, (h + 1) * LANES)
            a = acc_scr[:, cols]
            ms = jnp.mean(a * a, axis=-1, keepdims=True)
            inv = jnp.where(cf_ref[:, cols] > 0.0, lax.rsqrt(ms + EPS), 1.0)
            o_ref[:, cols] = (a * inv * cg_ref[:, cols]).astype(o_ref.dtype)

    @pl.when(j == 0)
    def _():
        x = x_ref[...]
        ms = jnp.mean(x * x, axis=-1, keepdims=True)
        n_scr[...] = (x * lax.rsqrt(ms + EPS) * g_ref[...]).astype(BF16)
        matmul()

    @pl.when(jnp.logical_and(j > 0, j <= n1))
    def _():
        epilogue(o1_ref)
        matmul()

    @pl.when(jnp.logical_and(j > n1, j < last))
    def _():
        epilogue(o2_ref)
        matmul()

    @pl.when(j == last)
    def _():
        epilogue(o2_ref)


def _proj(x2, gain, w, col_gain, col_flag, cols1, *, tm, tn):
    n_tok, d = x2.shape
    n_cols = w.shape[1]
    n1 = cols1 // tn
    n2 = (n_cols - cols1) // tn
    assert n1 * tn == cols1 and (n1 + n2) * tn == n_cols
    n_j = n1 + n2
    prev = lambda j: jnp.maximum(j - 1, 0)
    return pl.pallas_call(
        functools.partial(_proj_kernel, n1=n1),
        grid=(n_tok // tm, n_j + 1),
        in_specs=[pl.BlockSpec((tm, d), lambda i, j: (i, 0)),
                  pl.BlockSpec((1, d), lambda i, j: (0, 0)),
                  pl.BlockSpec((d, tn), lambda i, j: (0, jnp.minimum(j, n_j - 1))),
                  pl.BlockSpec((1, tn), lambda i, j: (0, prev(j))),
                  pl.BlockSpec((1, tn), lambda i, j: (0, prev(j)))],
        out_specs=[pl.BlockSpec((tm, tn), lambda i, j: (i, jnp.minimum(prev(j), n1 - 1))),
                   pl.BlockSpec((tm, tn), lambda i, j: (i, jnp.maximum(prev(j) - n1, 0)))],
        out_shape=[jax.ShapeDtypeStruct((n_tok, cols1), F32),
                   jax.ShapeDtypeStruct((n_tok, n_cols - cols1), BF16)],
        scratch_shapes=[pltpu.VMEM((tm, d), BF16), pltpu.VMEM((tm, tn), F32)],
        compiler_params=_params(("arbitrary", "arbitrary")),
        name="in_proj",
    )(x2, gain.reshape(1, d), w, col_gain.reshape(1, n_cols), col_flag.reshape(1, n_cols))


def _dilated_kernel(q_ref, kc_ref, kp_ref, vc_ref, vp_ref, bias_ref, o_ref,
                    k_scr, v_scr, out_scr, lse_scr):
    n = pl.program_id(2)
    k_scr[0:DIL_SPAN, :] = kp_ref[0]
    k_scr[DIL_SPAN:2 * DIL_SPAN, :] = kc_ref[0]
    v_scr[0:DIL_SPAN, :] = vp_ref[0]
    v_scr[DIL_SPAN:2 * DIL_SPAN, :] = vc_ref[0]

    for p, (_, dil) in enumerate(DILATED_PATTERNS):
        span = BAND * dil

        def unit(u, carry, p=p, dil=dil, span=span):
            s = u // dil
            r = u - s * dil
            base = s * span + r
            first = jnp.logical_and(n == 0, s == 0).astype(jnp.int32)
            q = q_ref[0, pl.ds(base, BAND, stride=dil), :].astype(BF16)
            kk = k_scr[pl.ds(DIL_SPAN + base - span, 2 * BAND, stride=dil), :].astype(BF16)
            vv = v_scr[pl.ds(DIL_SPAN + base - span, 2 * BAND, stride=dil), :].astype(BF16)
            lg = _nt_dot(q, kk) + bias_ref[p, first, 0]
            m = jnp.max(lg, axis=-1, keepdims=True)
            e = jnp.exp2(lg - m).astype(BF16)
            acc = _dot(e, jnp.concatenate([vv, jnp.ones((2 * BAND, LANES), BF16)], axis=1))
            ssum = acc[:, LANES:]
            out_scr[p, pl.ds(base, BAND, stride=dil), :] = acc[:, :LANES] / ssum
            lse_scr[p, pl.ds(base, BAND, stride=dil), :] = m + jnp.log2(ssum)
            return carry

        lax.fori_loop(0, DIL_SPAN // BAND, unit, 0, unroll=16)

    chunk = 256

    def combine(c, carry):
        rows = pl.ds(pl.multiple_of(c * chunk, chunk), chunk)
        l0, l1, l2 = lse_scr[0, rows, :], lse_scr[1, rows, :], lse_scr[2, rows, :]
        big = jnp.maximum(jnp.maximum(l0, l1), l2)
        e0, e1, e2 = jnp.exp2(l0 - big), jnp.exp2(l1 - big), jnp.exp2(l2 - big)
        num = e0 * out_scr[0, rows, :] + e1 * out_scr[1, rows, :] + e2 * out_scr[2, rows, :]
        o_ref[0, rows, :] = num / (e0 + e1 + e2)
        return carry

    lax.fori_loop(0, DIL_SPAN // chunk, combine, 0)


def _dilated(p1, bias, batch, seq):
    blk = (1, DIL_SPAN, LANES)
    prev = lambda nn: jnp.maximum(nn - 1, 0)
    return pl.pallas_call(
        _dilated_kernel,
        grid=(batch, A_HEADS, seq // DIL_SPAN),
        in_specs=[pl.BlockSpec(blk, lambda b, h, nn: (b, nn, P1_QA + h)),
                  pl.BlockSpec(blk, lambda b, h, nn: (b, nn, P1_KA + h)),
                  pl.BlockSpec(blk, lambda b, h, nn: (b, prev(nn), P1_KA + h)),
                  pl.BlockSpec(blk, lambda b, h, nn: (b, nn, P1_VA + h)),
                  pl.BlockSpec(blk, lambda b, h, nn: (b, prev(nn), P1_VA + h)),
                  pl.BlockSpec((3, 2, 1, BAND, 2 * BAND), lambda b, h, nn: (0, 0, h, 0, 0))],
        out_specs=pl.BlockSpec(blk, lambda b, h, nn: (b, nn, h)),
        out_shape=jax.ShapeDtypeStruct((batch, seq, A_WIDTH), F32),
        scratch_shapes=[pltpu.VMEM((2 * DIL_SPAN, LANES), F32),
                        pltpu.VMEM((2 * DIL_SPAN, LANES), F32),
                        pltpu.VMEM((3, DIL_SPAN, LANES), F32),
                        pltpu.VMEM((3, DIL_SPAN, LANES), F32)],
        compiler_params=_params(("arbitrary", "arbitrary", "arbitrary")),
        name="dilated_attn",
    )(p1, p1, p1, p1, p1, bias)


def _compress_kernel(t_ref, pos_ref, w1_ref, w2_ref, gain_ref, flag_ref, o_ref, lo_scr, hi_scr):
    n_blk = o_ref.shape[3]
    lo_scr[...] = jnp.zeros_like(lo_scr)
    hi_scr[...] = jnp.zeros_like(hi_scr)
    for i in range(CMP_STRIDE):
        rows = t_ref[0, pl.ds(i, n_blk, stride=CMP_STRIDE), :]
        a = (rows + pos_ref[i:i + 1, :]).astype(BF16)
        lo_scr[...] += _dot(a, w1_ref[0, i * HEAD_DIM:(i + 1) * HEAD_DIM, :])
        b = (rows + pos_ref[CMP_STRIDE + i:CMP_STRIDE + i + 1, :]).astype(BF16)
        hi_scr[...] += _dot(b, w1_ref[0, (CMP_STRIDE + i) * HEAD_DIM:(CMP_STRIDE + i + 1) * HEAD_DIM, :])
    hid = lo_scr[...] + pltpu.roll(hi_scr[...], n_blk - 1, 0)
    o = _dot(jax.nn.gelu(hid).astype(BF16), w2_ref[0])
    ms = jnp.mean(o * o, axis=-1, keepdims=True)
    inv = jnp.where(flag_ref[0] > 0.0, lax.rsqrt(ms + EPS), 1.0)
    o_ref[0, 0, 0] = o * inv * gain_ref[0]


def _compress(p1, pos, w1, w2, gain, flag, batch, seq):
    n_blk = seq // CMP_STRIDE
    return pl.pallas_call(
        _compress_kernel,
        grid=(batch, 2, NSA_KV_GROUPS),
        in_specs=[pl.BlockSpec((1, seq, LANES), lambda b, kv, g: (b, 0, P1_KC + 2 * kv + g)),
                  pl.BlockSpec((CMP_LEN, HEAD_DIM), lambda b, kv, g: (0, 0)),
                  pl.BlockSpec((1, CMP_LEN * HEAD_DIM, CMP_HIDDEN), lambda b, kv, g: (kv, 0, 0)),
                  pl.BlockSpec((1, CMP_HIDDEN, HEAD_DIM), lambda b, kv, g: (kv, 0, 0)),
                  pl.BlockSpec((1, 1, HEAD_DIM), lambda b, kv, g: (kv, 0, 0)),
                  pl.BlockSpec((1, 1, HEAD_DIM), lambda b, kv, g: (kv, 0, 0))],
        out_specs=pl.BlockSpec((1, 1, 1, n_blk, HEAD_DIM), lambda b, kv, g: (b, kv, g, 0, 0)),
        out_shape=jax.ShapeDtypeStruct((batch, 2, NSA_KV_GROUPS, n_blk, HEAD_DIM), F32),
        scratch_shapes=[pltpu.VMEM((n_blk, CMP_HIDDEN), F32), pltpu.VMEM((n_blk, CMP_HIDDEN), F32)],
        compiler_params=_params(("arbitrary", "arbitrary", "arbitrary")),
        name="nsa_compress",
    )(p1, pos, w1, w2, gain, flag)


def _nsa_kernel(q_ref, ks_ref, vs_ref, kw_ref, vw_ref, kc_ref, vc_ref, gl_ref, gb_ref,
                bias_ref, ovl_ref, et_ref, o_ref, m_scr, acc_scr, s_scr, p_scr, ocmp_scr, imp_scr,
                lc_scr, pcb_scr, split_scr):
    qi = pl.program_id(2)
    tq, rep = NSA_TQ, NSA_REP
    n_cmp = kc_ref.shape[3]

    q2 = q_ref[0]
    q_heads = [q2[:, r * HEAD_DIM:(r + 1) * HEAD_DIM] for r in range(rep)]
    t_col = qi * tq + lax.broadcasted_iota(jnp.int32, (tq, 1), 0)

    def compressed(width):
        cmp_end = lax.broadcasted_iota(jnp.int32, (1, width), 1) * CMP_STRIDE + (CMP_LEN - 1)
        kcb = kc_ref[0, 0, 0, 0:width, :].astype(BF16)
        vcb = vc_ref[0, 0, 0, 0:width, :].astype(BF16)
        for r in range(rep):
            lc_scr[r, :, 0:width] = _nt_dot(q_heads[r], kcb)
        for c in range(tq // CMP_ROWS):
            rows = slice(c * CMP_ROWS, (c + 1) * CMP_ROWS)
            cmask = cmp_end <= t_col[rows]
            pc_sum = jnp.zeros((CMP_ROWS, width), F32)
            for r in range(rep):
                lc = jnp.where(cmask, lc_scr[r, rows, 0:width], NEG_INF)
                mc = jnp.max(lc, axis=-1, keepdims=True)
                pc = jnp.where(cmask, jnp.exp2(lc - mc), 0.0)
                pc = pc / jnp.maximum(jnp.sum(pc, axis=-1, keepdims=True), 1e-30)
                pcb_scr[r, rows, 0:width] = pc.astype(BF16)
                pc_sum = pc_sum + pc
            hi = pc_sum.astype(BF16)
            rem = pc_sum - hi.astype(F32)
            mid = rem.astype(BF16)
            split_scr[0, rows, 0:width] = hi
            split_scr[1, rows, 0:width] = mid
            split_scr[2, rows, 0:width] = (rem - mid.astype(F32)).astype(BF16)
        for r in range(rep):
            ocmp_scr[r] = _dot(pcb_scr[r, :, 0:width], vcb)
        ovl_t = ovl_ref[:, 0:width]
        imp_scr[...] = (_nt_dot(ovl_t, split_scr[0, :, 0:width]) + _nt_dot(ovl_t, split_scr[1, :, 0:width])
                        + _nt_dot(ovl_t, split_scr[2, :, 0:width]))

    n_grp = n_cmp // LANES
    need = jnp.minimum(((qi + 1) * tq - CMP_LEN) // (CMP_STRIDE * LANES) + 1, n_grp)
    for k in range(1, n_grp + 1):
        pl.when(need == k)(functools.partial(compressed, k * LANES))
    o_cmp = [ocmp_scr[r] for r in range(rep)]
    imp_t = imp_scr[...]

    n_rb = tq // NSA_TK
    n_wt = WIN_LEN // NSA_TK + 1
    w_keys = n_wt * NSA_TK
    win_parts = [[] for _ in range(rep)]
    for a in range(n_rb):
        rows = slice(a * NSA_TK, (a + 1) * NSA_TK)
        blk_q = qi * n_rb + a
        j0 = jnp.maximum(blk_q - (n_wt - 1), 0)
        w_rows = pl.ds(pl.multiple_of(j0 * NSA_TK, NSA_TK), w_keys)
        kw_tile = kw_ref[0, w_rows, :]
        vw_aug = jnp.concatenate([vw_ref[0, w_rows, :], jnp.ones((w_keys, HEAD_DIM), BF16)], axis=1)
        dist = t_col[rows] - (j0 * NSA_TK + lax.broadcasted_iota(jnp.int32, (1, w_keys), 1))
        wmask = jnp.where(dist >= 0, dist, WIN_LEN) < WIN_LEN
        tiles = [jnp.clip(blk_q - j0 - i, 0, BIAS_FAR) for i in range(n_wt)]
        for r in range(rep):
            s = _nt_dot(q_heads[r][rows], kw_tile) + jnp.concatenate([bias_ref[t, r] for t in tiles], axis=-1)
            s = jnp.where(wmask, s, NEG_INF)
            p = jnp.exp2(s - jnp.max(s, axis=-1, keepdims=True)).astype(BF16)
            acc = _dot(p, vw_aug)
            win_parts[r].append(acc[:, :HEAD_DIM] / acc[:, HEAD_DIM:])
    o_win = [jnp.concatenate(parts, axis=0) for parts in win_parts]

    blk = lax.broadcasted_iota(jnp.int32, (LANES, 1), 0)
    blk_f = blk.astype(F32)
    t_row = qi * tq + lax.broadcasted_iota(jnp.int32, (1, tq), 1)
    cur = t_row >> (SLC_LEN.bit_length() - 1)
    valid = blk * SLC_LEN <= t_row
    forced = (blk == 0) | (blk == cur) | (blk == cur - 1)
    score = jnp.where(valid, jnp.where(forced, FORCE_SCORE, imp_t), -1.0)
    sel_t = jnp.zeros((LANES, tq), F32)
    for _ in range(SLC_TOPK):
        best = jnp.max(score, axis=0, keepdims=True)
        idx = jnp.min(jnp.where(score == best, blk_f, float(LANES)), axis=0, keepdims=True)
        hit = blk_f == idx
        sel_t = jnp.where(hit, 1.0, sel_t)
        score = jnp.where(hit, -3e38, score)
    unsel = (sel_t.T - 1.0).astype(BF16)
    lhs = [jnp.concatenate([q_heads[r], unsel], axis=1) for r in range(rep)]

    ones_v = jnp.ones((SEL_TK, HEAD_DIM), BF16)

    sub = SEL_TK // NSA_TK
    n_tiles = ((qi + 1) * n_rb - 1) // sub + 1

    def key_rows(jj):
        return pl.ds(pl.multiple_of(jj * SEL_TK, SEL_TK), SEL_TK)

    def logits_stage(jj, heads):
        rows = key_rows(jj)
        rhs = jnp.concatenate([ks_ref[0, rows, :], et_ref[rows, :]], axis=1)
        for r in heads:
            s_scr[r] = _nt_dot(lhs[r], rhs)

    def softmax_stage(jj, heads, far):
        for a in range(n_rb):
            rows = slice(a * NSA_TK, (a + 1) * NSA_TK)
            u = qi * n_rb + a - sub * jj
            tiles = [jnp.where(u < i, BIAS_FUTURE, jnp.minimum(u - i, BIAS_FAR)) for i in range(sub)]
            for r in heads:
                s = s_scr[r, rows, :]
                if not far:
                    s = s + jnp.concatenate([bias_ref[t, r] for t in tiles], axis=-1)
                m_old = m_scr[1 - (jj & 1), r, rows, :]
                m_new = jnp.maximum(m_old, jnp.max(s, axis=-1, keepdims=True))
                p_scr[r, rows, :] = jnp.exp2(s - jnp.concatenate([m_new] * sub, axis=1)).astype(BF16)
                m_scr[jj & 1, r, rows, :] = m_new

    def pv_stage(it, heads):
        v_aug = jnp.concatenate([vs_ref[0, key_rows(jnp.maximum(it - 1, 0)), :], ones_v], axis=1)
        for r in heads:
            alpha = jnp.exp2(m_scr[it & 1, r] - m_scr[1 - (it & 1), r])
            acc_scr[r] = jnp.concatenate([alpha, alpha], axis=1) * acc_scr[r] + _dot(p_scr[r], v_aug)

    all_heads = range(rep)
    m_scr[...] = jnp.full(m_scr.shape, NEG_INF, F32)
    acc_scr[...] = jnp.zeros_like(acc_scr)
    p_scr[...] = jnp.zeros_like(p_scr)
    logits_stage(0, all_heads)

    def sel_step(far, it, carry):
        pv_stage(it, all_heads)
        softmax_stage(it, all_heads, far)
        logits_stage(jnp.minimum(it + 1, n_tiles - 1), all_heads)
        return carry

    n_far = jnp.clip((qi * n_rb - BIAS_FAR + 1) // sub, 0, n_tiles)
    lax.fori_loop(0, n_far, functools.partial(sel_step, True), 0)
    lax.fori_loop(n_far, n_tiles, functools.partial(sel_step, False), 0)
    pv_stage(n_tiles, all_heads)

    gates = jax.nn.sigmoid(gl_ref[0] + gb_ref[0])
    o_slc = []
    for r in range(rep):
        acc = acc_scr[r]
        o_slc.append(acc[:, :HEAD_DIM] / acc[:, HEAD_DIM:])
    for r in range(rep):
        c = r * N_BRANCH
        out = (gates[:, c:c + 1] * o_cmp[r] + gates[:, c + 1:c + 2] * o_slc[r]
               + gates[:, c + 2:c + 3] * o_win[r])
        o_ref[0, :, r * HEAD_DIM:(r + 1) * HEAD_DIM] = out


def _nsa(p1, p2, kcv, gate_bias, bias, overlap_t, membership, batch, seq):
    tq = NSA_TQ
    n_cmp = seq // CMP_STRIDE
    kv_blk = (1, seq, LANES)
    grp_w = NSA_REP * HEAD_DIM
    once = pl.Buffered(1)
    return pl.pallas_call(
        _nsa_kernel,
        grid=(batch, NSA_KV_GROUPS, seq // tq),
        in_specs=[pl.BlockSpec((1, tq, grp_w), lambda b, g, i: (b, i, g)),
                  pl.BlockSpec(kv_blk, lambda b, g, i: (b, 0, P2_KS + g), pipeline_mode=once),
                  pl.BlockSpec(kv_blk, lambda b, g, i: (b, 0, P2_VS + g), pipeline_mode=once),
                  pl.BlockSpec(kv_blk, lambda b, g, i: (b, 0, P2_KW + g), pipeline_mode=once),
                  pl.BlockSpec(kv_blk, lambda b, g, i: (b, 0, P2_VW + g), pipeline_mode=once),
                  pl.BlockSpec((1, 1, 1, n_cmp, HEAD_DIM), lambda b, g, i: (b, 0, g, 0, 0)),
                  pl.BlockSpec((1, 1, 1, n_cmp, HEAD_DIM), lambda b, g, i: (b, 1, g, 0, 0)),
                  pl.BlockSpec((1, tq, LANES), lambda b, g, i: (b, i, P1_GL + g)),
                  pl.BlockSpec((1, 1, LANES), lambda b, g, i: (g, 0, 0)),
                  pl.BlockSpec((N_BIAS_TILES, NSA_REP, NSA_TK, NSA_TK), lambda b, g, i: (0, g, 0, 0),
                               pipeline_mode=once),
                  pl.BlockSpec((LANES, n_cmp), lambda b, g, i: (0, 0)),
                  pl.BlockSpec((seq, LANES), lambda b, g, i: (0, 0))],
        out_specs=pl.BlockSpec((1, tq, grp_w), lambda b, g, i: (b, i, g)),
        out_shape=jax.ShapeDtypeStruct((batch, seq, NSA_WIDTH), F32),
        scratch_shapes=[pltpu.VMEM((2, NSA_REP, tq, LANES), F32),
                        pltpu.VMEM((NSA_REP, tq, 2 * HEAD_DIM), F32),
                        pltpu.VMEM((NSA_REP, tq, SEL_TK), F32),
                        pltpu.VMEM((NSA_REP, tq, SEL_TK), BF16),
                        pltpu.VMEM((NSA_REP, tq, HEAD_DIM), F32),
                        pltpu.VMEM((LANES, tq), F32),
                        pltpu.VMEM((NSA_REP, tq, n_cmp), F32),
                        pltpu.VMEM((NSA_REP, tq, n_cmp), BF16),
                        pltpu.VMEM((3, tq, n_cmp), BF16)],
        compiler_params=_params(("arbitrary", "arbitrary", "arbitrary")),
        name="nsa_attn",
    )(p2, p2, p2, p2, p2, kcv, kcv, p1, gate_bias, bias, overlap_t, membership)


def _out_kernel(x_ref, oa_ref, on_ref, ga_ref, gn_ref, w_ref, o_ref):
    def norm(t, g):
        ms = jnp.mean(t * t, axis=-1, keepdims=True)
        return (t * lax.rsqrt(ms + EPS) * g).astype(BF16)

    ya = norm(oa_ref[...], ga_ref[...])
    yn = norm(on_ref[...], gn_ref[...])
    y = _dot(ya, w_ref[0:A_WIDTH, :]) + _dot(yn, w_ref[A_WIDTH:A_WIDTH + NSA_WIDTH, :])
    o_ref[...] = x_ref[...] + y


def _out_proj(x2, oa, on, gain, w, *, tm):
    n_tok, d = x2.shape
    return pl.pallas_call(
        _out_kernel,
        grid=(n_tok // tm,),
        in_specs=[pl.BlockSpec((tm, d), lambda i: (i, 0)),
                  pl.BlockSpec((tm, A_WIDTH), lambda i: (i, 0)),
                  pl.BlockSpec((tm, NSA_WIDTH), lambda i: (i, 0)),
                  pl.BlockSpec((1, A_WIDTH), lambda i: (0, 0)),
                  pl.BlockSpec((1, NSA_WIDTH), lambda i: (0, 0)),
                  pl.BlockSpec((d, d), lambda i: (0, 0))],
        out_specs=pl.BlockSpec((tm, d), lambda i: (i, 0)),
        out_shape=jax.ShapeDtypeStruct((n_tok, d), F32),
        compiler_params=_params(("arbitrary",)),
        name="out_proj",
    )(x2, oa, on, gain[:A_WIDTH].reshape(1, A_WIDTH), gain[A_WIDTH:].reshape(1, NSA_WIDTH), w)


def _pack_w_in(w_in, gate_bias, q_norm_a, k_norm_a, q_norm_nsa, k_norm_nsa):
    scale = HEAD_DIM ** -0.5
    sizes = (A_WIDTH,) * 3 + (NSA_WIDTH,) + (KV_WIDTH,) * 6 + (N_BRANCH * NSA_HEADS,)
    offs = np.concatenate([[0], np.cumsum(sizes)])
    qa, ka, va, qn, kc, vc, k_s, v_s, k_w, v_w, gl = [w_in[:, offs[i]:offs[i + 1]] for i in range(11)]
    per_grp = NSA_REP * N_BRANCH
    pad = jnp.zeros((w_in.shape[0], LANES - per_grp), w_in.dtype)
    gl_cols = [jnp.concatenate([gl[:, g * per_grp:(g + 1) * per_grp], pad], axis=1) for g in range(NSA_KV_GROUPS)]
    n_pad = P1_HEADS - P1_USED
    w_pad = jnp.zeros((w_in.shape[0], n_pad * LANES), w_in.dtype)
    w = jnp.concatenate([qa, ka, va, kc, vc] + gl_cols + [w_pad, qn, k_s, v_s, k_w, v_w], axis=1).astype(BF16)

    ones = jnp.ones((HEAD_DIM,), F32)
    tile = lambda v, n: jnp.tile(v, n)
    gain = jnp.concatenate([tile(q_norm_a * (scale * LOG2E), A_HEADS), tile(k_norm_a, A_HEADS),
                            tile(ones, A_HEADS + 6 + n_pad),
                            tile(q_norm_nsa * (scale * LOG2E), NSA_HEADS), tile(k_norm_nsa, 2), tile(ones, 2),
                            tile(k_norm_nsa, 2), tile(ones, 2)])
    flag = jnp.concatenate([jnp.ones((2 * A_WIDTH,), F32), jnp.zeros((A_WIDTH + (6 + n_pad) * HEAD_DIM,), F32),
                            jnp.ones((NSA_WIDTH + KV_WIDTH,), F32), jnp.zeros((KV_WIDTH,), F32),
                            jnp.ones((KV_WIDTH,), F32), jnp.zeros((KV_WIDTH,), F32)])
    gb = jnp.stack([jnp.concatenate([gate_bias[g * per_grp:(g + 1) * per_grp],
                                     jnp.zeros((LANES - per_grp,), F32)]) for g in range(NSA_KV_GROUPS)])
    return w, gain, flag, gb.reshape(NSA_KV_GROUPS, 1, LANES)


def _forward(x, ffn1_norm, ffn1_w1, ffn1_w3, ffn1_w2, mix_norm, w_in, gate_bias, q_norm_a, k_norm_a,
             q_norm_nsa, k_norm_nsa, cmp_pos, cmp_k_w1, cmp_k_w2, cmp_v_w1, cmp_v_w2, out_norm, w_out,
             ffn2_norm, ffn2_w1, ffn2_w3, ffn2_w2, rel_bias, *, tm_ffn, tf_ffn, tm_proj, tm_out):
    batch, seq, d = x.shape
    depth = ffn1_w1.shape[0]
    n_tok = batch * seq
    assert seq % DIL_SPAN == 0 and seq // SLC_LEN <= LANES
    assert seq % NSA_TQ == 0 and seq % SEL_TK == 0 and (seq // CMP_STRIDE) % LANES == 0
    assert n_tok % max(tm_ffn, tm_proj, tm_out) == 0 and d == D_MODEL

    bias_a = _expand_bias(rel_bias[:, :A_HEADS], jnp.asarray(_dilated_buckets()), scale=LOG2E)
    bias_a = bias_a.reshape(3, 2, A_HEADS, BAND, 2 * BAND)
    bias_n = _expand_bias(rel_bias[:, A_HEADS:], jnp.asarray(_nsa_buckets()),
                          scale=LOG2E, shift_far=True)
    overlap_t = jnp.asarray(_overlap_t_np(seq // CMP_STRIDE)).astype(BF16)
    membership = jnp.asarray(_block_membership_np(seq)).astype(BF16)

    x2 = x.reshape(n_tok, d)
    for l in range(depth):
        x2 = _ffn(x2, ffn1_norm[l], ffn1_w1[l].astype(BF16), ffn1_w3[l].astype(BF16),
                  ffn1_w2[l].astype(BF16), tm=tm_ffn, tf=tf_ffn)

        w_p, g_p, f_p, gb = _pack_w_in(w_in[l], gate_bias[l], q_norm_a[l], k_norm_a[l],
                                       q_norm_nsa[l], k_norm_nsa[l])
        p1, p2 = _proj(x2, mix_norm[l], w_p, g_p, f_p, P1_HEADS * LANES, tm=tm_proj, tn=1024)
        p1 = p1.reshape(batch, seq, P1_HEADS * LANES)
        p2 = p2.reshape(batch, seq, P2_HEADS * LANES)

        o_a = _dilated(p1, bias_a, batch, seq)

        cmp_w1 = jnp.stack([cmp_k_w1[l], cmp_v_w1[l]]).astype(BF16)
        cmp_w2 = jnp.stack([cmp_k_w2[l], cmp_v_w2[l]]).astype(BF16)
        cmp_gain = jnp.stack([k_norm_nsa[l], jnp.ones((HEAD_DIM,), F32)]).reshape(2, 1, HEAD_DIM)
        cmp_flag = jnp.stack([jnp.ones((HEAD_DIM,), F32), jnp.zeros((HEAD_DIM,), F32)]).reshape(2, 1, HEAD_DIM)
        kcv = _compress(p1, cmp_pos[l], cmp_w1, cmp_w2, cmp_gain, cmp_flag, batch, seq)

        o_n = _nsa(p1, p2, kcv, gb, bias_n, overlap_t, membership, batch, seq)

        x2 = _out_proj(x2, o_a.reshape(n_tok, A_WIDTH), o_n.reshape(n_tok, NSA_WIDTH), out_norm[l],
                       w_out[l].astype(BF16), tm=tm_out)

        x2 = _ffn(x2, ffn2_norm[l], ffn2_w1[l].astype(BF16), ffn2_w3[l].astype(BF16),
                  ffn2_w2[l].astype(BF16), tm=tm_ffn, tf=tf_ffn)
    return x2.reshape(batch, seq, d)


def kernel(x, ffn1_norm, ffn1_w1, ffn1_w3, ffn1_w2, mix_norm, w_in, gate_bias, q_norm_a, k_norm_a, q_norm_nsa, k_norm_nsa, cmp_pos, cmp_k_w1, cmp_k_w2, cmp_v_w1, cmp_v_w2, out_norm, w_out, ffn2_norm, ffn2_w1, ffn2_w3, ffn2_w2, rel_bias):
    return _forward(x, ffn1_norm, ffn1_w1, ffn1_w3, ffn1_w2, mix_norm, w_in, gate_bias, q_norm_a, k_norm_a,
                    q_norm_nsa, k_norm_nsa, cmp_pos, cmp_k_w1, cmp_k_w2, cmp_v_w1, cmp_v_w2, out_norm, w_out,
                    ffn2_norm, ffn2_w1, ffn2_w3, ffn2_w2, rel_bias,
                    tm_ffn=1024, tf_ffn=512, tm_proj=1024, tm_out=512)
```

```python
import functools
import math

import numpy as np
import jax
import jax.numpy as jnp
from jax import lax
from jax.experimental import pallas as pl
from jax.experimental.pallas import tpu as pltpu

D_MODEL = 2048
HEAD_DIM = 128
N_HEADS = 16
A_HEADS = 8
NSA_HEADS = 8
NSA_KV_GROUPS = 2
NSA_REP = 4
A_WIDTH = A_HEADS * HEAD_DIM
NSA_WIDTH = NSA_HEADS * HEAD_DIM
KV_WIDTH = NSA_KV_GROUPS * HEAD_DIM
N_BRANCH = 3
D_FF = 5632
DILATED_PATTERNS = ((128, 1), (512, 4), (2048, 16))
BAND = 128
CMP_STRIDE = 16
CMP_LEN = 32
CMP_HIDDEN = 512
SLC_LEN = 64
SLC_TOPK = 16
WIN_LEN = 512
N_BUCKETS = 32
MAX_DISTANCE = 2048
FORCE_SCORE = 1e6
NEG_INF = -1e30
EPS = 1e-6
LOG2E = 1.4426950408889634

LANES = 128
DIL_SPAN = 2048
NSA_TQ = 512
NSA_TK = 128
SEL_TK = 512
CMP_ROWS = 32
BIAS_FAR = 13
BIAS_FUTURE = 14
N_BIAS_TILES = 15
MASK_BIG = 1e30
VMEM_LIMIT = 56 * 1024 * 1024

F32 = jnp.float32
BF16 = jnp.bfloat16

P1_QA, P1_KA, P1_VA, P1_KC, P1_VC, P1_GL = 0, 8, 16, 24, 26, 28
P1_USED = 30
P1_HEADS = 32
P2_QN, P2_KS, P2_VS, P2_KW, P2_VW = 0, 8, 10, 12, 14
P2_HEADS = 16


def _params(sem, flags=None):
    return pltpu.CompilerParams(dimension_semantics=sem, vmem_limit_bytes=VMEM_LIMIT, flags=flags)


def _nt_dot(a, b):
    return lax.dot_general(a, b, (((1,), (1,)), ((), ())), preferred_element_type=F32)


def _dot(a, b):
    return jnp.dot(a, b, preferred_element_type=F32)


def _rel_bucket_np(dist):
    n = np.maximum(dist, 0)
    max_exact = N_BUCKETS // 2
    nf = np.maximum(n, 1).astype(np.float32)
    log_b = max_exact + (np.log(nf / np.float32(max_exact)) / np.float32(math.log(MAX_DISTANCE / max_exact))
                         * np.float32(N_BUCKETS - max_exact)).astype(np.int32)
    return np.where(n < max_exact, n, np.minimum(log_b, N_BUCKETS - 1)).astype(np.int32)


def _dilated_buckets():
    i = np.arange(BAND)[:, None]
    j = np.arange(2 * BAND)[None, :] - BAND
    diff = i - j
    out = []
    for window, dilation in DILATED_PATTERNS:
        mask = (diff >= 0) & (diff <= window // dilation)
        bkt = _rel_bucket_np(diff * dilation)
        out.append(np.where(mask, bkt, -1))
        out.append(np.where(mask & (j >= 0), bkt, -1))
    return np.stack(out).astype(np.int32)


def _nsa_buckets():
    i = np.arange(NSA_TK)[:, None]
    j = np.arange(NSA_TK)[None, :]
    tiles = []
    for m in range(BIAS_FAR + 1):
        dist = NSA_TK * m + i - j
        tiles.append(np.where(dist >= 0, _rel_bucket_np(dist), -1))
    assert (tiles[BIAS_FAR] == N_BUCKETS - 1).all()
    tiles.append(np.full((NSA_TK, NSA_TK), -1))
    assert len(tiles) == N_BIAS_TILES and BIAS_FUTURE == N_BIAS_TILES - 1
    return np.stack(tiles).astype(np.int32)


def _overlap_t_np(n_cmp_pad):
    c_start = np.arange(n_cmp_pad)[None, :] * CMP_STRIDE
    s_start = np.arange(LANES)[:, None] * SLC_LEN
    return ((c_start < s_start + SLC_LEN) & (c_start + CMP_LEN > s_start)).astype(np.float32)


def _block_membership_np(seq):
    k_blk = np.arange(seq)[:, None] // SLC_LEN
    return np.where(k_blk == np.arange(LANES)[None, :], MASK_BIG, 0.0).astype(np.float32)


def _bias_kernel(tbl_ref, bkt_ref, o_ref, *, n_heads, scale, shift_far):
    bkt = bkt_ref[0]
    for h in range(n_heads):
        acc = jnp.full(bkt.shape, NEG_INF, F32)
        base = tbl_ref[N_BUCKETS - 1, h] if shift_far else 0.0
        for b in range(N_BUCKETS):
            acc = jnp.where(bkt == b, (tbl_ref[b, h] - base) * scale, acc)
        o_ref[0, h] = acc


def _expand_bias(table, buckets, scale=1.0, shift_far=False):
    n_t, rows, width = buckets.shape
    nh = table.shape[1]
    return pl.pallas_call(
        functools.partial(_bias_kernel, n_heads=nh, scale=scale, shift_far=shift_far),
        grid=(n_t,),
        in_specs=[pl.BlockSpec(memory_space=pltpu.SMEM),
                  pl.BlockSpec((1, rows, width), lambda t: (t, 0, 0))],
        out_specs=pl.BlockSpec((1, nh, rows, width), lambda t: (t, 0, 0, 0)),
        out_shape=jax.ShapeDtypeStruct((n_t, nh, rows, width), F32),
        compiler_params=_params(("arbitrary",)),
        name="bias_expand",
    )(table, buckets)


def _ffn_kernel(x_ref, g_ref, w1_ref, w3_ref, w2_ref, o_ref, n_scr):
    j = pl.program_id(1)

    @pl.when(j == 0)
    def _():
        x = x_ref[...]
        ms = jnp.mean(x * x, axis=-1, keepdims=True)
        n_scr[...] = (x * lax.rsqrt(ms + EPS) * g_ref[...]).astype(BF16)
        o_ref[...] = jnp.zeros_like(o_ref)

    n = n_scr[...]
    h1 = _dot(n, w1_ref[...])
    h3 = _dot(n, w3_ref[...])
    a = (h1 * jax.nn.sigmoid(h1) * h3).astype(BF16)
    o_ref[...] += _dot(a, w2_ref[...])

    @pl.when(j == pl.num_programs(1) - 1)
    def _():
        o_ref[...] = x_ref[...] + 0.5 * o_ref[...]


def _ffn(x2, gain, w1, w3, w2, *, tm, tf):
    n_tok, d = x2.shape
    f = w1.shape[1]
    return pl.pallas_call(
        _ffn_kernel,
        grid=(n_tok // tm, f // tf),
        in_specs=[pl.BlockSpec((tm, d), lambda i, j: (i, 0)),
                  pl.BlockSpec((1, d), lambda i, j: (0, 0)),
                  pl.BlockSpec((d, tf), lambda i, j: (0, j)),
                  pl.BlockSpec((d, tf), lambda i, j: (0, j)),
                  pl.BlockSpec((tf, d), lambda i, j: (j, 0))],
        out_specs=pl.BlockSpec((tm, d), lambda i, j: (i, 0)),
        out_shape=jax.ShapeDtypeStruct((n_tok, d), F32),
        scratch_shapes=[pltpu.VMEM((tm, d), BF16)],
        compiler_params=_params(("arbitrary", "arbitrary")),
        name="ffn",
    )(x2, gain.reshape(1, d), w1, w3, w2)


def _proj_kernel(x_ref, g_ref, w_ref, cg_ref, cf_ref, o1_ref, o2_ref, n_scr, acc_scr, *, n1):
    j = pl.program_id(1)
    last = pl.num_programs(1) - 1

    def matmul():
        acc_scr[...] = _dot(n_scr[...], w_ref[...])

    def epilogue(o_ref):
        for h in range(acc_scr.shape[1] // LANES):
            cols = slice(h * LANES, (h + 1) * LANES)
            a = acc_scr[:, cols]
            ms = jnp.mean(a * a, axis=-1, keepdims=True)
            inv = jnp.where(cf_ref[:, cols] > 0.0, lax.rsqrt(ms + EPS), 1.0)
            o_ref[:, cols] = (a * inv * cg_ref[:, cols]).astype(o_ref.dtype)

    @pl.when(j == 0)
    def _():
        x = x_ref[...]
        ms = jnp.mean(x * x, axis=-1, keepdims=True)
        n_scr[...] = (x * lax.rsqrt(ms + EPS) * g_ref[...]).astype(BF16)
        matmul()

    @pl.when(jnp.logical_and(j > 0, j <= n1))
    def _():
        epilogue(o1_ref)
        matmul()

    @pl.when(jnp.logical_and(j > n1, j < last))
    def _():
        epilogue(o2_ref)
        matmul()

    @pl.when(j == last)
    def _():
        epilogue(o2_ref)


def _proj(x2, gain, w, col_gain, col_flag, cols1, *, tm, tn):
    n_tok, d = x2.shape
    n_cols = w.shape[1]
    n1 = cols1 // tn
    n2 = (n_cols - cols1) // tn
    assert n1 * tn == cols1 and (n1 + n2) * tn == n_cols
    n_j = n1 + n2
    prev = lambda j: jnp.maximum(j - 1, 0)
    return pl.pallas_call(
        functools.partial(_proj_kernel, n1=n1),
        grid=(n_tok // tm, n_j + 1),
        in_specs=[pl.BlockSpec((tm, d), lambda i, j: (i, 0)),
                  pl.BlockSpec((1, d), lambda i, j: (0, 0)),
                  pl.BlockSpec((d, tn), lambda i, j: (0, jnp.minimum(j, n_j - 1))),
                  pl.BlockSpec((1, tn), lambda i, j: (0, prev(j))),
                  pl.BlockSpec((1, tn), lambda i, j: (0, prev(j)))],
        out_specs=[pl.BlockSpec((tm, tn), lambda i, j: (i, jnp.minimum(prev(j), n1 - 1))),
                   pl.BlockSpec((tm, tn), lambda i, j: (i, jnp.maximum(prev(j) - n1, 0)))],
        out_shape=[jax.ShapeDtypeStruct((n_tok, cols1), F32),
                   jax.ShapeDtypeStruct((n_tok, n_cols - cols1), BF16)],
        scratch_shapes=[pltpu.VMEM((tm, d), BF16), pltpu.VMEM((tm, tn), F32)],
        compiler_params=_params(("arbitrary", "arbitrary")),
        name="in_proj",
    )(x2, gain.reshape(1, d), w, col_gain.reshape(1, n_cols), col_flag.reshape(1, n_cols))


def _dilated_kernel(q_ref, kc_ref, kp_ref, vc_ref, vp_ref, bias_ref, o_ref,
                    out_scr, lse_scr):
    n = pl.program_id(2)
    first_span = (n == 0).astype(jnp.int32)

    for p, (_, dil) in enumerate(DILATED_PATTERNS):
        span = BAND * dil
        for u in range(DIL_SPAN // BAND):
            s, r = divmod(u, dil)
            base = s * span + r
            rows = pl.ds(base, BAND, stride=dil)
            if s == 0:
                prev = pl.ds(DIL_SPAN - span + r, BAND, stride=dil)
                k_prev, v_prev, first = kp_ref[0, prev, :], vp_ref[0, prev, :], first_span
            else:
                prev = pl.ds(base - span, BAND, stride=dil)
                k_prev, v_prev, first = kc_ref[0, prev, :], vc_ref[0, prev, :], 0
            q = q_ref[0, rows, :].astype(BF16)
            kk = jnp.concatenate([k_prev, kc_ref[0, rows, :]], axis=0).astype(BF16)
            vv = jnp.concatenate([v_prev, vc_ref[0, rows, :]], axis=0).astype(BF16)
            lg = _nt_dot(q, kk) + bias_ref[p, first, 0]
            m = jnp.max(lg, axis=-1, keepdims=True)
            e = jnp.exp2(lg - m).astype(BF16)
            acc = _dot(e, jnp.concatenate([vv, jnp.ones((2 * BAND, LANES), BF16)], axis=1))
            ssum = acc[:, LANES:]
            out_scr[p, rows, :] = acc[:, :LANES] / ssum
            lse_scr[p, rows, :] = m + jnp.log2(ssum)

    chunk = 256

    def combine(c, carry):
        rows = pl.ds(pl.multiple_of(c * chunk, chunk), chunk)
        l0, l1, l2 = lse_scr[0, rows, :], lse_scr[1, rows, :], lse_scr[2, rows, :]
        big = jnp.maximum(jnp.maximum(l0, l1), l2)
        e0, e1, e2 = jnp.exp2(l0 - big), jnp.exp2(l1 - big), jnp.exp2(l2 - big)
        num = e0 * out_scr[0, rows, :] + e1 * out_scr[1, rows, :] + e2 * out_scr[2, rows, :]
        o_ref[0, rows, :] = num / (e0 + e1 + e2)
        return carry

    lax.fori_loop(0, DIL_SPAN // chunk, combine, 0)


def _dilated(p1, bias, batch, seq):
    blk = (1, DIL_SPAN, LANES)
    prev = lambda nn: jnp.maximum(nn - 1, 0)
    return pl.pallas_call(
        _dilated_kernel,
        grid=(batch, A_HEADS, seq // DIL_SPAN),
        in_specs=[pl.BlockSpec(blk, lambda b, h, nn: (b, nn, P1_QA + h)),
                  pl.BlockSpec(blk, lambda b, h, nn: (b, nn, P1_KA + h)),
                  pl.BlockSpec(blk, lambda b, h, nn: (b, prev(nn), P1_KA + h)),
                  pl.BlockSpec(blk, lambda b, h, nn: (b, nn, P1_VA + h)),
                  pl.BlockSpec(blk, lambda b, h, nn: (b, prev(nn), P1_VA + h)),
                  pl.BlockSpec((3, 2, 1, BAND, 2 * BAND), lambda b, h, nn: (0, 0, h, 0, 0))],
        out_specs=pl.BlockSpec(blk, lambda b, h, nn: (b, nn, h)),
        out_shape=jax.ShapeDtypeStruct((batch, seq, A_WIDTH), F32),
        scratch_shapes=[pltpu.VMEM((3, DIL_SPAN, LANES), F32),
                        pltpu.VMEM((3, DIL_SPAN, LANES), F32)],
        compiler_params=_params(("arbitrary", "arbitrary", "arbitrary")),
        name="dilated_attn",
    )(p1, p1, p1, p1, p1, bias)


def _compress_kernel(t_ref, pos_ref, w1_ref, w2_ref, gain_ref, flag_ref, o_ref, lo_scr, hi_scr):
    n_blk = o_ref.shape[3]
    lo_scr[...] = jnp.zeros_like(lo_scr)
    hi_scr[...] = jnp.zeros_like(hi_scr)
    for i in range(CMP_STRIDE):
        rows = t_ref[0, pl.ds(i, n_blk, stride=CMP_STRIDE), :]
        a = (rows + pos_ref[i:i + 1, :]).astype(BF16)
        lo_scr[...] += _dot(a, w1_ref[0, i * HEAD_DIM:(i + 1) * HEAD_DIM, :])
        b = (rows + pos_ref[CMP_STRIDE + i:CMP_STRIDE + i + 1, :]).astype(BF16)
        hi_scr[...] += _dot(b, w1_ref[0, (CMP_STRIDE + i) * HEAD_DIM:(CMP_STRIDE + i + 1) * HEAD_DIM, :])
    hid = lo_scr[...] + pltpu.roll(hi_scr[...], n_blk - 1, 0)
    o = _dot(jax.nn.gelu(hid).astype(BF16), w2_ref[0])
    ms = jnp.mean(o * o, axis=-1, keepdims=True)
    inv = jnp.where(flag_ref[0] > 0.0, lax.rsqrt(ms + EPS), 1.0)
    o_ref[0, 0, 0] = o * inv * gain_ref[0]


def _compress(p1, pos, w1, w2, gain, flag, batch, seq):
    n_blk = seq // CMP_STRIDE
    return pl.pallas_call(
        _compress_kernel,
        grid=(batch, 2, NSA_KV_GROUPS),
        in_specs=[pl.BlockSpec((1, seq, LANES), lambda b, kv, g: (b, 0, P1_KC + 2 * kv + g)),
                  pl.BlockSpec((CMP_LEN, HEAD_DIM), lambda b, kv, g: (0, 0)),
                  pl.BlockSpec((1, CMP_LEN * HEAD_DIM, CMP_HIDDEN), lambda b, kv, g: (kv, 0, 0)),
                  pl.BlockSpec((1, CMP_HIDDEN, HEAD_DIM), lambda b, kv, g: (kv, 0, 0)),
                  pl.BlockSpec((1, 1, HEAD_DIM), lambda b, kv, g: (kv, 0, 0)),
                  pl.BlockSpec((1, 1, HEAD_DIM), lambda b, kv, g: (kv, 0, 0))],
        out_specs=pl.BlockSpec((1, 1, 1, n_blk, HEAD_DIM), lambda b, kv, g: (b, kv, g, 0, 0)),
        out_shape=jax.ShapeDtypeStruct((batch, 2, NSA_KV_GROUPS, n_blk, HEAD_DIM), F32),
        scratch_shapes=[pltpu.VMEM((n_blk, CMP_HIDDEN), F32), pltpu.VMEM((n_blk, CMP_HIDDEN), F32)],
        compiler_params=_params(("arbitrary", "arbitrary", "arbitrary")),
        name="nsa_compress",
    )(p1, pos, w1, w2, gain, flag)


def _nsa_kernel(q_ref, ks_ref, vs_ref, kw_ref, vw_ref, kc_ref, vc_ref, gl_ref, gb_ref,
                bias_ref, ovl_ref, et_ref, o_ref, m_scr, acc_scr, s_scr, p_scr, ocmp_scr, imp_scr,
                lc_scr, pcb_scr, split_scr):
    qi = pl.program_id(2)
    tq, rep = NSA_TQ, NSA_REP
    n_cmp = kc_ref.shape[3]

    q2 = q_ref[0]
    q_heads = [q2[:, r * HEAD_DIM:(r + 1) * HEAD_DIM] for r in range(rep)]
    t_col = qi * tq + lax.broadcasted_iota(jnp.int32, (tq, 1), 0)

    def compressed(width):
        cmp_end = lax.broadcasted_iota(jnp.int32, (1, width), 1) * CMP_STRIDE + (CMP_LEN - 1)
        kcb = kc_ref[0, 0, 0, 0:width, :].astype(BF16)
        vcb = vc_ref[0, 0, 0, 0:width, :].astype(BF16)
        for r in range(rep):
            lc_scr[r, :, 0:width] = _nt_dot(q_heads[r], kcb)
        for c in range(tq // CMP_ROWS):
            rows = slice(c * CMP_ROWS, (c + 1) * CMP_ROWS)
            cmask = cmp_end <= t_col[rows]
            pc_sum = jnp.zeros((CMP_ROWS, width), F32)
            for r in range(rep):
                lc = jnp.where(cmask, lc_scr[r, rows, 0:width], NEG_INF)
                mc = jnp.max(lc, axis=-1, keepdims=True)
                pc = jnp.where(cmask, jnp.exp2(lc - mc), 0.0)
                pc = pc / jnp.maximum(jnp.sum(pc, axis=-1, keepdims=True), 1e-30)
                pcb_scr[r, rows, 0:width] = pc.astype(BF16)
                pc_sum = pc_sum + pc
            hi = pc_sum.astype(BF16)
            rem = pc_sum - hi.astype(F32)
            mid = rem.astype(BF16)
            split_scr[0, rows, 0:width] = hi
            split_scr[1, rows, 0:width] = mid
            split_scr[2, rows, 0:width] = (rem - mid.astype(F32)).astype(BF16)
        for r in range(rep):
            ocmp_scr[r] = _dot(pcb_scr[r, :, 0:width], vcb)
        ovl_t = ovl_ref[:, 0:width]
        imp_scr[...] = (_nt_dot(ovl_t, split_scr[0, :, 0:width]) + _nt_dot(ovl_t, split_scr[1, :, 0:width])
                        + _nt_dot(ovl_t, split_scr[2, :, 0:width]))

    n_grp = n_cmp // LANES
    need = jnp.minimum(((qi + 1) * tq - CMP_LEN) // (CMP_STRIDE * LANES) + 1, n_grp)
    for k in range(1, n_grp + 1):
        pl.when(need == k)(functools.partial(compressed, k * LANES))
    o_cmp = [ocmp_scr[r] for r in range(rep)]
    imp_t = imp_scr[...]

    n_rb = tq // NSA_TK
    n_wt = WIN_LEN // NSA_TK + 1
    w_keys = n_wt * NSA_TK
    win_parts = [[] for _ in range(rep)]
    for a in range(n_rb):
        rows = slice(a * NSA_TK, (a + 1) * NSA_TK)
        blk_q = qi * n_rb + a
        j0 = jnp.maximum(blk_q - (n_wt - 1), 0)
        w_rows = pl.ds(pl.multiple_of(j0 * NSA_TK, NSA_TK), w_keys)
        kw_tile = kw_ref[0, w_rows, :]
        vw_aug = jnp.concatenate([vw_ref[0, w_rows, :], jnp.ones((w_keys, HEAD_DIM), BF16)], axis=1)
        dist = t_col[rows] - (j0 * NSA_TK + lax.broadcasted_iota(jnp.int32, (1, w_keys), 1))
        wmask = jnp.where(dist >= 0, dist, WIN_LEN) < WIN_LEN
        tiles = [jnp.clip(blk_q - j0 - i, 0, BIAS_FAR) for i in range(n_wt)]
        for r in range(rep):
            s = _nt_dot(q_heads[r][rows], kw_tile) + jnp.concatenate([bias_ref[t, r] for t in tiles], axis=-1)
            s = jnp.where(wmask, s, NEG_INF)
            p = jnp.exp2(s - jnp.max(s, axis=-1, keepdims=True)).astype(BF16)
            acc = _dot(p, vw_aug)
            win_parts[r].append(acc[:, :HEAD_DIM] / acc[:, HEAD_DIM:])
    o_win = [jnp.concatenate(parts, axis=0) for parts in win_parts]

    blk = lax.broadcasted_iota(jnp.int32, (LANES, 1), 0)
    blk_f = blk.astype(F32)
    t_row = qi * tq + lax.broadcasted_iota(jnp.int32, (1, tq), 1)
    cur = t_row >> (SLC_LEN.bit_length() - 1)
    valid = blk * SLC_LEN <= t_row
    forced = (blk == 0) | (blk == cur) | (blk == cur - 1)
    score = jnp.where(valid, jnp.where(forced, FORCE_SCORE, imp_t), -1.0)
    sel_t = jnp.zeros((LANES, tq), F32)
    for _ in range(SLC_TOPK):
        best = jnp.max(score, axis=0, keepdims=True)
        idx = jnp.min(jnp.where(score == best, blk_f, float(LANES)), axis=0, keepdims=True)
        hit = blk_f == idx
        sel_t = jnp.where(hit, 1.0, sel_t)
        score = jnp.where(hit, -3e38, score)
    unsel = (sel_t.T - 1.0).astype(BF16)
    lhs = [jnp.concatenate([q_heads[r], unsel], axis=1) for r in range(rep)]

    ones_v = jnp.ones((SEL_TK, HEAD_DIM), BF16)

    sub = SEL_TK // NSA_TK
    n_tiles = ((qi + 1) * n_rb - 1) // sub + 1

    def key_rows(jj):
        return pl.ds(pl.multiple_of(jj * SEL_TK, SEL_TK), SEL_TK)

    def logits_stage(jj, heads):
        rows = key_rows(jj)
        rhs = jnp.concatenate([ks_ref[0, rows, :], et_ref[rows, :]], axis=1)
        for r in heads:
            s_scr[r] = _nt_dot(lhs[r], rhs)

    def softmax_stage(jj, heads, far):
        for a in range(n_rb):
            rows = slice(a * NSA_TK, (a + 1) * NSA_TK)
            u = qi * n_rb + a - sub * jj
            tiles = [jnp.where(u < i, BIAS_FUTURE, jnp.minimum(u - i, BIAS_FAR)) for i in range(sub)]
            for r in heads:
                s = s_scr[r, rows, :]
                if not far:
                    s = s + jnp.concatenate([bias_ref[t, r] for t in tiles], axis=-1)
                m_old = m_scr[1 - (jj & 1), r, rows, :]
                m_new = jnp.maximum(m_old, jnp.max(s, axis=-1, keepdims=True))
                p_scr[r, rows, :] = jnp.exp2(s - jnp.concatenate([m_new] * sub, axis=1)).astype(BF16)
                m_scr[jj & 1, r, rows, :] = m_new

    def pv_stage(it, heads):
        v_aug = jnp.concatenate([vs_ref[0, key_rows(jnp.maximum(it - 1, 0)), :], ones_v], axis=1)
        for r in heads:
            alpha = jnp.exp2(m_scr[it & 1, r] - m_scr[1 - (it & 1), r])
            acc_scr[r] = jnp.concatenate([alpha, alpha], axis=1) * acc_scr[r] + _dot(p_scr[r], v_aug)

    all_heads = range(rep)
    m_scr[...] = jnp.full(m_scr.shape, NEG_INF, F32)
    acc_scr[...] = jnp.zeros_like(acc_scr)
    p_scr[...] = jnp.zeros_like(p_scr)
    logits_stage(0, all_heads)

    def sel_step(far, it, carry):
        pv_stage(it, all_heads)
        softmax_stage(it, all_heads, far)
        logits_stage(jnp.minimum(it + 1, n_tiles - 1), all_heads)
        return carry

    n_far = jnp.clip((qi * n_rb - BIAS_FAR + 1) // sub, 0, n_tiles)
    lax.fori_loop(0, n_far, functools.partial(sel_step, True), 0)
    lax.fori_loop(n_far, n_tiles, functools.partial(sel_step, False), 0)
    pv_stage(n_tiles, all_heads)

    gates = jax.nn.sigmoid(gl_ref[0] + gb_ref[0])
    o_slc = []
    for r in range(rep):
        acc = acc_scr[r]
        o_slc.append(acc[:, :HEAD_DIM] / acc[:, HEAD_DIM:])
    for r in range(rep):
        c = r * N_BRANCH
        out = (gates[:, c:c + 1] * o_cmp[r] + gates[:, c + 1:c + 2] * o_slc[r]
               + gates[:, c + 2:c + 3] * o_win[r])
        o_ref[0, :, r * HEAD_DIM:(r + 1) * HEAD_DIM] = out


def _nsa(p1, p2, kcv, gate_bias, bias, overlap_t, membership, batch, seq):
    tq = NSA_TQ
    n_cmp = seq // CMP_STRIDE
    kv_blk = (1, seq, LANES)
    grp_w = NSA_REP * HEAD_DIM
    once = pl.Buffered(1)
    return pl.pallas_call(
        _nsa_kernel,
        grid=(batch, NSA_KV_GROUPS, seq // tq),
        in_specs=[pl.BlockSpec((1, tq, grp_w), lambda b, g, i: (b, i, g)),
                  pl.BlockSpec(kv_blk, lambda b, g, i: (b, 0, P2_KS + g), pipeline_mode=once),
                  pl.BlockSpec(kv_blk, lambda b, g, i: (b, 0, P2_VS + g), pipeline_mode=once),
                  pl.BlockSpec(kv_blk, lambda b, g, i: (b, 0, P2_KW + g), pipeline_mode=once),
                  pl.BlockSpec(kv_blk, lambda b, g, i: (b, 0, P2_VW + g), pipeline_mode=once),
                  pl.BlockSpec((1, 1, 1, n_cmp, HEAD_DIM), lambda b, g, i: (b, 0, g, 0, 0)),
                  pl.BlockSpec((1, 1, 1, n_cmp, HEAD_DIM), lambda b, g, i: (b, 1, g, 0, 0)),
                  pl.BlockSpec((1, tq, LANES), lambda b, g, i: (b, i, P1_GL + g)),
                  pl.BlockSpec((1, 1, LANES), lambda b, g, i: (g, 0, 0)),
                  pl.BlockSpec((N_BIAS_TILES, NSA_REP, NSA_TK, NSA_TK), lambda b, g, i: (0, g, 0, 0),
                               pipeline_mode=once),
                  pl.BlockSpec((LANES, n_cmp), lambda b, g, i: (0, 0)),
                  pl.BlockSpec((seq, LANES), lambda b, g, i: (0, 0))],
        out_specs=pl.BlockSpec((1, tq, grp_w), lambda b, g, i: (b, i, g)),
        out_shape=jax.ShapeDtypeStruct((batch, seq, NSA_WIDTH), F32),
        scratch_shapes=[pltpu.VMEM((2, NSA_REP, tq, LANES), F32),
                        pltpu.VMEM((NSA_REP, tq, 2 * HEAD_DIM), F32),
                        pltpu.VMEM((NSA_REP, tq, SEL_TK), F32),
                        pltpu.VMEM((NSA_REP, tq, SEL_TK), BF16),
                        pltpu.VMEM((NSA_REP, tq, HEAD_DIM), F32),
                        pltpu.VMEM((LANES, tq), F32),
                        pltpu.VMEM((NSA_REP, tq, n_cmp), F32),
                        pltpu.VMEM((NSA_REP, tq, n_cmp), BF16),
                        pltpu.VMEM((3, tq, n_cmp), BF16)],
        compiler_params=_params(("arbitrary", "arbitrary", "arbitrary")),
        name="nsa_attn",
    )(p2, p2, p2, p2, p2, kcv, kcv, p1, gate_bias, bias, overlap_t, membership)


def _out_kernel(x_ref, oa_ref, on_ref, ga_ref, gn_ref, w_ref, o_ref):
    def norm(t, g):
        ms = jnp.mean(t * t, axis=-1, keepdims=True)
        return (t * lax.rsqrt(ms + EPS) * g).astype(BF16)

    ya = norm(oa_ref[...], ga_ref[...])
    yn = norm(on_ref[...], gn_ref[...])
    y = _dot(ya, w_ref[0:A_WIDTH, :]) + _dot(yn, w_ref[A_WIDTH:A_WIDTH + NSA_WIDTH, :])
    o_ref[...] = x_ref[...] + y


def _out_proj(x2, oa, on, gain, w, *, tm):
    n_tok, d = x2.shape
    return pl.pallas_call(
        _out_kernel,
        grid=(n_tok // tm,),
        in_specs=[pl.BlockSpec((tm, d), lambda i: (i, 0)),
                  pl.BlockSpec((tm, A_WIDTH), lambda i: (i, 0)),
                  pl.BlockSpec((tm, NSA_WIDTH), lambda i: (i, 0)),
                  pl.BlockSpec((1, A_WIDTH), lambda i: (0, 0)),
                  pl.BlockSpec((1, NSA_WIDTH), lambda i: (0, 0)),
                  pl.BlockSpec((d, d), lambda i: (0, 0))],
        out_specs=pl.BlockSpec((tm, d), lambda i: (i, 0)),
        out_shape=jax.ShapeDtypeStruct((n_tok, d), F32),
        compiler_params=_params(("arbitrary",)),
        name="out_proj",
    )(x2, oa, on, gain[:A_WIDTH].reshape(1, A_WIDTH), gain[A_WIDTH:].reshape(1, NSA_WIDTH), w)


def _pack_w_in(w_in, gate_bias, q_norm_a, k_norm_a, q_norm_nsa, k_norm_nsa):
    scale = HEAD_DIM ** -0.5
    sizes = (A_WIDTH,) * 3 + (NSA_WIDTH,) + (KV_WIDTH,) * 6 + (N_BRANCH * NSA_HEADS,)
    offs = np.concatenate([[0], np.cumsum(sizes)])
    qa, ka, va, qn, kc, vc, k_s, v_s, k_w, v_w, gl = [w_in[:, offs[i]:offs[i + 1]] for i in range(11)]
    per_grp = NSA_REP * N_BRANCH
    pad = jnp.zeros((w_in.shape[0], LANES - per_grp), w_in.dtype)
    gl_cols = [jnp.concatenate([gl[:, g * per_grp:(g + 1) * per_grp], pad], axis=1) for g in range(NSA_KV_GROUPS)]
    n_pad = P1_HEADS - P1_USED
    w_pad = jnp.zeros((w_in.shape[0], n_pad * LANES), w_in.dtype)
    w = jnp.concatenate([qa, ka, va, kc, vc] + gl_cols + [w_pad, qn, k_s, v_s, k_w, v_w], axis=1).astype(BF16)

    ones = jnp.ones((HEAD_DIM,), F32)
    tile = lambda v, n: jnp.tile(v, n)
    gain = jnp.concatenate([tile(q_norm_a * (scale * LOG2E), A_HEADS), tile(k_norm_a, A_HEADS),
                            tile(ones, A_HEADS + 6 + n_pad),
                            tile(q_norm_nsa * (scale * LOG2E), NSA_HEADS), tile(k_norm_nsa, 2), tile(ones, 2),
                            tile(k_norm_nsa, 2), tile(ones, 2)])
    flag = jnp.concatenate([jnp.ones((2 * A_WIDTH,), F32), jnp.zeros((A_WIDTH + (6 + n_pad) * HEAD_DIM,), F32),
                            jnp.ones((NSA_WIDTH + KV_WIDTH,), F32), jnp.zeros((KV_WIDTH,), F32),
                            jnp.ones((KV_WIDTH,), F32), jnp.zeros((KV_WIDTH,), F32)])
    gb = jnp.stack([jnp.concatenate([gate_bias[g * per_grp:(g + 1) * per_grp],
                                     jnp.zeros((LANES - per_grp,), F32)]) for g in range(NSA_KV_GROUPS)])
    return w, gain, flag, gb.reshape(NSA_KV_GROUPS, 1, LANES)


def _forward(x, ffn1_norm, ffn1_w1, ffn1_w3, ffn1_w2, mix_norm, w_in, gate_bias, q_norm_a, k_norm_a,
             q_norm_nsa, k_norm_nsa, cmp_pos, cmp_k_w1, cmp_k_w2, cmp_v_w1, cmp_v_w2, out_norm, w_out,
             ffn2_norm, ffn2_w1, ffn2_w3, ffn2_w2, rel_bias, *, tm_ffn, tf_ffn, tm_proj, tm_out):
    batch, seq, d = x.shape
    depth = ffn1_w1.shape[0]
    n_tok = batch * seq
    assert seq % DIL_SPAN == 0 and seq // SLC_LEN <= LANES
    assert seq % NSA_TQ == 0 and seq % SEL_TK == 0 and (seq // CMP_STRIDE) % LANES == 0
    assert n_tok % max(tm_ffn, tm_proj, tm_out) == 0 and d == D_MODEL

    bias_a = _expand_bias(rel_bias[:, :A_HEADS], jnp.asarray(_dilated_buckets()), scale=LOG2E)
    bias_a = bias_a.reshape(3, 2, A_HEADS, BAND, 2 * BAND)
    bias_n = _expand_bias(rel_bias[:, A_HEADS:], jnp.asarray(_nsa_buckets()),
                          scale=LOG2E, shift_far=True)
    overlap_t = jnp.asarray(_overlap_t_np(seq // CMP_STRIDE)).astype(BF16)
    membership = jnp.asarray(_block_membership_np(seq)).astype(BF16)

    x2 = x.reshape(n_tok, d)
    for l in range(depth):
        x2 = _ffn(x2, ffn1_norm[l], ffn1_w1[l].astype(BF16), ffn1_w3[l].astype(BF16),
                  ffn1_w2[l].astype(BF16), tm=tm_ffn, tf=tf_ffn)

        w_p, g_p, f_p, gb = _pack_w_in(w_in[l], gate_bias[l], q_norm_a[l], k_norm_a[l],
                                       q_norm_nsa[l], k_norm_nsa[l])
        p1, p2 = _proj(x2, mix_norm[l], w_p, g_p, f_p, P1_HEADS * LANES, tm=tm_proj, tn=1024)
        p1 = p1.reshape(batch, seq, P1_HEADS * LANES)
        p2 = p2.reshape(batch, seq, P2_HEADS * LANES)

        o_a = _dilated(p1, bias_a, batch, seq)

        cmp_w1 = jnp.stack([cmp_k_w1[l], cmp_v_w1[l]]).astype(BF16)
        cmp_w2 = jnp.stack([cmp_k_w2[l], cmp_v_w2[l]]).astype(BF16)
        cmp_gain = jnp.stack([k_norm_nsa[l], jnp.ones((HEAD_DIM,), F32)]).reshape(2, 1, HEAD_DIM)
        cmp_flag = jnp.stack([jnp.ones((HEAD_DIM,), F32), jnp.zeros((HEAD_DIM,), F32)]).reshape(2, 1, HEAD_DIM)
        kcv = _compress(p1, cmp_pos[l], cmp_w1, cmp_w2, cmp_gain, cmp_flag, batch, seq)

        o_n = _nsa(p1, p2, kcv, gb, bias_n, overlap_t, membership, batch, seq)

        x2 = _out_proj(x2, o_a.reshape(n_tok, A_WIDTH), o_n.reshape(n_tok, NSA_WIDTH), out_norm[l],
                       w_out[l].astype(BF16), tm=tm_out)

        x2 = _ffn(x2, ffn2_norm[l], ffn2_w1[l].astype(BF16), ffn2_w3[l].astype(BF16),
                  ffn2_w2[l].astype(BF16), tm=tm_ffn, tf=tf_ffn)
    return x2.reshape(batch, seq, d)


def kernel(x, ffn1_norm, ffn1_w1, ffn1_w3, ffn1_w2, mix_norm, w_in, gate_bias, q_norm_a, k_norm_a, q_norm_nsa, k_norm_nsa, cmp_pos, cmp_k_w1, cmp_k_w2, cmp_v_w1, cmp_v_w2, out_norm, w_out, ffn2_norm, ffn2_w1, ffn2_w3, ffn2_w2, rel_bias):
    return _forward(x, ffn1_norm, ffn1_w1, ffn1_w3, ffn1_w2, mix_norm, w_in, gate_bias, q_norm_a, k_norm_a,
                    q_norm_nsa, k_norm_nsa, cmp_pos, cmp_k_w1, cmp_k_w2, cmp_v_w1, cmp_v_w2, out_norm, w_out,
                    ffn2_norm, ffn2_w1, ffn2_w3, ffn2_w2, rel_bias,
                    tm_ffn=1024, tf_ffn=512, tm_proj=1024, tm_out=512)
```
